```python
import math
import jax, jax.numpy as jnp
from jax import lax
import numpy as np

D_MODEL = 1024
BATCH = 4
SEQ = 4096
DEPTH = 1

CHUNK = 64
Q_BLOCK = 128
SB_HEADS = 8
SB_HEAD_DIM = 64
DA_HEADS = 4
DA_HEAD_DIM = 64
SB_WIDTH = SB_HEADS * SB_HEAD_DIM
DA_WIDTH = DA_HEADS * 2 * DA_HEAD_DIM
IN_WIDTH = 3 * SB_WIDTH + 3 * DA_WIDTH
N_BRANCH = 2
ROPE_THETA = 500000.0
ROPE_DIM = DA_HEAD_DIM // 4
D_FF = ((8 * D_MODEL // 3 + 255) // 256) * 256
EPS = 1e-6
NEG_INF = -1e30

kernel_name = "hybrid_stickbreak_diffattn_gated_block"


def _rmsnorm(x, g):
    xf = x.astype(jnp.float32)
    y = xf * lax.rsqrt(jnp.mean(xf * xf, axis=-1, keepdims=True) + EPS) * g.astype(jnp.float32)
    return y.astype(x.dtype)


def _heads(t, n_heads):
    b, s, _ = t.shape
    return t.reshape(b, s, n_heads, -1).transpose(0, 2, 1, 3)


def _merge_heads(t):
    b, h, s, d = t.shape
    return t.transpose(0, 2, 1, 3).reshape(b, s, h * d)


def _to_blocks(t):
    b, h, s, d = t.shape
    return t.reshape(b, h, s // Q_BLOCK, Q_BLOCK, d).transpose(2, 0, 1, 3, 4)


def _from_blocks(t):
    nb, b, h, qb, d = t.shape
    return t.transpose(1, 2, 0, 3, 4).reshape(b, h, nb * qb, d)


def _partial_rope(t):
    s = t.shape[2]
    pos = jnp.arange(s, dtype=jnp.float32)
    inv_freq = ROPE_THETA ** (-jnp.arange(0, ROPE_DIM, 2, dtype=jnp.float32) / ROPE_DIM)
    ang = pos[:, None] * inv_freq[None, :]
    cos, sin = jnp.cos(ang), jnp.sin(ang)
    tf = t.astype(jnp.float32)
    half = ROPE_DIM // 2
    x1, x2, rest = tf[..., :half], tf[..., half:ROPE_DIM], tf[..., ROPE_DIM:]
    rot = jnp.concatenate([x1 * cos - x2 * sin, x2 * cos + x1 * sin, rest], axis=-1)
    return rot.astype(t.dtype)


def _stick_breaking_attention(q, k, v):
    s_len = q.shape[2]
    nb = s_len // Q_BLOCK
    scale = q.shape[-1] ** -0.5
    kf = k.astype(jnp.float32)
    vf = v.astype(jnp.float32)
    kpos = jnp.arange(s_len)

    def one_block(args):
        qb, t0 = args
        tpos = t0 + jnp.arange(Q_BLOCK)
        mask = kpos[None, :] < tpos[:, None]
        z = jnp.einsum('bhqd,bhkd->bhqk', qb.astype(jnp.float32), kf) * scale
        log_1mb = jnp.where(mask, jax.nn.log_sigmoid(-z), 0.0)
        between = lax.cumsum(log_1mb, axis=3, reverse=True) - log_1mb
        a = jnp.where(mask, jnp.exp(jax.nn.log_sigmoid(z) + between), 0.0)
        return jnp.einsum('bhqk,bhkd->bhqd', a, vf)

    out = lax.map(one_block, (_to_blocks(q), jnp.arange(nb) * Q_BLOCK))
    return _from_blocks(out).astype(v.dtype)


def _diff_attention(q1, q2, k1, k2, v, lam):
    s_len = q1.shape[2]
    nb = s_len // Q_BLOCK
    scale = q1.shape[-1] ** -0.5
    k1f, k2f, vf = k1.astype(jnp.float32), k2.astype(jnp.float32), v.astype(jnp.float32)
    kchunk = jnp.arange(s_len) // CHUNK
    lamf = lam.astype(jnp.float32)

    def one_block(args):
        q1b, q2b, t0 = args
        tchunk = (t0 + jnp.arange(Q_BLOCK)) // CHUNK
        mask = kchunk[None, :] <= tchunk[:, None]
        s1 = jnp.einsum('bhqd,bhkd->bhqk', q1b.astype(jnp.float32), k1f) * scale
        s2 = jnp.einsum('bhqd,bhkd->bhqk', q2b.astype(jnp.float32), k2f) * scale
        p1 = jax.nn.softmax(jnp.where(mask, s1, NEG_INF), axis=-1)
        p2 = jax.nn.softmax(jnp.where(mask, s2, NEG_INF), axis=-1)
        return jnp.einsum('bhqk,bhkd->bhqd', p1 - lamf * p2, vf)

    out = lax.map(one_block, (_to_blocks(q1), _to_blocks(q2), jnp.arange(nb) * Q_BLOCK))
    return _from_blocks(out).astype(v.dtype)


def setup_inputs(seed: int = 0) -> dict:
    key = jax.random.key(seed)
    ks = jax.random.split(key, 20)
    L = DEPTH

    def w(k, shape, fan_in):
        return jax.random.normal(k, shape, jnp.float32) * fan_in ** -0.5

    def gain(k, shape):
        return 1.0 + 0.02 * jax.random.normal(k, shape, jnp.float32)

    return {
        "x": jax.random.normal(ks[0], (BATCH, SEQ, D_MODEL), jnp.float32),
        "g_mix": gain(ks[1], (L, D_MODEL)),
        "w_in": w(ks[2], (L, D_MODEL, IN_WIDTH), D_MODEL),
        "g_q": gain(ks[3], (L, DA_HEAD_DIM)),
        "g_k": gain(ks[4], (L, DA_HEAD_DIM)),
        "lam_q1": 0.1 * jax.random.normal(ks[5], (L, DA_HEAD_DIM), jnp.float32),
        "lam_k1": 0.1 * jax.random.normal(ks[6], (L, DA_HEAD_DIM), jnp.float32),
        "lam_q2": 0.1 * jax.random.normal(ks[7], (L, DA_HEAD_DIM), jnp.float32),
        "lam_k2": 0.1 * jax.random.normal(ks[8], (L, DA_HEAD_DIM), jnp.float32),
        "g_sub": gain(ks[9], (L, 2 * DA_HEAD_DIM)),
        "w_branch_a": w(ks[10], (L, SB_WIDTH, D_MODEL), SB_WIDTH),
        "w_branch_b": w(ks[11], (L, DA_WIDTH, D_MODEL), DA_WIDTH),
        "w_gate": w(ks[12], (L, D_MODEL, N_BRANCH * D_MODEL), D_MODEL),
        "b_gate": 0.02 * jax.random.normal(ks[13], (L, N_BRANCH * D_MODEL), jnp.float32),
        "w_out": w(ks[14], (L, D_MODEL, D_MODEL), D_MODEL),
        "g_ffn": gain(ks[15], (L, D_MODEL)),
        "w_ffn_gate": w(ks[16], (L, D_MODEL, D_FF), D_MODEL),
        "w_ffn_up": w(ks[17], (L, D_MODEL, D_FF), D_MODEL),
        "w_ffn_down": w(ks[18], (L, D_FF, D_MODEL), D_FF),
    }


def reference(x, g_mix, w_in, g_q, g_k, lam_q1, lam_k1, lam_q2, lam_k2, g_sub,
              w_branch_a, w_branch_b, w_gate, b_gate, w_out, g_ffn,
              w_ffn_gate, w_ffn_up, w_ffn_down):
    b, s, _ = x.shape
    for layer in range(DEPTH):
        lambda_init = 0.8 - 0.6 * math.exp(-0.3 * layer)

        h = _rmsnorm(x, g_mix[layer])
        proj = jnp.einsum('bsd,de->bse', h, w_in[layer])
        sb_q, sb_k, sb_v, da_q, da_k, da_v = jnp.split(
            proj, np.cumsum([SB_WIDTH, SB_WIDTH, SB_WIDTH, DA_WIDTH, DA_WIDTH]).tolist(), axis=-1)

        o_a = _stick_breaking_attention(_heads(sb_q, SB_HEADS), _heads(sb_k, SB_HEADS),
                                        _heads(sb_v, SB_HEADS))
        o_a = _merge_heads(o_a)

        qh = _heads(da_q, DA_HEADS)
        kh = _heads(da_k, DA_HEADS)
        vh = _heads(da_v, DA_HEADS)
        q1 = _partial_rope(_rmsnorm(qh[..., :DA_HEAD_DIM], g_q[layer]))
        q2 = _partial_rope(_rmsnorm(qh[..., DA_HEAD_DIM:], g_q[layer]))
        k1 = _partial_rope(_rmsnorm(kh[..., :DA_HEAD_DIM], g_k[layer]))
        k2 = _partial_rope(_rmsnorm(kh[..., DA_HEAD_DIM:], g_k[layer]))
        lam = (jnp.exp(jnp.sum(lam_q1[layer].astype(jnp.float32) * lam_k1[layer].astype(jnp.float32)))
               - jnp.exp(jnp.sum(lam_q2[layer].astype(jnp.float32) * lam_k2[layer].astype(jnp.float32)))
               + lambda_init)
        o_b = _diff_attention(q1, q2, k1, k2, vh, lam)
        o_b = _rmsnorm(o_b, g_sub[layer]) * (1.0 - lambda_init)
        o_b = _merge_heads(o_b.astype(x.dtype))

        gates = jax.nn.sigmoid(jnp.einsum('bsd,de->bse', h, w_gate[layer]) + b_gate[layer])
        gates = gates.reshape(b, s, N_BRANCH, D_MODEL)
        br_a = jnp.einsum('bsc,cd->bsd', o_a, w_branch_a[layer])
        br_b = jnp.einsum('bsc,cd->bsd', o_b, w_branch_b[layer])
        merged = gates[:, :, 0] * br_a + gates[:, :, 1] * br_b
        x = x + jnp.einsum('bsd,de->bse', merged, w_out[layer])

        h2 = _rmsnorm(x, g_ffn[layer])
        ff = jax.nn.silu(jnp.einsum('bsd,df->bsf', h2, w_ffn_gate[layer])) * \
            jnp.einsum('bsd,df->bsf', h2, w_ffn_up[layer])
        x = x + jnp.einsum('bsf,fd->bsd', ff, w_ffn_down[layer])
    return x
```

```python
import functools
import math

import jax
import jax.numpy as jnp
from jax import lax
from jax.experimental import pallas as pl
from jax.experimental.pallas import tpu as pltpu

F32 = jnp.float32
BF16 = jnp.bfloat16

CHUNK = 64
SB_HEADS = 8
DA_HEADS = 4
HEAD_DIM = 64
ROPE_THETA = 500000.0
ROPE_DIM = HEAD_DIM // 4
EPS = 1e-6
NEG_INF = -1e30
LANES = 128
VMEM_LIMIT = 56 * 1024 * 1024


def _dot(a, b):
    return jnp.dot(a, b, preferred_element_type=F32)


def _dot_nt(a, b):
    return lax.dot_general(a, b, (((1,), (1,)), ((), ())), preferred_element_type=F32)


def _split_bf16(x):
    hi = x.astype(BF16)
    lo = (x - hi.astype(F32)).astype(BF16)
    return hi, lo


def _proj_kernel(x_ref, gmix_ref, win_ref, gq_ref, gk_ref, gmat_ref,
                 ra_ref, rm_ref, rp_ref, out_ref, *, sb_w, da_w):
    x = x_ref[...]
    ms = jnp.mean(x * x, axis=-1, keepdims=True)
    h = x * lax.rsqrt(ms + EPS) * gmix_ref[...]
    proj = _dot(h.astype(BF16), win_ref[...])
    scale = HEAD_DIM ** -0.5

    out_ref[:, 0:sb_w] = (proj[:, 0:sb_w] * scale).astype(BF16)
    out_ref[:, sb_w:3 * sb_w] = proj[:, sb_w:3 * sb_w].astype(BF16)

    ra, rm, rp = ra_ref[...], rm_ref[...], rp_ref[...]

    def qk_norm_rope(t, g, mult):
        hi, lo = _split_bf16(t * t)
        gm = gmat_ref[...]
        msq = _dot(hi, gm) + _dot(lo, gm)
        tn = t * lax.rsqrt(msq + EPS) * g
        cols = []
        for j in range(da_w // LANES):
            c = tn[:, j * LANES:(j + 1) * LANES]
            r = (c * ra + pltpu.roll(c, LANES - ROPE_DIM // 2, 1) * rm
                 + pltpu.roll(c, ROPE_DIM // 2, 1) * rp)
            cols.append((r * mult).astype(BF16))
        return jnp.concatenate(cols, axis=1)

    o = 3 * sb_w
    out_ref[:, o:o + da_w] = qk_norm_rope(proj[:, o:o + da_w], gq_ref[...], scale)
    out_ref[:, o + da_w:o + 2 * da_w] = qk_norm_rope(proj[:, o + da_w:o + 2 * da_w], gk_ref[...], 1.0)
    out_ref[:, o + 2 * da_w:o + 3 * da_w] = proj[:, o + 2 * da_w:o + 3 * da_w].astype(BF16)


def _sb_kernel(q_ref, k_ref, v_ref, o_ref, acc_ref, run_ref, *, tq, tk):
    i = pl.program_id(2)
    lane = lax.broadcasted_iota(jnp.int32, (tq, LANES), 1)
    row = lax.broadcasted_iota(jnp.int32, (tq, tk), 0)
    col = lax.broadcasted_iota(jnp.int32, (tq, tk), 1)
    r2 = lax.broadcasted_iota(jnp.int32, (2 * tk, tk + LANES), 0) % tk
    c2 = lax.broadcasted_iota(jnp.int32, (2 * tk, tk + LANES), 1)
    tri = jnp.where((c2 >= tk) | (r2 > c2), 1.0, 0.0).astype(BF16)

    q = q_ref[...]
    n_kt = (i + 1) * (tq // tk)
    outs = []
    for hh in range(LANES // HEAD_DIM):
        in_head = (lane >= hh * HEAD_DIM) & (lane < (hh + 1) * HEAD_DIM)
        qh = jnp.where(in_head, q, jnp.zeros_like(q))
        acc_ref[...] = jnp.zeros_like(acc_ref)
        run_ref[...] = jnp.zeros_like(run_ref)

        def body(it, _):
            kt = n_kt - 1 - it
            ks = pl.multiple_of(kt * tk, tk)
            kk = k_ref[pl.ds(ks, tk), :]
            vv = v_ref[pl.ds(ks, tk), :]
            z = _dot_nt(qh, kk)
            mask = (ks + col) < (i * tq + row)
            sp = jnp.log(1.0 + jnp.exp(-jnp.abs(z)))
            l1m = jnp.where(mask, jnp.minimum(-z, 0.0) - sp, 0.0)
            hi, lo = _split_bf16(l1m)
            cs = _dot(jnp.concatenate([hi, lo], axis=1), tri)
            between = cs[:, :tk] + run_ref[...]
            a = jnp.where(mask, jnp.exp(z + l1m + between), 0.0)
            acc_ref[...] += _dot(a.astype(BF16), vv)
            run_ref[...] += cs[:, tk:]
            return 0

        lax.fori_loop(0, n_kt, body, 0)
        outs.append(acc_ref[...])
    o_ref[...] = jnp.where(lane < HEAD_DIM, outs[0], outs[1]).astype(o_ref.dtype)


def _da_kernel(q_ref, k_ref, v_ref, lq1_ref, lk1_ref, lq2_ref, lk2_ref, gsub_ref, o_ref,
               m_ref, l_ref, acc_ref, *, tq, tk, lambda_init):
    i = pl.program_id(2)
    lane = lax.broadcasted_iota(jnp.int32, (tq, LANES), 1)
    row = lax.broadcasted_iota(jnp.int32, (tq, tk), 0)
    col = lax.broadcasted_iota(jnp.int32, (tq, tk), 1)
    q = q_ref[...]
    n_kt = (i + 1) * (tq // tk)
    reps = tk // LANES
    res = []
    for c in range(2):
        in_half = (lane >= c * HEAD_DIM) & (lane < (c + 1) * HEAD_DIM)
        qc = jnp.where(in_half, q, jnp.zeros_like(q))
        m_ref[...] = jnp.full_like(m_ref, NEG_INF)
        l_ref[...] = jnp.zeros_like(l_ref)
        acc_ref[...] = jnp.zeros_like(acc_ref)

        def body(kt, _):
            ks = pl.multiple_of(kt * tk, tk)
            kk = k_ref[pl.ds(ks, tk), :]
            vv = v_ref[pl.ds(ks, tk), :]
            s = _dot_nt(qc, kk)
            mask = ((ks + col) // CHUNK) <= ((i * tq + row) // CHUNK)
            s = jnp.where(mask, s, NEG_INF)
            m_prev = m_ref[...]
            m_new = jnp.maximum(m_prev, jnp.max(s, axis=1, keepdims=True))
            p = jnp.exp(s - pltpu.repeat(m_new, reps, axis=1))
            alpha = jnp.exp(m_prev - m_new)
            l_ref[...] = alpha * l_ref[...] + jnp.sum(p, axis=1, keepdims=True)
            acc_ref[...] = alpha * acc_ref[...] + _dot(p.astype(BF16), vv)
            m_ref[...] = m_new
            return 0

        lax.fori_loop(0, n_kt, body, 0)
        res.append(acc_ref[...] / l_ref[...])

    lam = (jnp.exp(jnp.sum(lq1_ref[...] * lk1_ref[...], axis=-1, keepdims=True))
           - jnp.exp(jnp.sum(lq2_ref[...] * lk2_ref[...], axis=-1, keepdims=True))
           + lambda_init)
    o = res[0] - lam * res[1]
    ms = jnp.mean(o * o, axis=-1, keepdims=True)
    o = o * lax.rsqrt(ms + EPS) * gsub_ref[...] * (1.0 - lambda_init)
    o_ref[...] = o.astype(o_ref.dtype)


def _post_kernel(x_ref, oa_ref, ob_ref, gmix_ref, wgate_ref, bgate_ref, wa_ref, wb_ref, wout_ref,
                 gffn_ref, wfg_ref, wfu_ref, wfd_ref, out_ref, *, d_model):
    x = x_ref[...]
    ms = jnp.mean(x * x, axis=-1, keepdims=True)
    h = (x * lax.rsqrt(ms + EPS) * gmix_ref[...]).astype(BF16)
    gates = jax.nn.sigmoid(_dot(h, wgate_ref[...]) + bgate_ref[...])
    br_a = _dot(oa_ref[...], wa_ref[...])
    br_b = _dot(ob_ref[...], wb_ref[...])
    merged = gates[:, :d_model] * br_a + gates[:, d_model:] * br_b
    x1 = x + _dot(merged.astype(BF16), wout_ref[...])

    ms2 = jnp.mean(x1 * x1, axis=-1, keepdims=True)
    h2 = (x1 * lax.rsqrt(ms2 + EPS) * gffn_ref[...]).astype(BF16)
    fg = _dot(h2, wfg_ref[...])
    fu = _dot(h2, wfu_ref[...])
    ff = (fg * jax.nn.sigmoid(fg)) * fu
    out_ref[...] = x1 + _dot(ff.astype(BF16), wfd_ref[...])


def _rope_tables(seq):
    half = ROPE_DIM // 2
    pos = jnp.arange(seq, dtype=F32)
    inv_freq = ROPE_THETA ** (-jnp.arange(0, ROPE_DIM, 2, dtype=F32) / ROPE_DIM)
    ang = pos[:, None] * inv_freq[None, :]
    cos, sin = jnp.cos(ang), jnp.sin(ang)
    ones = jnp.ones((seq, HEAD_DIM - ROPE_DIM), F32)
    zeros = jnp.zeros((seq, HEAD_DIM - ROPE_DIM), F32)
    zh = jnp.zeros((seq, half), F32)
    ra = jnp.concatenate([cos, cos, ones], axis=1)
    rm = jnp.concatenate([-sin, zh, zeros], axis=1)
    rp = jnp.concatenate([zh, sin, zeros], axis=1)
    tile = lambda t: jnp.concatenate([t] * (LANES // HEAD_DIM), axis=1)
    return tile(ra), tile(rm), tile(rp)


def kernel(x, g_mix, w_in, g_q, g_k, lam_q1, lam_k1, lam_q2, lam_k2, g_sub, w_branch_a, w_branch_b,
           w_gate, b_gate, w_out, g_ffn, w_ffn_gate, w_ffn_up, w_ffn_down):
    b, s, d = x.shape
    depth = g_mix.shape[0]
    sb_w = SB_HEADS * HEAD_DIM
    da_w = DA_HEADS * 2 * HEAD_DIM
    in_w = 3 * sb_w + 3 * da_w
    n = b * s
    d_ff = w_ffn_gate.shape[-1]

    tm = 512
    tq, tk = 256, 128
    assert s % tm == 0 and s % tq == 0 and tq % tk == 0 and tk % CHUNK == 0

    ra, rm, rp = _rope_tables(s)
    grp = jnp.arange(da_w) // HEAD_DIM
    gmat = jnp.where(grp[:, None] == grp[None, :], 1.0 / HEAD_DIM, 0.0).astype(BF16)
    tile_g = lambda g: jnp.tile(g, da_w // HEAD_DIM)[None, :]
    vmem_full = pl.BlockSpec(memory_space=pltpu.VMEM)
    nq = s // tq
    spt = s // tm

    xf = x.reshape(n, d)
    for layer in range(depth):
        lambda_init = 0.8 - 0.6 * math.exp(-0.3 * layer)
        gmix = g_mix[layer][None, :]

        proj = pl.pallas_call(
            functools.partial(_proj_kernel, sb_w=sb_w, da_w=da_w),
            grid=(n // tm,),
            in_specs=[
                pl.BlockSpec((tm, d), lambda t: (t, 0)),
                vmem_full, vmem_full, vmem_full, vmem_full, vmem_full,
                pl.BlockSpec((tm, LANES), lambda t: (t % spt, 0)),
                pl.BlockSpec((tm, LANES), lambda t: (t % spt, 0)),
                pl.BlockSpec((tm, LANES), lambda t: (t % spt, 0)),
            ],
            out_specs=pl.BlockSpec((tm, in_w), lambda t: (t, 0)),
            out_shape=jax.ShapeDtypeStruct((n, in_w), BF16),
            compiler_params=pltpu.CompilerParams(
                dimension_semantics=("arbitrary",), vmem_limit_bytes=VMEM_LIMIT),
            name="proj",
        )(xf, gmix, w_in[layer].astype(BF16), tile_g(g_q[layer]), tile_g(g_k[layer]), gmat,
          ra, rm, rp)

        cb = lambda off: off // LANES
        o_a = pl.pallas_call(
            functools.partial(_sb_kernel, tq=tq, tk=tk),
            grid=(b, sb_w // LANES, nq),
            in_specs=[
                pl.BlockSpec((tq, LANES), lambda bi, j, i: (bi * nq + i, j)),
                pl.BlockSpec((s, LANES), lambda bi, j, i: (bi, cb(sb_w) + j)),
                pl.BlockSpec((s, LANES), lambda bi, j, i: (bi, cb(2 * sb_w) + j)),
            ],
            out_specs=pl.BlockSpec((tq, LANES), lambda bi, j, i: (bi * nq + i, j)),
            out_shape=jax.ShapeDtypeStruct((n, sb_w), BF16),
            scratch_shapes=[pltpu.VMEM((tq, LANES), F32), pltpu.VMEM((tq, LANES), F32)],
            compiler_params=pltpu.CompilerParams(
                dimension_semantics=("arbitrary", "arbitrary", "arbitrary"),
                vmem_limit_bytes=VMEM_LIMIT),
            name="sb_attn",
        )(proj, proj, proj)

        o_b = pl.pallas_call(
            functools.partial(_da_kernel, tq=tq, tk=tk, lambda_init=lambda_init),
            grid=(b, DA_HEADS, nq),
            in_specs=[
                pl.BlockSpec((tq, LANES), lambda bi, j, i: (bi * nq + i, cb(3 * sb_w) + j)),
                pl.BlockSpec((s, LANES), lambda bi, j, i: (bi, cb(3 * sb_w + da_w) + j)),
                pl.BlockSpec((s, LANES), lambda bi, j, i: (bi, cb(3 * sb_w + 2 * da_w) + j)),
                vmem_full, vmem_full, vmem_full, vmem_full, vmem_full,
            ],
            out_specs=pl.BlockSpec((tq, LANES), lambda bi, j, i: (bi * nq + i, j)),
            out_shape=jax.ShapeDtypeStruct((n, da_w), BF16),
            scratch_shapes=[pltpu.VMEM((tq, LANES), F32), pltpu.VMEM((tq, LANES), F32),
                            pltpu.VMEM((tq, LANES), F32)],
            compiler_params=pltpu.CompilerParams(
                dimension_semantics=("arbitrary", "arbitrary", "arbitrary"),
                vmem_limit_bytes=VMEM_LIMIT),
            name="da_attn",
        )(proj, proj, proj, lam_q1[layer][None, :], lam_k1[layer][None, :],
          lam_q2[layer][None, :], lam_k2[layer][None, :], g_sub[layer][None, :])

        tp = 256
        xf = pl.pallas_call(
            functools.partial(_post_kernel, d_model=d),
            grid=(n // tp,),
            in_specs=[
                pl.BlockSpec((tp, d), lambda t: (t, 0)),
                pl.BlockSpec((tp, sb_w), lambda t: (t, 0)),
                pl.BlockSpec((tp, da_w), lambda t: (t, 0)),
            ] + [vmem_full] * 10,
            out_specs=pl.BlockSpec((tp, d), lambda t: (t, 0)),
            out_shape=jax.ShapeDtypeStruct((n, d), F32),
            compiler_params=pltpu.CompilerParams(
                dimension_semantics=("arbitrary",), vmem_limit_bytes=VMEM_LIMIT),
            name="post",
        )(xf, o_a, o_b, gmix, w_gate[layer].astype(BF16), b_gate[layer][None, :],
          w_branch_a[layer].astype(BF16), w_branch_b[layer].astype(BF16),
          w_out[layer].astype(BF16), g_ffn[layer][None, :],
          w_ffn_gate[layer].astype(BF16), w_ffn_up[layer].astype(BF16),
          w_ffn_down[layer].astype(BF16))
    return xf.reshape(b, s, d)
```

```python
import functools
import math

import jax
import jax.numpy as jnp
from jax import lax
from jax.experimental import pallas as pl
from jax.experimental.pallas import tpu as pltpu

F32 = jnp.float32
BF16 = jnp.bfloat16

CHUNK = 64
SB_HEADS = 8
DA_HEADS = 4
HEAD_DIM = 64
ROPE_THETA = 500000.0
ROPE_DIM = HEAD_DIM // 4
EPS = 1e-6
NEG_INF = -1e30
LANES = 128
VMEM_LIMIT = 56 * 1024 * 1024


def _dot(a, b):
    return jnp.dot(a, b, preferred_element_type=F32)


def _dot_nt(a, b):
    return lax.dot_general(a, b, (((1,), (1,)), ((), ())), preferred_element_type=F32)


def _split_bf16(x):
    hi = x.astype(BF16)
    lo = (x - hi.astype(F32)).astype(BF16)
    return hi, lo


def _proj_kernel(x_ref, gmix_ref, win_ref, gq_ref, gk_ref, gmat_ref,
                 ra_ref, rm_ref, rp_ref, out_ref, *, sb_w, da_w):
    x = x_ref[...]
    ms = jnp.mean(x * x, axis=-1, keepdims=True)
    h = x * lax.rsqrt(ms + EPS) * gmix_ref[...]
    proj = _dot(h.astype(BF16), win_ref[...])
    scale = HEAD_DIM ** -0.5

    out_ref[:, 0:sb_w] = (proj[:, 0:sb_w] * scale).astype(BF16)
    out_ref[:, sb_w:3 * sb_w] = proj[:, sb_w:3 * sb_w].astype(BF16)

    ra, rm, rp = ra_ref[...], rm_ref[...], rp_ref[...]

    def qk_norm_rope(t, g, mult):
        hi, lo = _split_bf16(t * t)
        gm = gmat_ref[...]
        msq = _dot(hi, gm) + _dot(lo, gm)
        tn = t * lax.rsqrt(msq + EPS) * g
        cols = []
        for j in range(da_w // LANES):
            c = tn[:, j * LANES:(j + 1) * LANES]
            r = (c * ra + pltpu.roll(c, LANES - ROPE_DIM // 2, 1) * rm
                 + pltpu.roll(c, ROPE_DIM // 2, 1) * rp)
            cols.append((r * mult).astype(BF16))
        return jnp.concatenate(cols, axis=1)

    o = 3 * sb_w
    out_ref[:, o:o + da_w] = qk_norm_rope(proj[:, o:o + da_w], gq_ref[...], scale)
    out_ref[:, o + da_w:o + 2 * da_w] = qk_norm_rope(proj[:, o + da_w:o + 2 * da_w], gk_ref[...], 1.0)
    out_ref[:, o + 2 * da_w:o + 3 * da_w] = proj[:, o + 2 * da_w:o + 3 * da_w].astype(BF16)


def _stack_masked(q, n_parts):
    lane = lax.broadcasted_iota(jnp.int32, q.shape, 1)
    zero = jnp.zeros_like(q)
    return jnp.concatenate(
        [jnp.where((lane >= p * HEAD_DIM) & (lane < (p + 1) * HEAD_DIM), q, zero)
         for p in range(n_parts)], axis=0)


def _sb_kernel(q_ref, k_ref, v_ref, o_ref, acc_ref, run_ref, *, tq):
    i = pl.program_id(2)
    sub = LANES
    n_sub = tq // sub
    m2 = 2 * tq
    r2 = lax.broadcasted_iota(jnp.int32, (2 * sub, 2 * sub), 0) % sub
    c2 = lax.broadcasted_iota(jnp.int32, (2 * sub, 2 * sub), 1)
    tri = jnp.where((c2 >= sub) | (r2 > c2), 1.0, 0.0).astype(BF16)

    qs = _stack_masked(q_ref[...], 2)
    acc_ref[...] = jnp.zeros_like(acc_ref)
    run_ref[...] = jnp.zeros_like(run_ref)

    def block(ks, masked):
        kk = k_ref[pl.ds(ks, tq), :]
        vv = v_ref[pl.ds(ks, tq), :]
        z = _dot_nt(qs, kk)
        sp = jnp.log(1.0 + jnp.exp(-jnp.abs(z)))
        l1m = jnp.minimum(-z, 0.0) - sp
        if masked:
            row = lax.broadcasted_iota(jnp.int32, (m2, tq), 0)
            col = lax.broadcasted_iota(jnp.int32, (m2, tq), 1)
            mask = col < jnp.where(row >= tq, row - tq, row)
            l1m = jnp.where(mask, l1m, 0.0)
        hi, lo = _split_bf16(l1m)
        run = run_ref[...]
        betw = [None] * n_sub
        for c in reversed(range(n_sub)):
            sl = slice(c * sub, (c + 1) * sub)
            cs = _dot(jnp.concatenate([hi[:, sl], lo[:, sl]], axis=1), tri)
            betw[c] = cs[:, :sub] + run
            run = run + cs[:, sub:]
        run_ref[...] = run
        a = jnp.exp(z + l1m + jnp.concatenate(betw, axis=1))
        if masked:
            a = jnp.where(mask, a, 0.0)
        acc_ref[...] += _dot(a.astype(BF16), vv)

    block(pl.multiple_of(i * tq, tq), True)

    def body(it, _):
        block(pl.multiple_of((i - 1 - it) * tq, tq), False)
        return 0

    lax.fori_loop(0, i, body, 0)
    lane = lax.broadcasted_iota(jnp.int32, (tq, LANES), 1)
    o_ref[...] = jnp.where(lane < HEAD_DIM, acc_ref[0:tq, :], acc_ref[tq:m2, :]).astype(o_ref.dtype)


def _da_kernel(q_ref, k_ref, v_ref, lq1_ref, lk1_ref, lq2_ref, lk2_ref, gsub_ref, o_ref,
               vext_ref, m_ref, l_ref, acc_ref, *, tq, lambda_init):
    i = pl.program_id(2)
    m2 = 2 * tq
    reps = tq // LANES

    @pl.when(i == 0)
    def _():
        vext_ref[:, 0:LANES] = v_ref[...]
        vext_ref[:, LANES:2 * LANES] = jnp.ones(v_ref.shape, v_ref.dtype)

    qs = _stack_masked(q_ref[...], 2)
    m_ref[...] = jnp.full_like(m_ref, NEG_INF)
    l_ref[...] = jnp.zeros_like(l_ref)
    acc_ref[...] = jnp.zeros_like(acc_ref)

    def block(ks, masked):
        kk = k_ref[pl.ds(ks, tq), :]
        vv = vext_ref[pl.ds(ks, tq), :]
        s = _dot_nt(qs, kk)
        if masked:
            row = lax.broadcasted_iota(jnp.int32, (m2, tq), 0)
            col = lax.broadcasted_iota(jnp.int32, (m2, tq), 1)
            qrow = jnp.where(row >= tq, row - tq, row)
            s = jnp.where((col // CHUNK) <= (qrow // CHUNK), s, NEG_INF)
        m_prev = m_ref[...]
        m_new = jnp.maximum(m_prev, jnp.max(s, axis=1, keepdims=True))
        p = jnp.exp(s - pltpu.repeat(m_new, reps, axis=1))
        alpha = jnp.exp(m_prev - m_new)
        pv = _dot(p.astype(BF16), vv)
        acc_ref[...] = alpha * acc_ref[...] + pv[:, :LANES]
        l_ref[...] = alpha * l_ref[...] + pv[:, LANES:]
        m_ref[...] = m_new

    block(pl.multiple_of(i * tq, tq), True)

    def body(it, _):
        block(pl.multiple_of(it * tq, tq), False)
        return 0

    lax.fori_loop(0, i, body, 0)

    res = acc_ref[...] / l_ref[...]
    lam = (jnp.exp(jnp.sum(lq1_ref[...] * lk1_ref[...], axis=-1, keepdims=True))
           - jnp.exp(jnp.sum(lq2_ref[...] * lk2_ref[...], axis=-1, keepdims=True))
           + lambda_init)
    o = res[0:tq, :] - lam * res[tq:m2, :]
    ms = jnp.mean(o * o, axis=-1, keepdims=True)
    o = o * lax.rsqrt(ms + EPS) * gsub_ref[...] * (1.0 - lambda_init)
    o_ref[...] = o.astype(o_ref.dtype)


def _post_kernel(x_ref, oa_ref, ob_ref, gmix_ref, wgate_ref, bgate_ref, wa_ref, wb_ref, wout_ref,
                 gffn_ref, wfg_ref, wfu_ref, wfd_ref, out_ref, *, d_model):
    x = x_ref[...]
    ms = jnp.mean(x * x, axis=-1, keepdims=True)
    h = (x * lax.rsqrt(ms + EPS) * gmix_ref[...]).astype(BF16)
    gates = jax.nn.sigmoid(_dot(h, wgate_ref[...]) + bgate_ref[...])
    br_a = _dot(oa_ref[...], wa_ref[...])
    br_b = _dot(ob_ref[...], wb_ref[...])
    merged = gates[:, :d_model] * br_a + gates[:, d_model:] * br_b
    x1 = x + _dot(merged.astype(BF16), wout_ref[...])

    ms2 = jnp.mean(x1 * x1, axis=-1, keepdims=True)
    h2 = (x1 * lax.rsqrt(ms2 + EPS) * gffn_ref[...]).astype(BF16)
    fg = _dot(h2, wfg_ref[...])
    fu = _dot(h2, wfu_ref[...])
    ff = (fg * jax.nn.sigmoid(fg)) * fu
    out_ref[...] = x1 + _dot(ff.astype(BF16), wfd_ref[...])


def _rope_tables(seq):
    half = ROPE_DIM // 2
    pos = jnp.arange(seq, dtype=F32)
    inv_freq = ROPE_THETA ** (-jnp.arange(0, ROPE_DIM, 2, dtype=F32) / ROPE_DIM)
    ang = pos[:, None] * inv_freq[None, :]
    cos, sin = jnp.cos(ang), jnp.sin(ang)
    ones = jnp.ones((seq, HEAD_DIM - ROPE_DIM), F32)
    zeros = jnp.zeros((seq, HEAD_DIM - ROPE_DIM), F32)
    zh = jnp.zeros((seq, half), F32)
    ra = jnp.concatenate([cos, cos, ones], axis=1)
    rm = jnp.concatenate([-sin, zh, zeros], axis=1)
    rp = jnp.concatenate([zh, sin, zeros], axis=1)
    tile = lambda t: jnp.concatenate([t] * (LANES // HEAD_DIM), axis=1)
    return tile(ra), tile(rm), tile(rp)


def kernel(x, g_mix, w_in, g_q, g_k, lam_q1, lam_k1, lam_q2, lam_k2, g_sub, w_branch_a, w_branch_b,
           w_gate, b_gate, w_out, g_ffn, w_ffn_gate, w_ffn_up, w_ffn_down):
    b, s, d = x.shape
    depth = g_mix.shape[0]
    sb_w = SB_HEADS * HEAD_DIM
    da_w = DA_HEADS * 2 * HEAD_DIM
    in_w = 3 * sb_w + 3 * da_w
    n = b * s
    d_ff = w_ffn_gate.shape[-1]

    tm = 512
    tq = 256
    assert s % tm == 0 and s % tq == 0 and tq % LANES == 0 and tq % CHUNK == 0

    ra, rm, rp = _rope_tables(s)
    grp = jnp.arange(da_w) // HEAD_DIM
    gmat = jnp.where(grp[:, None] == grp[None, :], 1.0 / HEAD_DIM, 0.0).astype(BF16)
    tile_g = lambda g: jnp.tile(g, da_w // HEAD_DIM)[None, :]
    vmem_full = pl.BlockSpec(memory_space=pltpu.VMEM)
    nq = s // tq
    spt = s // tm

    xf = x.reshape(n, d)
    for layer in range(depth):
        lambda_init = 0.8 - 0.6 * math.exp(-0.3 * layer)
        gmix = g_mix[layer][None, :]

        proj = pl.pallas_call(
            functools.partial(_proj_kernel, sb_w=sb_w, da_w=da_w),
            grid=(n // tm,),
            in_specs=[
                pl.BlockSpec((tm, d), lambda t: (t, 0)),
                vmem_full, vmem_full, vmem_full, vmem_full, vmem_full,
                pl.BlockSpec((tm, LANES), lambda t: (t % spt, 0)),
                pl.BlockSpec((tm, LANES), lambda t: (t % spt, 0)),
                pl.BlockSpec((tm, LANES), lambda t: (t % spt, 0)),
            ],
            out_specs=pl.BlockSpec((tm, in_w), lambda t: (t, 0)),
            out_shape=jax.ShapeDtypeStruct((n, in_w), BF16),
            compiler_params=pltpu.CompilerParams(
                dimension_semantics=("arbitrary",), vmem_limit_bytes=VMEM_LIMIT),
            name="proj",
        )(xf, gmix, w_in[layer].astype(BF16), tile_g(g_q[layer]), tile_g(g_k[layer]), gmat,
          ra, rm, rp)

        cb = lambda off: off // LANES
        o_a = pl.pallas_call(
            functools.partial(_sb_kernel, tq=tq),
            grid=(b, sb_w // LANES, nq),
            in_specs=[
                pl.BlockSpec((tq, LANES), lambda bi, j, i: (bi * nq + i, j)),
                pl.BlockSpec((s, LANES), lambda bi, j, i: (bi, cb(sb_w) + j)),
                pl.BlockSpec((s, LANES), lambda bi, j, i: (bi, cb(2 * sb_w) + j)),
            ],
            out_specs=pl.BlockSpec((tq, LANES), lambda bi, j, i: (bi * nq + i, j)),
            out_shape=jax.ShapeDtypeStruct((n, sb_w), BF16),
            scratch_shapes=[pltpu.VMEM((2 * tq, LANES), F32), pltpu.VMEM((2 * tq, LANES), F32)],
            compiler_params=pltpu.CompilerParams(
                dimension_semantics=("arbitrary", "arbitrary", "arbitrary"),
                vmem_limit_bytes=VMEM_LIMIT),
            name="sb_attn",
        )(proj, proj, proj)

        o_b = pl.pallas_call(
            functools.partial(_da_kernel, tq=tq, lambda_init=lambda_init),
            grid=(b, DA_HEADS, nq),
            in_specs=[
                pl.BlockSpec((tq, LANES), lambda bi, j, i: (bi * nq + i, cb(3 * sb_w) + j)),
                pl.BlockSpec((s, LANES), lambda bi, j, i: (bi, cb(3 * sb_w + da_w) + j)),
                pl.BlockSpec((s, LANES), lambda bi, j, i: (bi, cb(3 * sb_w + 2 * da_w) + j)),
                vmem_full, vmem_full, vmem_full, vmem_full, vmem_full,
            ],
            out_specs=pl.BlockSpec((tq, LANES), lambda bi, j, i: (bi * nq + i, j)),
            out_shape=jax.ShapeDtypeStruct((n, da_w), BF16),
            scratch_shapes=[pltpu.VMEM((s, 2 * LANES), BF16)]
            + [pltpu.VMEM((2 * tq, LANES), F32)] * 3,
            compiler_params=pltpu.CompilerParams(
                dimension_semantics=("arbitrary", "arbitrary", "arbitrary"),
                vmem_limit_bytes=VMEM_LIMIT),
            name="da_attn",
        )(proj, proj, proj, lam_q1[layer][None, :], lam_k1[layer][None, :],
          lam_q2[layer][None, :], lam_k2[layer][None, :], g_sub[layer][None, :])

        tp = 256
        xf = pl.pallas_call(
            functools.partial(_post_kernel, d_model=d),
            grid=(n // tp,),
            in_specs=[
                pl.BlockSpec((tp, d), lambda t: (t, 0)),
                pl.BlockSpec((tp, sb_w), lambda t: (t, 0)),
                pl.BlockSpec((tp, da_w), lambda t: (t, 0)),
            ] + [vmem_full] * 10,
            out_specs=pl.BlockSpec((tp, d), lambda t: (t, 0)),
            out_shape=jax.ShapeDtypeStruct((n, d), F32),
            compiler_params=pltpu.CompilerParams(
                dimension_semantics=("arbitrary",), vmem_limit_bytes=VMEM_LIMIT),
            name="post",
        )(xf, o_a, o_b, gmix, w_gate[layer].astype(BF16), b_gate[layer][None, :],
          w_branch_a[layer].astype(BF16), w_branch_b[layer].astype(BF16),
          w_out[layer].astype(BF16), g_ffn[layer][None, :],
          w_ffn_gate[layer].astype(BF16), w_ffn_up[layer].astype(BF16),
          w_ffn_down[layer].astype(BF16))
    return xf.reshape(b, s, d)
```

```python
import functools
import math

import jax
import jax.numpy as jnp
from jax import lax
from jax.experimental import pallas as pl
from jax.experimental.pallas import tpu as pltpu

F32 = jnp.float32
BF16 = jnp.bfloat16

CHUNK = 64
SB_HEADS = 8
DA_HEADS = 4
HEAD_DIM = 64
ROPE_THETA = 500000.0
ROPE_DIM = HEAD_DIM // 4
EPS = 1e-6
NEG_INF = -1e30
LOG2E = math.log2(math.e)
LANES = 128
VMEM_LIMIT = 56 * 1024 * 1024


def _dot(a, b):
    return jnp.dot(a, b, preferred_element_type=F32)


def _dot_nt(a, b):
    return lax.dot_general(a, b, (((1,), (1,)), ((), ())), preferred_element_type=F32)


def _split_bf16(x):
    hi = x.astype(BF16)
    lo = (x - hi.astype(F32)).astype(BF16)
    return hi, lo


def _proj_kernel(x_ref, gmix_ref, win_ref, gq_ref, gk_ref, gmat_ref,
                 ra_ref, rm_ref, rp_ref, out_ref, *, sb_w, da_w):
    x = x_ref[...]
    ms = jnp.mean(x * x, axis=-1, keepdims=True)
    h = x * lax.rsqrt(ms + EPS) * gmix_ref[...]
    proj = _dot(h.astype(BF16), win_ref[...])
    scale = HEAD_DIM ** -0.5

    out_ref[:, 0:sb_w] = (proj[:, 0:sb_w] * scale).astype(BF16)
    out_ref[:, sb_w:3 * sb_w] = proj[:, sb_w:3 * sb_w].astype(BF16)

    ra, rm, rp = ra_ref[...], rm_ref[...], rp_ref[...]

    def qk_norm_rope(t, g, mult):
        hi, lo = _split_bf16(t * t)
        gm = gmat_ref[...]
        msq = _dot(hi, gm) + _dot(lo, gm)
        tn = t * lax.rsqrt(msq + EPS) * g
        cols = []
        for j in range(da_w // LANES):
            c = tn[:, j * LANES:(j + 1) * LANES]
            r = (c * ra + pltpu.roll(c, LANES - ROPE_DIM // 2, 1) * rm
                 + pltpu.roll(c, ROPE_DIM // 2, 1) * rp)
            cols.append((r * mult).astype(BF16))
        return jnp.concatenate(cols, axis=1)

    o = 3 * sb_w
    out_ref[:, o:o + da_w] = qk_norm_rope(proj[:, o:o + da_w], gq_ref[...], scale * LOG2E)
    out_ref[:, o + da_w:o + 2 * da_w] = qk_norm_rope(proj[:, o + da_w:o + 2 * da_w], gk_ref[...], 1.0)
    out_ref[:, o + 2 * da_w:o + 3 * da_w] = proj[:, o + 2 * da_w:o + 3 * da_w].astype(BF16)


def _stack_masked(q, n_parts):
    lane = lax.broadcasted_iota(jnp.int32, q.shape, 1)
    zero = jnp.zeros_like(q)
    return jnp.concatenate(
        [jnp.where((lane >= p * HEAD_DIM) & (lane < (p + 1) * HEAD_DIM), q, zero)
         for p in range(n_parts)], axis=0)


def _sb_kernel(q_ref, k_ref, v_ref, o_ref, acc_ref, run_ref, *, tq):
    i = pl.program_id(2)
    sub = LANES
    n_sub = tq // sub
    m2 = 2 * tq
    r2 = lax.broadcasted_iota(jnp.int32, (2 * sub, 2 * sub), 0) % sub
    c2 = lax.broadcasted_iota(jnp.int32, (2 * sub, 2 * sub), 1)
    tri = jnp.where((c2 >= sub) | (r2 > c2), 1.0, 0.0).astype(BF16)

    qs = _stack_masked(q_ref[...], 2)
    acc_ref[...] = jnp.zeros_like(acc_ref)
    run_ref[...] = jnp.zeros_like(run_ref)

    def block(ks, masked):
        kk = k_ref[pl.ds(ks, tq), :]
        vv = v_ref[pl.ds(ks, tq), :]
        z = _dot_nt(qs, kk)
        sp = jnp.log(1.0 + jnp.exp(-jnp.abs(z)))
        l1m = jnp.minimum(-z, 0.0) - sp
        if masked:
            row = lax.broadcasted_iota(jnp.int32, (m2, tq), 0)
            col = lax.broadcasted_iota(jnp.int32, (m2, tq), 1)
            mask = col < jnp.where(row >= tq, row - tq, row)
            l1m = jnp.where(mask, l1m, 0.0)
        hi, lo = _split_bf16(l1m)
        run = run_ref[...]
        betw = [None] * n_sub
        for c in reversed(range(n_sub)):
            sl = slice(c * sub, (c + 1) * sub)
            cs = _dot(jnp.concatenate([hi[:, sl], lo[:, sl]], axis=1), tri)
            betw[c] = cs[:, :sub] + run
            run = run + cs[:, sub:]
        run_ref[...] = run
        a = jnp.exp(z + l1m + jnp.concatenate(betw, axis=1))
        if masked:
            a = jnp.where(mask, a, 0.0)
        acc_ref[...] += _dot(a.astype(BF16), vv)

    block(pl.multiple_of(i * tq, tq), True)

    def body(it, _):
        block(pl.multiple_of((i - 1 - it) * tq, tq), False)
        return 0

    lax.fori_loop(0, i, body, 0)
    lane = lax.broadcasted_iota(jnp.int32, (tq, LANES), 1)
    o_ref[...] = jnp.where(lane < HEAD_DIM, acc_ref[0:tq, :], acc_ref[tq:m2, :]).astype(o_ref.dtype)


def _da_kernel(q_ref, k_ref, v_ref, lq1_ref, lk1_ref, lq2_ref, lk2_ref, gsub_ref, o_ref,
               vext_ref, qs_ref, s_ref, mc_ref, m_ref, l_ref, acc_ref, *, tq, lambda_init):
    i = pl.program_id(2)
    m2 = 2 * tq
    reps = tq // LANES

    @pl.when(i == 0)
    def _():
        vext_ref[:, 0:LANES] = v_ref[...]
        vext_ref[:, LANES:2 * LANES] = jnp.ones(v_ref.shape, v_ref.dtype)

    qs_ref[...] = _stack_masked(q_ref[...], 2)
    m_ref[...] = jnp.full_like(m_ref, NEG_INF)
    l_ref[...] = jnp.zeros_like(l_ref)
    acc_ref[...] = jnp.zeros_like(acc_ref)

    def scores(slot, blk):
        ks = pl.multiple_of(blk * tq, tq)
        s = _dot_nt(qs_ref[...], k_ref[pl.ds(ks, tq), :])
        s_ref[slot] = s
        mc_ref[slot] = jnp.broadcast_to(jnp.max(s, axis=1, keepdims=True), (m2, LANES))

    def absorb(slot, blk, diagonal=False):
        ks = pl.multiple_of(blk * tq, tq)
        s = s_ref[slot]
        if diagonal:
            row = lax.broadcasted_iota(jnp.int32, (m2, tq), 0)
            col = lax.broadcasted_iota(jnp.int32, (m2, tq), 1)
            qrow = jnp.where(row >= tq, row - tq, row)
            s = jnp.where((col // CHUNK) <= (qrow // CHUNK), s, NEG_INF)
            m_cur = jnp.max(s, axis=1, keepdims=True)
        else:
            m_cur = mc_ref[slot]
        m_prev = m_ref[...]
        m_new = jnp.maximum(m_prev, m_cur)
        p = jnp.exp2(s - jnp.concatenate([m_new] * reps, axis=1))
        alpha = jnp.exp2(m_prev - m_new)
        pv = _dot(p.astype(BF16), vext_ref[pl.ds(ks, tq), :])
        acc_ref[...] = alpha * acc_ref[...] + pv[:, :LANES]
        l_ref[...] = alpha * l_ref[...] + pv[:, LANES:]
        m_ref[...] = m_new

    scores(0, 0)

    def body(pair, _):
        blk = 2 * pair
        scores(1, blk + 1)
        absorb(0, blk)
        scores(0, blk + 2)
        absorb(1, blk + 1)
        return 0

    lax.fori_loop(0, i // 2, body, 0)

    @pl.when(i % 2 == 0)
    def _():
        absorb(0, i, diagonal=True)

    @pl.when(i % 2 == 1)
    def _():
        scores(1, i)
        absorb(0, i - 1)
        absorb(1, i, diagonal=True)

    res = acc_ref[...] / l_ref[...]
    lam = (jnp.exp(jnp.sum(lq1_ref[...] * lk1_ref[...], axis=-1, keepdims=True))
           - jnp.exp(jnp.sum(lq2_ref[...] * lk2_ref[...], axis=-1, keepdims=True))
           + lambda_init)
    o = res[0:tq, :] - lam * res[tq:m2, :]
    ms = jnp.mean(o * o, axis=-1, keepdims=True)
    o = o * lax.rsqrt(ms + EPS) * gsub_ref[...] * (1.0 - lambda_init)
    o_ref[...] = o.astype(o_ref.dtype)


def _post_kernel(x_ref, oa_ref, ob_ref, gmix_ref, wgate_ref, bgate_ref, wa_ref, wb_ref, wout_ref,
                 gffn_ref, wfg_ref, wfu_ref, wfd_ref, out_ref, *, d_model):
    x = x_ref[...]
    ms = jnp.mean(x * x, axis=-1, keepdims=True)
    h = (x * lax.rsqrt(ms + EPS) * gmix_ref[...]).astype(BF16)
    gates = jax.nn.sigmoid(_dot(h, wgate_ref[...]) + bgate_ref[...])
    br_a = _dot(oa_ref[...], wa_ref[...])
    br_b = _dot(ob_ref[...], wb_ref[...])
    merged = gates[:, :d_model] * br_a + gates[:, d_model:] * br_b
    x1 = x + _dot(merged.astype(BF16), wout_ref[...])

    ms2 = jnp.mean(x1 * x1, axis=-1, keepdims=True)
    h2 = (x1 * lax.rsqrt(ms2 + EPS) * gffn_ref[...]).astype(BF16)
    fg = _dot(h2, wfg_ref[...])
    fu = _dot(h2, wfu_ref[...])
    ff = (fg * jax.nn.sigmoid(fg)) * fu
    out_ref[...] = x1 + _dot(ff.astype(BF16), wfd_ref[...])


def _rope_tables(seq):
    half = ROPE_DIM // 2
    pos = jnp.arange(seq, dtype=F32)
    inv_freq = ROPE_THETA ** (-jnp.arange(0, ROPE_DIM, 2, dtype=F32) / ROPE_DIM)
    ang = pos[:, None] * inv_freq[None, :]
    cos, sin = jnp.cos(ang), jnp.sin(ang)
    ones = jnp.ones((seq, HEAD_DIM - ROPE_DIM), F32)
    zeros = jnp.zeros((seq, HEAD_DIM - ROPE_DIM), F32)
    zh = jnp.zeros((seq, half), F32)
    ra = jnp.concatenate([cos, cos, ones], axis=1)
    rm = jnp.concatenate([-sin, zh, zeros], axis=1)
    rp = jnp.concatenate([zh, sin, zeros], axis=1)
    tile = lambda t: jnp.concatenate([t] * (LANES // HEAD_DIM), axis=1)
    return tile(ra), tile(rm), tile(rp)


def _tiles(seq):
    tm, tp, tq_sb, tq_da = 512, 256, 256, 512
    for t in (tm, tp, tq_sb, tq_da):
        assert seq % t == 0 and t % LANES == 0 and t % CHUNK == 0
    return tm, tp, tq_sb, tq_da


def kernel(x, g_mix, w_in, g_q, g_k, lam_q1, lam_k1, lam_q2, lam_k2, g_sub, w_branch_a, w_branch_b,
           w_gate, b_gate, w_out, g_ffn, w_ffn_gate, w_ffn_up, w_ffn_down):
    b, s, d = x.shape
    depth = g_mix.shape[0]
    sb_w = SB_HEADS * HEAD_DIM
    da_w = DA_HEADS * 2 * HEAD_DIM
    in_w = 3 * sb_w + 3 * da_w
    n = b * s
    d_ff = w_ffn_gate.shape[-1]

    tm, tp, tq_sb, tq_da = _tiles(s)

    ra, rm, rp = _rope_tables(s)
    grp = jnp.arange(da_w) // HEAD_DIM
    gmat = jnp.where(grp[:, None] == grp[None, :], 1.0 / HEAD_DIM, 0.0).astype(BF16)
    tile_g = lambda g: jnp.tile(g, da_w // HEAD_DIM)[None, :]
    vmem_full = pl.BlockSpec(memory_space=pltpu.VMEM)
    spt = s // tm

    xf = x.reshape(n, d)
    for layer in range(depth):
        lambda_init = 0.8 - 0.6 * math.exp(-0.3 * layer)
        gmix = g_mix[layer][None, :]

        proj = pl.pallas_call(
            functools.partial(_proj_kernel, sb_w=sb_w, da_w=da_w),
            grid=(n // tm,),
            in_specs=[
                pl.BlockSpec((tm, d), lambda t: (t, 0)),
                vmem_full, vmem_full, vmem_full, vmem_full, vmem_full,
                pl.BlockSpec((tm, LANES), lambda t: (t % spt, 0)),
                pl.BlockSpec((tm, LANES), lambda t: (t % spt, 0)),
                pl.BlockSpec((tm, LANES), lambda t: (t % spt, 0)),
            ],
            out_specs=pl.BlockSpec((tm, in_w), lambda t: (t, 0)),
            out_shape=jax.ShapeDtypeStruct((n, in_w), BF16),
            compiler_params=pltpu.CompilerParams(
                dimension_semantics=("arbitrary",), vmem_limit_bytes=VMEM_LIMIT),
            name="proj",
        )(xf, gmix, w_in[layer].astype(BF16), tile_g(g_q[layer]), tile_g(g_k[layer]), gmat,
          ra, rm, rp)

        cb = lambda off: off // LANES
        tq, nq = tq_sb, s // tq_sb
        o_a = pl.pallas_call(
            functools.partial(_sb_kernel, tq=tq),
            grid=(b, sb_w // LANES, nq),
            in_specs=[
                pl.BlockSpec((tq, LANES), lambda bi, j, i, nq=nq: (bi * nq + i, j)),
                pl.BlockSpec((s, LANES), lambda bi, j, i: (bi, cb(sb_w) + j)),
                pl.BlockSpec((s, LANES), lambda bi, j, i: (bi, cb(2 * sb_w) + j)),
            ],
            out_specs=pl.BlockSpec((tq, LANES), lambda bi, j, i, nq=nq: (bi * nq + i, j)),
            out_shape=jax.ShapeDtypeStruct((n, sb_w), BF16),
            scratch_shapes=[pltpu.VMEM((2 * tq, LANES), F32), pltpu.VMEM((2 * tq, LANES), F32)],
            compiler_params=pltpu.CompilerParams(
                dimension_semantics=("arbitrary", "arbitrary", "arbitrary"),
                vmem_limit_bytes=VMEM_LIMIT),
            name="sb_attn",
        )(proj, proj, proj)

        tq, nq = tq_da, s // tq_da
        o_b = pl.pallas_call(
            functools.partial(_da_kernel, tq=tq, lambda_init=lambda_init),
            grid=(b, DA_HEADS, nq),
            in_specs=[
                pl.BlockSpec((tq, LANES), lambda bi, j, i, nq=nq: (bi * nq + i, cb(3 * sb_w) + j)),
                pl.BlockSpec((s, LANES), lambda bi, j, i: (bi, cb(3 * sb_w + da_w) + j)),
                pl.BlockSpec((s, LANES), lambda bi, j, i: (bi, cb(3 * sb_w + 2 * da_w) + j)),
                vmem_full, vmem_full, vmem_full, vmem_full, vmem_full,
            ],
            out_specs=pl.BlockSpec((tq, LANES), lambda bi, j, i, nq=nq: (bi * nq + i, j)),
            out_shape=jax.ShapeDtypeStruct((n, da_w), BF16),
            scratch_shapes=[pltpu.VMEM((s, 2 * LANES), BF16),
                            pltpu.VMEM((2 * tq, LANES), BF16),
                            pltpu.VMEM((2, 2 * tq, tq), F32),
                            pltpu.VMEM((2, 2 * tq, LANES), F32)]
            + [pltpu.VMEM((2 * tq, LANES), F32)] * 3,
            compiler_params=pltpu.CompilerParams(
                dimension_semantics=("arbitrary", "arbitrary", "arbitrary"),
                vmem_limit_bytes=VMEM_LIMIT),
            name="da_attn",
        )(proj, proj, proj, lam_q1[layer][None, :], lam_k1[layer][None, :],
          lam_q2[layer][None, :], lam_k2[layer][None, :], g_sub[layer][None, :])

        xf = pl.pallas_call(
            functools.partial(_post_kernel, d_model=d),
            grid=(n // tp,),
            in_specs=[
                pl.BlockSpec((tp, d), lambda t: (t, 0)),
                pl.BlockSpec((tp, sb_w), lambda t: (t, 0)),
                pl.BlockSpec((tp, da_w), lambda t: (t, 0)),
            ] + [vmem_full] * 10,
            out_specs=pl.BlockSpec((tp, d), lambda t: (t, 0)),
            out_shape=jax.ShapeDtypeStruct((n, d), F32),
            compiler_params=pltpu.CompilerParams(
                dimension_semantics=("arbitrary",), vmem_limit_bytes=VMEM_LIMIT),
            name="post",
        )(xf, o_a, o_b, gmix, w_gate[layer].astype(BF16), b_gate[layer][None, :],
          w_branch_a[layer].astype(BF16), w_branch_b[layer].astype(BF16),
          w_out[layer].astype(BF16), g_ffn[layer][None, :],
          w_ffn_gate[layer].astype(BF16), w_ffn_up[layer].astype(BF16),
          w_ffn_down[layer].astype(BF16))
    return xf.reshape(b, s, d)
```

```python
import functools
import math

import jax
import jax.numpy as jnp
from jax import lax
from jax.experimental import pallas as pl
from jax.experimental.pallas import tpu as pltpu

F32 = jnp.float32
BF16 = jnp.bfloat16

CHUNK = 64
SB_HEADS = 8
DA_HEADS = 4
HEAD_DIM = 64
ROPE_THETA = 500000.0
ROPE_DIM = HEAD_DIM // 4
EPS = 1e-6
NEG_INF = -1e30
LOG2E = math.log2(math.e)
SB_STOP_LOG2 = 150.0
LANES = 128
VMEM_LIMIT = 56 * 1024 * 1024


def _dot(a, b):
    return jnp.dot(a, b, preferred_element_type=F32)


def _dot_nt(a, b):
    return lax.dot_general(a, b, (((1,), (1,)), ((), ())), preferred_element_type=F32)


def _split_bf16(x):
    hi = x.astype(BF16)
    lo = (x - hi.astype(F32)).astype(BF16)
    return hi, lo


def _proj_kernel(x_ref, gmix_ref, win_ref, gq_ref, gk_ref, gmat_ref,
                 ra_ref, rm_ref, rp_ref, out_ref, *, sb_w, da_w):
    x = x_ref[...]
    ms = jnp.mean(x * x, axis=-1, keepdims=True)
    h = x * lax.rsqrt(ms + EPS) * gmix_ref[...]
    proj = _dot(h.astype(BF16), win_ref[...])
    scale = HEAD_DIM ** -0.5

    out_ref[:, 0:sb_w] = (proj[:, 0:sb_w] * (scale * LOG2E)).astype(BF16)
    out_ref[:, sb_w:3 * sb_w] = proj[:, sb_w:3 * sb_w].astype(BF16)

    ra, rm, rp = ra_ref[...], rm_ref[...], rp_ref[...]

    def qk_norm_rope(t, g, mult):
        hi, lo = _split_bf16(t * t)
        gm = gmat_ref[...]
        msq = _dot(hi, gm) + _dot(lo, gm)
        tn = t * lax.rsqrt(msq + EPS) * g
        cols = []
        for j in range(da_w // LANES):
            c = tn[:, j * LANES:(j + 1) * LANES]
            r = (c * ra + pltpu.roll(c, LANES - ROPE_DIM // 2, 1) * rm
                 + pltpu.roll(c, ROPE_DIM // 2, 1) * rp)
            cols.append((r * mult).astype(BF16))
        return jnp.concatenate(cols, axis=1)

    o = 3 * sb_w
    out_ref[:, o:o + da_w] = qk_norm_rope(proj[:, o:o + da_w], gq_ref[...], scale * LOG2E)
    out_ref[:, o + da_w:o + 2 * da_w] = qk_norm_rope(proj[:, o + da_w:o + 2 * da_w], gk_ref[...], 1.0)
    out_ref[:, o + 2 * da_w:o + 3 * da_w] = proj[:, o + 2 * da_w:o + 3 * da_w].astype(BF16)


def _stack_masked(q, n_parts):
    lane = lax.broadcasted_iota(jnp.int32, q.shape, 1)
    zero = jnp.zeros_like(q)
    return jnp.concatenate(
        [jnp.where((lane >= p * HEAD_DIM) & (lane < (p + 1) * HEAD_DIM), q, zero)
         for p in range(n_parts)], axis=0)


def _sb_kernel(q_ref, k_ref, v_ref, o_ref, qs_ref, acc_ref, run_ref, *, tq):
    i = pl.program_id(2)
    sub = LANES
    m2 = 2 * tq
    r2 = lax.broadcasted_iota(jnp.int32, (2 * sub, 2 * sub), 0) % sub
    c2 = lax.broadcasted_iota(jnp.int32, (2 * sub, 2 * sub), 1)
    tri = jnp.where((c2 >= sub) | (r2 > c2), 1.0, 0.0).astype(BF16)

    qs_ref[...] = _stack_masked(q_ref[...], 2)
    acc_ref[...] = jnp.zeros_like(acc_ref)
    run_ref[...] = jnp.zeros_like(run_ref)

    def block(ks, width, delta):
        z = _dot_nt(qs_ref[...], k_ref[pl.ds(ks, width), :])
        p = jnp.maximum(z, 0.0) + jnp.log2(1.0 + jnp.exp2(-jnp.abs(z)))
        if delta is not None:
            row = lax.broadcasted_iota(jnp.int32, (m2, width), 0)
            col = lax.broadcasted_iota(jnp.int32, (m2, width), 1)
            mask = (col - jnp.where(row >= tq, row - tq, row)) < delta
            p = jnp.where(mask, p, 0.0)
        hi, lo = _split_bf16(p)
        run = run_ref[...]
        n_sub = width // sub
        spent = [None] * n_sub
        for c in reversed(range(n_sub)):
            sl = slice(c * sub, (c + 1) * sub)
            cs = _dot(jnp.concatenate([hi[:, sl], lo[:, sl]], axis=1), tri)
            spent[c] = cs[:, :sub] + run
            run = run + cs[:, sub:]
        run_ref[...] = run
        a = jnp.exp2((z - p) - jnp.concatenate(spent, axis=1))
        if delta is not None:
            a = jnp.where(mask, a, 0.0)
        acc_ref[...] += _dot(a.astype(BF16), v_ref[pl.ds(ks, width), :])
        return run

    first = jnp.maximum(i - 1, 0)
    run = block(pl.multiple_of(first * tq, tq), 2 * tq, (i - first) * tq)

    def more(carry):
        blk, least = carry
        return jnp.logical_and(blk >= 0, least < SB_STOP_LOG2)

    def step(carry):
        blk, _ = carry
        return blk - 1, jnp.min(block(pl.multiple_of(blk * tq, tq), tq, None))

    lax.while_loop(more, step, (i - 2, jnp.min(run)))
    lane = lax.broadcasted_iota(jnp.int32, (tq, LANES), 1)
    o_ref[...] = jnp.where(lane < HEAD_DIM, acc_ref[0:tq, :], acc_ref[tq:m2, :]).astype(o_ref.dtype)


def _da_kernel(q_ref, k_ref, v_ref, lq1_ref, lk1_ref, lq2_ref, lk2_ref, gsub_ref, o_ref,
               vext_ref, qs_ref, s_ref, mc_ref, m_ref, l_ref, acc_ref, *, tq, lambda_init):
    i = pl.program_id(2)
    m2 = 2 * tq
    reps = tq // LANES

    @pl.when(i == 0)
    def _():
        vext_ref[:, 0:LANES] = v_ref[...]
        vext_ref[:, LANES:2 * LANES] = jnp.ones(v_ref.shape, v_ref.dtype)

    qs_ref[...] = _stack_masked(q_ref[...], 2)
    m_ref[...] = jnp.full_like(m_ref, NEG_INF)
    l_ref[...] = jnp.zeros_like(l_ref)
    acc_ref[...] = jnp.zeros_like(acc_ref)

    def scores(slot, blk):
        ks = pl.multiple_of(blk * tq, tq)
        s = _dot_nt(qs_ref[...], k_ref[pl.ds(ks, tq), :])
        s_ref[slot] = s
        mc_ref[slot] = jnp.broadcast_to(jnp.max(s, axis=1, keepdims=True), (m2, LANES))

    def absorb(slot, blk, diagonal=False):
        ks = pl.multiple_of(blk * tq, tq)
        s = s_ref[slot]
        if diagonal:
            row = lax.broadcasted_iota(jnp.int32, (m2, tq), 0)
            col = lax.broadcasted_iota(jnp.int32, (m2, tq), 1)
            qrow = jnp.where(row >= tq, row - tq, row)
            s = jnp.where((col // CHUNK) <= (qrow // CHUNK), s, NEG_INF)
            m_cur = jnp.max(s, axis=1, keepdims=True)
        else:
            m_cur = mc_ref[slot]
        m_prev = m_ref[...]
        m_new = jnp.maximum(m_prev, m_cur)
        p = jnp.exp2(s - jnp.concatenate([m_new] * reps, axis=1))
        alpha = jnp.exp2(m_prev - m_new)
        pv = _dot(p.astype(BF16), vext_ref[pl.ds(ks, tq), :])
        acc_ref[...] = alpha * acc_ref[...] + pv[:, :LANES]
        l_ref[...] = alpha * l_ref[...] + pv[:, LANES:]
        m_ref[...] = m_new

    scores(0, 0)

    def body(pair, _):
        blk = 2 * pair
        scores(1, blk + 1)
        absorb(0, blk)
        scores(0, blk + 2)
        absorb(1, blk + 1)
        return 0

    lax.fori_loop(0, i // 2, body, 0)

    @pl.when(i % 2 == 0)
    def _():
        absorb(0, i, diagonal=True)

    @pl.when(i % 2 == 1)
    def _():
        scores(1, i)
        absorb(0, i - 1)
        absorb(1, i, diagonal=True)

    res = acc_ref[...] / l_ref[...]
    lam = (jnp.exp(jnp.sum(lq1_ref[...] * lk1_ref[...], axis=-1, keepdims=True))
           - jnp.exp(jnp.sum(lq2_ref[...] * lk2_ref[...], axis=-1, keepdims=True))
           + lambda_init)
    o = res[0:tq, :] - lam * res[tq:m2, :]
    ms = jnp.mean(o * o, axis=-1, keepdims=True)
    o = o * lax.rsqrt(ms + EPS) * gsub_ref[...] * (1.0 - lambda_init)
    o_ref[...] = o.astype(o_ref.dtype)


def _post_kernel(x_ref, oa_ref, ob_ref, gmix_ref, wgate_ref, bgate_ref, wa_ref, wb_ref, wout_ref,
                 gffn_ref, wfg_ref, wfu_ref, wfd_ref, out_ref, *, d_model):
    x = x_ref[...]
    ms = jnp.mean(x * x, axis=-1, keepdims=True)
    h = (x * lax.rsqrt(ms + EPS) * gmix_ref[...]).astype(BF16)
    gates = jax.nn.sigmoid(_dot(h, wgate_ref[...]) + bgate_ref[...])
    br_a = _dot(oa_ref[...], wa_ref[...])
    br_b = _dot(ob_ref[...], wb_ref[...])
    merged = gates[:, :d_model] * br_a + gates[:, d_model:] * br_b
    x1 = x + _dot(merged.astype(BF16), wout_ref[...])

    ms2 = jnp.mean(x1 * x1, axis=-1, keepdims=True)
    h2 = (x1 * lax.rsqrt(ms2 + EPS) * gffn_ref[...]).astype(BF16)
    fg = _dot(h2, wfg_ref[...])
    fu = _dot(h2, wfu_ref[...])
    ff = (fg * jax.nn.sigmoid(fg)) * fu
    out_ref[...] = x1 + _dot(ff.astype(BF16), wfd_ref[...])


def _rope_tables(seq):
    half = ROPE_DIM // 2
    pos = jnp.arange(seq, dtype=F32)
    inv_freq = ROPE_THETA ** (-jnp.arange(0, ROPE_DIM, 2, dtype=F32) / ROPE_DIM)
    ang = pos[:, None] * inv_freq[None, :]
    cos, sin = jnp.cos(ang), jnp.sin(ang)
    ones = jnp.ones((seq, HEAD_DIM - ROPE_DIM), F32)
    zeros = jnp.zeros((seq, HEAD_DIM - ROPE_DIM), F32)
    zh = jnp.zeros((seq, half), F32)
    ra = jnp.concatenate([cos, cos, ones], axis=1)
    rm = jnp.concatenate([-sin, zh, zeros], axis=1)
    rp = jnp.concatenate([zh, sin, zeros], axis=1)
    tile = lambda t: jnp.concatenate([t] * (LANES // HEAD_DIM), axis=1)
    return tile(ra), tile(rm), tile(rp)


def _tiles(seq):
    tm, tp, tq_sb, tq_da = 512, 256, 256, 512
    for t in (tm, tp, tq_sb, tq_da):
        assert seq % t == 0 and t % LANES == 0 and t % CHUNK == 0
    return tm, tp, tq_sb, tq_da


def kernel(x, g_mix, w_in, g_q, g_k, lam_q1, lam_k1, lam_q2, lam_k2, g_sub, w_branch_a, w_branch_b,
           w_gate, b_gate, w_out, g_ffn, w_ffn_gate, w_ffn_up, w_ffn_down):
    b, s, d = x.shape
    depth = g_mix.shape[0]
    sb_w = SB_HEADS * HEAD_DIM
    da_w = DA_HEADS * 2 * HEAD_DIM
    in_w = 3 * sb_w + 3 * da_w
    n = b * s
    d_ff = w_ffn_gate.shape[-1]

    tm, tp, tq_sb, tq_da = _tiles(s)

    ra, rm, rp = _rope_tables(s)
    grp = jnp.arange(da_w) // HEAD_DIM
    gmat = jnp.where(grp[:, None] == grp[None, :], 1.0 / HEAD_DIM, 0.0).astype(BF16)
    tile_g = lambda g: jnp.tile(g, da_w // HEAD_DIM)[None, :]
    vmem_full = pl.BlockSpec(memory_space=pltpu.VMEM)
    spt = s // tm

    xf = x.reshape(n, d)
    for layer in range(depth):
        lambda_init = 0.8 - 0.6 * math.exp(-0.3 * layer)
        gmix = g_mix[layer][None, :]

        proj = pl.pallas_call(
            functools.partial(_proj_kernel, sb_w=sb_w, da_w=da_w),
            grid=(n // tm,),
            in_specs=[
                pl.BlockSpec((tm, d), lambda t: (t, 0)),
                vmem_full, vmem_full, vmem_full, vmem_full, vmem_full,
                pl.BlockSpec((tm, LANES), lambda t: (t % spt, 0)),
                pl.BlockSpec((tm, LANES), lambda t: (t % spt, 0)),
                pl.BlockSpec((tm, LANES), lambda t: (t % spt, 0)),
            ],
            out_specs=pl.BlockSpec((tm, in_w), lambda t: (t, 0)),
            out_shape=jax.ShapeDtypeStruct((n, in_w), BF16),
            compiler_params=pltpu.CompilerParams(
                dimension_semantics=("arbitrary",), vmem_limit_bytes=VMEM_LIMIT),
            name="proj",
        )(xf, gmix, w_in[layer].astype(BF16), tile_g(g_q[layer]), tile_g(g_k[layer]), gmat,
          ra, rm, rp)

        cb = lambda off: off // LANES
        tq, nq = tq_sb, s // tq_sb
        o_a = pl.pallas_call(
            functools.partial(_sb_kernel, tq=tq),
            grid=(b, sb_w // LANES, nq),
            in_specs=[
                pl.BlockSpec((tq, LANES), lambda bi, j, i, nq=nq: (bi * nq + i, j)),
                pl.BlockSpec((s, LANES), lambda bi, j, i: (bi, cb(sb_w) + j)),
                pl.BlockSpec((s, LANES), lambda bi, j, i: (bi, cb(2 * sb_w) + j)),
            ],
            out_specs=pl.BlockSpec((tq, LANES), lambda bi, j, i, nq=nq: (bi * nq + i, j)),
            out_shape=jax.ShapeDtypeStruct((n, sb_w), BF16),
            scratch_shapes=[pltpu.VMEM((2 * tq, LANES), BF16),
                            pltpu.VMEM((2 * tq, LANES), F32), pltpu.VMEM((2 * tq, LANES), F32)],
            compiler_params=pltpu.CompilerParams(
                dimension_semantics=("arbitrary", "arbitrary", "arbitrary"),
                vmem_limit_bytes=VMEM_LIMIT),
            name="sb_attn",
        )(proj, proj, proj)

        tq, nq = tq_da, s // tq_da
        o_b = pl.pallas_call(
            functools.partial(_da_kernel, tq=tq, lambda_init=lambda_init),
            grid=(b, DA_HEADS, nq),
            in_specs=[
                pl.BlockSpec((tq, LANES), lambda bi, j, i, nq=nq: (bi * nq + i, cb(3 * sb_w) + j)),
                pl.BlockSpec((s, LANES), lambda bi, j, i: (bi, cb(3 * sb_w + da_w) + j)),
                pl.BlockSpec((s, LANES), lambda bi, j, i: (bi, cb(3 * sb_w + 2 * da_w) + j)),
                vmem_full, vmem_full, vmem_full, vmem_full, vmem_full,
            ],
            out_specs=pl.BlockSpec((tq, LANES), lambda bi, j, i, nq=nq: (bi * nq + i, j)),
            out_shape=jax.ShapeDtypeStruct((n, da_w), BF16),
            scratch_shapes=[pltpu.VMEM((s, 2 * LANES), BF16),
                            pltpu.VMEM((2 * tq, LANES), BF16),
                            pltpu.VMEM((2, 2 * tq, tq), F32),
                            pltpu.VMEM((2, 2 * tq, LANES), F32)]
            + [pltpu.VMEM((2 * tq, LANES), F32)] * 3,
            compiler_params=pltpu.CompilerParams(
                dimension_semantics=("arbitrary", "arbitrary", "arbitrary"),
                vmem_limit_bytes=VMEM_LIMIT),
            name="da_attn",
        )(proj, proj, proj, lam_q1[layer][None, :], lam_k1[layer][None, :],
          lam_q2[layer][None, :], lam_k2[layer][None, :], g_sub[layer][None, :])

        xf = pl.pallas_call(
            functools.partial(_post_kernel, d_model=d),
            grid=(n // tp,),
            in_specs=[
                pl.BlockSpec((tp, d), lambda t: (t, 0)),
                pl.BlockSpec((tp, sb_w), lambda t: (t, 0)),
                pl.BlockSpec((tp, da_w), lambda t: (t, 0)),
            ] + [vmem_full] * 10,
            out_specs=pl.BlockSpec((tp, d), lambda t: (t, 0)),
            out_shape=jax.ShapeDtypeStruct((n, d), F32),
            compiler_params=pltpu.CompilerParams(
                dimension_semantics=("arbitrary",), vmem_limit_bytes=VMEM_LIMIT),
            name="post",
        )(xf, o_a, o_b, gmix, w_gate[layer].astype(BF16), b_gate[layer][None, :],
          w_branch_a[layer].astype(BF16), w_branch_b[layer].astype(BF16),
          w_out[layer].astype(BF16), g_ffn[layer][None, :],
          w_ffn_gate[layer].astype(BF16), w_ffn_up[layer].astype(BF16),
          w_ffn_down[layer].astype(BF16))
    return xf.reshape(b, s, d)
```

```python
import functools
import math

import jax
import jax.numpy as jnp
import numpy as np
from jax import lax
from jax.experimental import pallas as pl
from jax.experimental.pallas import tpu as pltpu

F32 = jnp.float32
BF16 = jnp.bfloat16

CHUNK = 64
SB_HEADS = 8
DA_HEADS = 4
HEAD_DIM = 64
ROPE_THETA = 500000.0
ROPE_DIM = HEAD_DIM // 4
EPS = 1e-6
NEG_INF = -1e30
LOG2E = math.log2(math.e)
SB_GROUPS = 4
SB_STOP_LOG2 = 150.0
LANES = 128
VMEM_LIMIT = 56 * 1024 * 1024


def _dot(a, b):
    return jnp.dot(a, b, preferred_element_type=F32)


def _dot_nt(a, b):
    return lax.dot_general(a, b, (((1,), (1,)), ((), ())), preferred_element_type=F32)


def _split_bf16(x):
    hi = x.astype(BF16)
    lo = (x - hi.astype(F32)).astype(BF16)
    return hi, lo


def _proj_kernel(x_ref, gmix_ref, win_ref, gq_ref, gk_ref, gmat_ref,
                 ra_ref, rm_ref, rp_ref, out_ref, *, sb_w, da_w):
    x = x_ref[...]
    ms = jnp.mean(x * x, axis=-1, keepdims=True)
    h = x * lax.rsqrt(ms + EPS) * gmix_ref[...]
    proj = _dot(h.astype(BF16), win_ref[...])
    scale = HEAD_DIM ** -0.5

    out_ref[:, 0:sb_w] = (proj[:, 0:sb_w] * (scale * LOG2E)).astype(BF16)
    out_ref[:, sb_w:3 * sb_w] = proj[:, sb_w:3 * sb_w].astype(BF16)

    ra, rm, rp = ra_ref[...], rm_ref[...], rp_ref[...]

    def qk_norm_rope(t, g, mult):
        hi, lo = _split_bf16(t * t)
        gm = gmat_ref[...]
        msq = _dot(hi, gm) + _dot(lo, gm)
        tn = t * lax.rsqrt(msq + EPS) * g
        cols = []
        for j in range(da_w // LANES):
            c = tn[:, j * LANES:(j + 1) * LANES]
            r = (c * ra + pltpu.roll(c, LANES - ROPE_DIM // 2, 1) * rm
                 + pltpu.roll(c, ROPE_DIM // 2, 1) * rp)
            cols.append((r * mult).astype(BF16))
        return jnp.concatenate(cols, axis=1)

    o = 3 * sb_w
    out_ref[:, o:o + da_w] = qk_norm_rope(proj[:, o:o + da_w], gq_ref[...], scale * LOG2E)
    out_ref[:, o + da_w:o + 2 * da_w] = qk_norm_rope(proj[:, o + da_w:o + 2 * da_w], gk_ref[...], 1.0)
    out_ref[:, o + 2 * da_w:o + 3 * da_w] = proj[:, o + 2 * da_w:o + 3 * da_w].astype(BF16)


def _stack_masked(q, n_parts):
    lane = lax.broadcasted_iota(jnp.int32, q.shape, 1)
    zero = jnp.zeros_like(q)
    return jnp.concatenate(
        [jnp.where((lane >= p * HEAD_DIM) & (lane < (p + 1) * HEAD_DIM), q, zero)
         for p in range(n_parts)], axis=0)


def _sb_kernel(q_ref, k_ref, v_ref, o_ref, qs_ref, acc_ref, run_ref, *, tq, n_grp):
    i = pl.program_id(2)
    sub = LANES
    m2 = 2 * tq
    r2 = lax.broadcasted_iota(jnp.int32, (2 * sub, 2 * sub), 0) % sub
    c2 = lax.broadcasted_iota(jnp.int32, (2 * sub, 2 * sub), 1)
    tri = jnp.where((c2 >= sub) | (r2 > c2), 1.0, 0.0).astype(BF16)
    lanes = lambda g: slice(g * LANES, (g + 1) * LANES)

    for g in range(n_grp):
        qs_ref[g] = _stack_masked(q_ref[:, lanes(g)], 2)

    def block(g, blk, run, diagonal=False):
        ks = pl.multiple_of(blk * tq, tq)
        z = _dot_nt(qs_ref[g], k_ref[pl.ds(ks, tq), lanes(g)])
        p = jnp.maximum(z, 0.0) + jnp.log2(1.0 + jnp.exp2(-jnp.abs(z)))
        if diagonal:
            row = lax.broadcasted_iota(jnp.int32, (m2, tq), 0)
            col = lax.broadcasted_iota(jnp.int32, (m2, tq), 1)
            mask = col < jnp.where(row >= tq, row - tq, row)
            p = jnp.where(mask, p, 0.0)
        hi, lo = _split_bf16(p)
        n_sub = tq // sub
        spent = [None] * n_sub
        for c in reversed(range(n_sub)):
            sl = slice(c * sub, (c + 1) * sub)
            cs = _dot(jnp.concatenate([hi[:, sl], lo[:, sl]], axis=1), tri)
            spent[c] = cs[:, :sub] + run
            run = run + cs[:, sub:]
        a = jnp.exp2((z - p) - jnp.concatenate(spent, axis=1))
        if diagonal:
            a = jnp.where(mask, a, 0.0)
        return run, _dot(a.astype(BF16), v_ref[pl.ds(ks, tq), lanes(g)])

    zeros = jnp.zeros((m2, LANES), F32)

    @pl.when(i == 0)
    def _():
        for g in range(n_grp):
            run_ref[g], acc_ref[g] = block(g, i, zeros, diagonal=True)

    @pl.when(i > 0)
    def _():
        for g in range(n_grp):
            run, out = block(g, i, zeros, diagonal=True)
            run, out2 = block(g, i - 1, run)
            run_ref[g] = run
            acc_ref[g] = out + out2

    def more(carry):
        blk, least = carry
        return jnp.logical_and(blk >= 0, least < SB_STOP_LOG2)

    def step(carry):
        blk, _ = carry
        least = None
        for g in range(n_grp):
            run, out = block(g, blk, run_ref[g])
            run_ref[g] = run
            acc_ref[g] += out
            least = jnp.min(run) if least is None else jnp.minimum(least, jnp.min(run))
        return blk - 1, least

    lax.while_loop(more, step, (i - 2, jnp.min(run_ref[...])))
    lane = lax.broadcasted_iota(jnp.int32, (tq, LANES), 1)
    for g in range(n_grp):
        o_ref[:, lanes(g)] = jnp.where(lane < HEAD_DIM, acc_ref[g, 0:tq, :],
                                       acc_ref[g, tq:m2, :]).astype(o_ref.dtype)


def _da_kernel(q_ref, k_ref, v_ref, lq1_ref, lk1_ref, lq2_ref, lk2_ref, gsub_ref, o_ref,
               vext_ref, qs_ref, s_ref, mc_ref, m_ref, l_ref, acc_ref, *, tq, lambda_init):
    i = pl.program_id(2)
    m2 = 2 * tq
    reps = tq // LANES

    @pl.when(i == 0)
    def _():
        vext_ref[:, 0:LANES] = v_ref[...]
        vext_ref[:, LANES:2 * LANES] = jnp.ones(v_ref.shape, v_ref.dtype)

    qs_ref[...] = _stack_masked(q_ref[...], 2)
    m_ref[...] = jnp.full_like(m_ref, NEG_INF)
    l_ref[...] = jnp.zeros_like(l_ref)
    acc_ref[...] = jnp.zeros_like(acc_ref)

    def scores(slot, blk):
        ks = pl.multiple_of(blk * tq, tq)
        s = _dot_nt(qs_ref[...], k_ref[pl.ds(ks, tq), :])
        s_ref[slot] = s
        mc_ref[slot] = jnp.broadcast_to(jnp.max(s, axis=1, keepdims=True), (m2, LANES))

    def absorb(slot, blk, diagonal=False):
        ks = pl.multiple_of(blk * tq, tq)
        s = s_ref[slot]
        if diagonal:
            row = lax.broadcasted_iota(jnp.int32, (m2, tq), 0)
            col = lax.broadcasted_iota(jnp.int32, (m2, tq), 1)
            qrow = jnp.where(row >= tq, row - tq, row)
            s = jnp.where((col // CHUNK) <= (qrow // CHUNK), s, NEG_INF)
            m_cur = jnp.max(s, axis=1, keepdims=True)
        else:
            m_cur = mc_ref[slot]
        m_prev = m_ref[...]
        m_new = jnp.maximum(m_prev, m_cur)
        p = jnp.exp2(s - jnp.concatenate([m_new] * reps, axis=1))
        alpha = jnp.exp2(m_prev - m_new)
        pv = _dot(p.astype(BF16), vext_ref[pl.ds(ks, tq), :])
        acc_ref[...] = alpha * acc_ref[...] + pv[:, :LANES]
        l_ref[...] = alpha * l_ref[...] + pv[:, LANES:]
        m_ref[...] = m_new

    scores(0, 0)

    def body(pair, _):
        blk = 2 * pair
        scores(1, blk + 1)
        absorb(0, blk)
        scores(0, blk + 2)
        absorb(1, blk + 1)
        return 0

    lax.fori_loop(0, i // 2, body, 0)

    @pl.when(i % 2 == 0)
    def _():
        absorb(0, i, diagonal=True)

    @pl.when(i % 2 == 1)
    def _():
        scores(1, i)
        absorb(0, i - 1)
        absorb(1, i, diagonal=True)

    res = acc_ref[...] / l_ref[...]
    lam = (jnp.exp(jnp.sum(lq1_ref[...] * lk1_ref[...], axis=-1, keepdims=True))
           - jnp.exp(jnp.sum(lq2_ref[...] * lk2_ref[...], axis=-1, keepdims=True))
           + lambda_init)
    o = res[0:tq, :] - lam * res[tq:m2, :]
    ms = jnp.mean(o * o, axis=-1, keepdims=True)
    o = o * lax.rsqrt(ms + EPS) * gsub_ref[...] * (1.0 - lambda_init)
    o_ref[...] = o.astype(o_ref.dtype)


def _post_kernel(x_ref, oa_ref, ob_ref, gmix_ref, wgate_ref, bgate_ref, wa_ref, wb_ref, wout_ref,
                 gffn_ref, wfg_ref, wfu_ref, wfd_ref, out_ref, *, d_model):
    x = x_ref[...]
    ms = jnp.mean(x * x, axis=-1, keepdims=True)
    h = (x * lax.rsqrt(ms + EPS) * gmix_ref[...]).astype(BF16)
    gates = jax.nn.sigmoid(_dot(h, wgate_ref[...]) + bgate_ref[...])
    br_a = _dot(oa_ref[...], wa_ref[...])
    br_b = _dot(ob_ref[...], wb_ref[...])
    merged = gates[:, :d_model] * br_a + gates[:, d_model:] * br_b
    x1 = x + _dot(merged.astype(BF16), wout_ref[...])

    ms2 = jnp.mean(x1 * x1, axis=-1, keepdims=True)
    h2 = (x1 * lax.rsqrt(ms2 + EPS) * gffn_ref[...]).astype(BF16)
    fg = _dot(h2, wfg_ref[...])
    fu = _dot(h2, wfu_ref[...])
    ff = (fg * jax.nn.sigmoid(fg)) * fu
    out_ref[...] = x1 + _dot(ff.astype(BF16), wfd_ref[...])


def _rope_tables(seq):
    half = ROPE_DIM // 2
    pos = np.arange(seq, dtype=np.float64)
    inv_freq = ROPE_THETA ** (-np.arange(0, ROPE_DIM, 2, dtype=np.float64) / ROPE_DIM)
    ang = pos[:, None] * inv_freq[None, :]
    cos, sin = np.cos(ang), np.sin(ang)
    ones = np.ones((seq, HEAD_DIM - ROPE_DIM))
    zeros = np.zeros((seq, HEAD_DIM - ROPE_DIM))
    zh = np.zeros((seq, half))
    ra = np.concatenate([cos, cos, ones], axis=1)
    rm = np.concatenate([-sin, zh, zeros], axis=1)
    rp = np.concatenate([zh, sin, zeros], axis=1)
    tile = lambda t: jnp.asarray(np.tile(t, (1, LANES // HEAD_DIM)), dtype=F32)
    return tile(ra), tile(rm), tile(rp)


def _tiles(seq):
    tm, tp, tq_sb, tq_da = 512, 256, 256, 512
    for t in (tm, tp, tq_sb, tq_da):
        assert seq % t == 0 and t % LANES == 0 and t % CHUNK == 0
    return tm, tp, tq_sb, tq_da


def kernel(x, g_mix, w_in, g_q, g_k, lam_q1, lam_k1, lam_q2, lam_k2, g_sub, w_branch_a, w_branch_b,
           w_gate, b_gate, w_out, g_ffn, w_ffn_gate, w_ffn_up, w_ffn_down):
    b, s, d = x.shape
    depth = g_mix.shape[0]
    sb_w = SB_HEADS * HEAD_DIM
    da_w = DA_HEADS * 2 * HEAD_DIM
    in_w = 3 * sb_w + 3 * da_w
    n = b * s
    d_ff = w_ffn_gate.shape[-1]

    tm, tp, tq_sb, tq_da = _tiles(s)

    ra, rm, rp = _rope_tables(s)
    grp = np.arange(da_w) // HEAD_DIM
    gmat = jnp.asarray(np.where(grp[:, None] == grp[None, :], 1.0 / HEAD_DIM, 0.0), dtype=BF16)
    tile_g = lambda g: jnp.tile(g, da_w // HEAD_DIM)[None, :]
    vmem_full = pl.BlockSpec(memory_space=pltpu.VMEM)
    spt = s // tm

    xf = x.reshape(n, d)
    for layer in range(depth):
        lambda_init = 0.8 - 0.6 * math.exp(-0.3 * layer)
        gmix = g_mix[layer][None, :]

        proj = pl.pallas_call(
            functools.partial(_proj_kernel, sb_w=sb_w, da_w=da_w),
            grid=(n // tm,),
            in_specs=[
                pl.BlockSpec((tm, d), lambda t: (t, 0)),
                vmem_full, vmem_full, vmem_full, vmem_full, vmem_full,
                pl.BlockSpec((tm, LANES), lambda t: (t % spt, 0)),
                pl.BlockSpec((tm, LANES), lambda t: (t % spt, 0)),
                pl.BlockSpec((tm, LANES), lambda t: (t % spt, 0)),
            ],
            out_specs=pl.BlockSpec((tm, in_w), lambda t: (t, 0)),
            out_shape=jax.ShapeDtypeStruct((n, in_w), BF16),
            compiler_params=pltpu.CompilerParams(
                dimension_semantics=("arbitrary",), vmem_limit_bytes=VMEM_LIMIT),
            name="proj",
        )(xf, gmix, w_in[layer].astype(BF16), tile_g(g_q[layer]), tile_g(g_k[layer]), gmat,
          ra, rm, rp)

        cb = lambda off: off // LANES
        tq, nq, gw = tq_sb, s // tq_sb, SB_GROUPS * LANES
        o_a = pl.pallas_call(
            functools.partial(_sb_kernel, tq=tq, n_grp=SB_GROUPS),
            grid=(b, sb_w // gw, nq),
            in_specs=[
                pl.BlockSpec((tq, gw), lambda bi, j, i, nq=nq: (bi * nq + i, j)),
                pl.BlockSpec((s, gw), lambda bi, j, i: (bi, sb_w // gw + j)),
                pl.BlockSpec((s, gw), lambda bi, j, i: (bi, 2 * sb_w // gw + j)),
            ],
            out_specs=pl.BlockSpec((tq, gw), lambda bi, j, i, nq=nq: (bi * nq + i, j)),
            out_shape=jax.ShapeDtypeStruct((n, sb_w), BF16),
            scratch_shapes=[pltpu.VMEM((SB_GROUPS, 2 * tq, LANES), BF16),
                            pltpu.VMEM((SB_GROUPS, 2 * tq, LANES), F32),
                            pltpu.VMEM((SB_GROUPS, 2 * tq, LANES), F32)],
            compiler_params=pltpu.CompilerParams(
                dimension_semantics=("arbitrary", "arbitrary", "arbitrary"),
                vmem_limit_bytes=VMEM_LIMIT),
            name="sb_attn",
        )(proj, proj, proj)

        tq, nq = tq_da, s // tq_da
        o_b = pl.pallas_call(
            functools.partial(_da_kernel, tq=tq, lambda_init=lambda_init),
            grid=(b, DA_HEADS, nq),
            in_specs=[
                pl.BlockSpec((tq, LANES), lambda bi, j, i, nq=nq: (bi * nq + i, cb(3 * sb_w) + j)),
                pl.BlockSpec((s, LANES), lambda bi, j, i: (bi, cb(3 * sb_w + da_w) + j)),
                pl.BlockSpec((s, LANES), lambda bi, j, i: (bi, cb(3 * sb_w + 2 * da_w) + j)),
                vmem_full, vmem_full, vmem_full, vmem_full, vmem_full,
            ],
            out_specs=pl.BlockSpec((tq, LANES), lambda bi, j, i, nq=nq: (bi * nq + i, j)),
            out_shape=jax.ShapeDtypeStruct((n, da_w), BF16),
            scratch_shapes=[pltpu.VMEM((s, 2 * LANES), BF16),
                            pltpu.VMEM((2 * tq, LANES), BF16),
                            pltpu.VMEM((2, 2 * tq, tq), F32),
                            pltpu.VMEM((2, 2 * tq, LANES), F32)]
            + [pltpu.VMEM((2 * tq, LANES), F32)] * 3,
            compiler_params=pltpu.CompilerParams(
                dimension_semantics=("arbitrary", "arbitrary", "arbitrary"),
                vmem_limit_bytes=VMEM_LIMIT),
            name="da_attn",
        )(proj, proj, proj, lam_q1[layer][None, :], lam_k1[layer][None, :],
          lam_q2[layer][None, :], lam_k2[layer][None, :], g_sub[layer][None, :])

        xf = pl.pallas_call(
            functools.partial(_post_kernel, d_model=d),
            grid=(n // tp,),
            in_specs=[
                pl.BlockSpec((tp, d), lambda t: (t, 0)),
                pl.BlockSpec((tp, sb_w), lambda t: (t, 0)),
                pl.BlockSpec((tp, da_w), lambda t: (t, 0)),
            ] + [vmem_full] * 10,
            out_specs=pl.BlockSpec((tp, d), lambda t: (t, 0)),
            out_shape=jax.ShapeDtypeStruct((n, d), F32),
            compiler_params=pltpu.CompilerParams(
                dimension_semantics=("arbitrary",), vmem_limit_bytes=VMEM_LIMIT),
            name="post",
        )(xf, o_a, o_b, gmix, w_gate[layer].astype(BF16), b_gate[layer][None, :],
          w_branch_a[layer].astype(BF16), w_branch_b[layer].astype(BF16),
          w_out[layer].astype(BF16), g_ffn[layer][None, :],
          w_ffn_gate[layer].astype(BF16), w_ffn_up[layer].astype(BF16),
          w_ffn_down[layer].astype(BF16))
    return xf.reshape(b, s, d)
```

```python
import functools
import math

import jax
import jax.numpy as jnp
import numpy as np
from jax import lax
from jax.experimental import pallas as pl
from jax.experimental.pallas import tpu as pltpu

F32 = jnp.float32
BF16 = jnp.bfloat16

CHUNK = 64
SB_HEADS = 8
DA_HEADS = 4
HEAD_DIM = 64
ROPE_THETA = 500000.0
ROPE_DIM = HEAD_DIM // 4
EPS = 1e-6
NEG_INF = -1e30
LOG2E = math.log2(math.e)
DA_GROUPS = 1
SB_GROUPS = 4
SB_STOP_LOG2 = 150.0
LANES = 128
VMEM_LIMIT = 56 * 1024 * 1024


def _dot(a, b):
    return jnp.dot(a, b, preferred_element_type=F32)


def _dot_nt(a, b):
    return lax.dot_general(a, b, (((1,), (1,)), ((), ())), preferred_element_type=F32)


def _proj_kernel(x_ref, gmix_ref, win_ref, gq_ref, gk_ref, gmat_ref,
                 ra_ref, rm_ref, rp_ref, out_ref, *, sb_w, da_w):
    x = x_ref[...]
    ms = jnp.mean(x * x, axis=-1, keepdims=True)
    h = x * lax.rsqrt(ms + EPS) * gmix_ref[...]
    proj = _dot(h.astype(BF16), win_ref[...])
    scale = HEAD_DIM ** -0.5

    out_ref[:, 0:sb_w] = (proj[:, 0:sb_w] * (scale * LOG2E)).astype(BF16)
    out_ref[:, sb_w:3 * sb_w] = proj[:, sb_w:3 * sb_w].astype(BF16)

    ra, rm, rp = ra_ref[...], rm_ref[...], rp_ref[...]

    def qk_norm_rope(t, g, mult):
        msq = _dot((t * t).astype(BF16), gmat_ref[...])
        tn = t * lax.rsqrt(msq + EPS) * g
        cols = []
        for j in range(da_w // LANES):
            c = tn[:, j * LANES:(j + 1) * LANES]
            r = (c * ra + pltpu.roll(c, LANES - ROPE_DIM // 2, 1) * rm
                 + pltpu.roll(c, ROPE_DIM // 2, 1) * rp)
            cols.append((r * mult).astype(BF16))
        return jnp.concatenate(cols, axis=1)

    o = 3 * sb_w
    out_ref[:, o:o + da_w] = qk_norm_rope(proj[:, o:o + da_w], gq_ref[...], scale * LOG2E)
    out_ref[:, o + da_w:o + 2 * da_w] = qk_norm_rope(proj[:, o + da_w:o + 2 * da_w], gk_ref[...], 1.0)
    out_ref[:, o + 2 * da_w:o + 3 * da_w] = proj[:, o + 2 * da_w:o + 3 * da_w].astype(BF16)


def _stack_masked(q, n_parts):
    lane = lax.broadcasted_iota(jnp.int32, q.shape, 1)
    zero = jnp.zeros_like(q)
    return jnp.concatenate(
        [jnp.where((lane >= p * HEAD_DIM) & (lane < (p + 1) * HEAD_DIM), q, zero)
         for p in range(n_parts)], axis=0)


def _sb_kernel(q_ref, k_ref, v_ref, o_ref, qs_ref, acc_ref, run_ref, *, tq, n_grp):
    i = pl.program_id(2)
    sub = LANES
    m2 = 2 * tq
    r2 = lax.broadcasted_iota(jnp.int32, (sub, 2 * sub), 0)
    c2 = lax.broadcasted_iota(jnp.int32, (sub, 2 * sub), 1)
    tri = jnp.where((c2 >= sub) | (r2 > c2), 1.0, 0.0).astype(BF16)
    lanes = lambda g: slice(g * LANES, (g + 1) * LANES)

    for g in range(n_grp):
        qs_ref[g] = _stack_masked(q_ref[:, lanes(g)], 2)

    def block(g, blk, run, diagonal=False):
        ks = pl.multiple_of(blk * tq, tq)
        z = _dot_nt(qs_ref[g], k_ref[pl.ds(ks, tq), lanes(g)])
        p = jnp.maximum(z, 0.0) + jnp.log2(1.0 + jnp.exp2(-jnp.abs(z)))
        if diagonal:
            row = lax.broadcasted_iota(jnp.int32, (m2, tq), 0)
            col = lax.broadcasted_iota(jnp.int32, (m2, tq), 1)
            mask = col < jnp.where(row >= tq, row - tq, row)
            p = jnp.where(mask, p, 0.0)
        pb = p.astype(BF16)
        n_sub = tq // sub
        spent = [None] * n_sub
        for c in reversed(range(n_sub)):
            sl = slice(c * sub, (c + 1) * sub)
            cs = _dot(pb[:, sl], tri)
            spent[c] = cs[:, :sub] + run
            run = run + cs[:, sub:]
        a = jnp.exp2((z - p) - jnp.concatenate(spent, axis=1))
        if diagonal:
            a = jnp.where(mask, a, 0.0)
        return run, _dot(a.astype(BF16), v_ref[pl.ds(ks, tq), lanes(g)])

    zeros = jnp.zeros((m2, LANES), F32)

    @pl.when(i == 0)
    def _():
        for g in range(n_grp):
            run_ref[g], acc_ref[g] = block(g, i, zeros, diagonal=True)

    @pl.when(i > 0)
    def _():
        for g in range(n_grp):
            run, out = block(g, i, zeros, diagonal=True)
            run, out2 = block(g, i - 1, run)
            run_ref[g] = run
            acc_ref[g] = out + out2

    def more(carry):
        blk, least = carry
        return jnp.logical_and(blk >= 0, least < SB_STOP_LOG2)

    def step(carry):
        blk, _ = carry
        least = None
        for g in range(n_grp):
            run, out = block(g, blk, run_ref[g])
            run_ref[g] = run
            acc_ref[g] += out
            least = jnp.min(run) if least is None else jnp.minimum(least, jnp.min(run))
        return blk - 1, least

    lax.while_loop(more, step, (i - 2, jnp.min(run_ref[...])))
    lane = lax.broadcasted_iota(jnp.int32, (tq, LANES), 1)
    for g in range(n_grp):
        o_ref[:, lanes(g)] = jnp.where(lane < HEAD_DIM, acc_ref[g, 0:tq, :],
                                       acc_ref[g, tq:m2, :]).astype(o_ref.dtype)


def _da_kernel(q_ref, k_ref, v_ref, lq1_ref, lk1_ref, lq2_ref, lk2_ref, gsub_ref, o_ref,
               vext_ref, qs_ref, s_ref, mc_ref, m_ref, l_ref, acc_ref, *, tq, n_grp, lambda_init):
    i = pl.program_id(2)
    m2 = 2 * tq
    reps = tq // LANES
    groups = range(n_grp)
    lanes = lambda g: slice(g * LANES, (g + 1) * LANES)

    @pl.when(i == 0)
    def _():
        for g in groups:
            vext_ref[g, :, 0:LANES] = v_ref[:, lanes(g)]
            vext_ref[g, :, LANES:2 * LANES] = jnp.ones((v_ref.shape[0], LANES), v_ref.dtype)

    for g in groups:
        qs_ref[g] = _stack_masked(q_ref[:, lanes(g)], 2)
    m_ref[...] = jnp.full_like(m_ref, NEG_INF)
    l_ref[...] = jnp.zeros_like(l_ref)
    acc_ref[...] = jnp.zeros_like(acc_ref)

    def scores(slot, blk):
        ks = pl.multiple_of(blk * tq, tq)
        for g in groups:
            s = _dot_nt(qs_ref[g], k_ref[pl.ds(ks, tq), lanes(g)])
            s_ref[g, slot] = s
            mc_ref[g, slot] = jnp.broadcast_to(jnp.max(s, axis=1, keepdims=True), (m2, LANES))

    def absorb(slot, blk, diagonal=False):
        ks = pl.multiple_of(blk * tq, tq)
        for g in groups:
            s = s_ref[g, slot]
            if diagonal:
                row = lax.broadcasted_iota(jnp.int32, (m2, tq), 0)
                col = lax.broadcasted_iota(jnp.int32, (m2, tq), 1)
                qrow = jnp.where(row >= tq, row - tq, row)
                s = jnp.where((col // CHUNK) <= (qrow // CHUNK), s, NEG_INF)
                m_cur = jnp.max(s, axis=1, keepdims=True)
            else:
                m_cur = mc_ref[g, slot]
            m_prev = m_ref[g]
            m_new = jnp.maximum(m_prev, m_cur)
            p = jnp.exp2(s - jnp.concatenate([m_new] * reps, axis=1))
            alpha = jnp.exp2(m_prev - m_new)
            pv = _dot(p.astype(BF16), vext_ref[g, pl.ds(ks, tq), :])
            acc_ref[g] = alpha * acc_ref[g] + pv[:, :LANES]
            l_ref[g] = alpha * l_ref[g] + pv[:, LANES:]
            m_ref[g] = m_new

    scores(0, 0)

    def body(pair, _):
        blk = 2 * pair
        scores(1, blk + 1)
        absorb(0, blk)
        scores(0, blk + 2)
        absorb(1, blk + 1)
        return 0

    lax.fori_loop(0, i // 2, body, 0)

    @pl.when(i % 2 == 0)
    def _():
        absorb(0, i, diagonal=True)

    @pl.when(i % 2 == 1)
    def _():
        scores(1, i)
        absorb(0, i - 1)
        absorb(1, i, diagonal=True)

    lam = (jnp.exp(jnp.sum(lq1_ref[...] * lk1_ref[...], axis=-1, keepdims=True))
           - jnp.exp(jnp.sum(lq2_ref[...] * lk2_ref[...], axis=-1, keepdims=True))
           + lambda_init)
    for g in groups:
        res = acc_ref[g] / l_ref[g]
        o = res[0:tq, :] - lam * res[tq:m2, :]
        ms = jnp.mean(o * o, axis=-1, keepdims=True)
        o = o * lax.rsqrt(ms + EPS) * gsub_ref[...] * (1.0 - lambda_init)
        o_ref[:, lanes(g)] = o.astype(o_ref.dtype)


def _post_kernel(x_ref, oa_ref, ob_ref, gmix_ref, wgate_ref, bgate_ref, wa_ref, wb_ref, wout_ref,
                 gffn_ref, wfg_ref, wfu_ref, wfd_ref, out_ref, *, d_model):
    x = x_ref[...]
    ms = jnp.mean(x * x, axis=-1, keepdims=True)
    h = (x * lax.rsqrt(ms + EPS) * gmix_ref[...]).astype(BF16)
    gates = jax.nn.sigmoid(_dot(h, wgate_ref[...]) + bgate_ref[...])
    br_a = _dot(oa_ref[...], wa_ref[...])
    br_b = _dot(ob_ref[...], wb_ref[...])
    merged = gates[:, :d_model] * br_a + gates[:, d_model:] * br_b
    x1 = x + _dot(merged.astype(BF16), wout_ref[...])

    ms2 = jnp.mean(x1 * x1, axis=-1, keepdims=True)
    h2 = (x1 * lax.rsqrt(ms2 + EPS) * gffn_ref[...]).astype(BF16)
    fg = _dot(h2, wfg_ref[...])
    fu = _dot(h2, wfu_ref[...])
    ff = (fg * jax.nn.sigmoid(fg)) * fu
    out_ref[...] = x1 + _dot(ff.astype(BF16), wfd_ref[...])


def _rope_tables(seq):
    half = ROPE_DIM // 2
    pos = np.arange(seq, dtype=np.float64)
    inv_freq = ROPE_THETA ** (-np.arange(0, ROPE_DIM, 2, dtype=np.float64) / ROPE_DIM)
    ang = pos[:, None] * inv_freq[None, :]
    cos, sin = np.cos(ang), np.sin(ang)
    ones = np.ones((seq, HEAD_DIM - ROPE_DIM))
    zeros = np.zeros((seq, HEAD_DIM - ROPE_DIM))
    zh = np.zeros((seq, half))
    ra = np.concatenate([cos, cos, ones], axis=1)
    rm = np.concatenate([-sin, zh, zeros], axis=1)
    rp = np.concatenate([zh, sin, zeros], axis=1)
    tile = lambda t: jnp.asarray(np.tile(t, (1, LANES // HEAD_DIM)), dtype=F32)
    return tile(ra), tile(rm), tile(rp)


def _tiles(seq):
    tm, tp, tq_sb, tq_da = 512, 256, 256, 512
    for t in (tm, tp, tq_sb, tq_da):
        assert seq % t == 0 and t % LANES == 0 and t % CHUNK == 0
    return tm, tp, tq_sb, tq_da


def kernel(x, g_mix, w_in, g_q, g_k, lam_q1, lam_k1, lam_q2, lam_k2, g_sub, w_branch_a, w_branch_b,
           w_gate, b_gate, w_out, g_ffn, w_ffn_gate, w_ffn_up, w_ffn_down):
    b, s, d = x.shape
    depth = g_mix.shape[0]
    sb_w = SB_HEADS * HEAD_DIM
    da_w = DA_HEADS * 2 * HEAD_DIM
    in_w = 3 * sb_w + 3 * da_w
    n = b * s
    d_ff = w_ffn_gate.shape[-1]

    tm, tp, tq_sb, tq_da = _tiles(s)

    ra, rm, rp = _rope_tables(s)
    grp = np.arange(da_w) // HEAD_DIM
    gmat = jnp.asarray(np.where(grp[:, None] == grp[None, :], 1.0 / HEAD_DIM, 0.0), dtype=BF16)
    tile_g = lambda g: jnp.tile(g, da_w // HEAD_DIM)[None, :]
    vmem_full = pl.BlockSpec(memory_space=pltpu.VMEM)
    spt = s // tm

    xf = x.reshape(n, d)
    for layer in range(depth):
        lambda_init = 0.8 - 0.6 * math.exp(-0.3 * layer)
        gmix = g_mix[layer][None, :]

        proj = pl.pallas_call(
            functools.partial(_proj_kernel, sb_w=sb_w, da_w=da_w),
            grid=(n // tm,),
            in_specs=[
                pl.BlockSpec((tm, d), lambda t: (t, 0)),
                vmem_full, vmem_full, vmem_full, vmem_full, vmem_full,
                pl.BlockSpec((tm, LANES), lambda t: (t % spt, 0)),
                pl.BlockSpec((tm, LANES), lambda t: (t % spt, 0)),
                pl.BlockSpec((tm, LANES), lambda t: (t % spt, 0)),
            ],
            out_specs=pl.BlockSpec((tm, in_w), lambda t: (t, 0)),
            out_shape=jax.ShapeDtypeStruct((n, in_w), BF16),
            compiler_params=pltpu.CompilerParams(
                dimension_semantics=("arbitrary",), vmem_limit_bytes=VMEM_LIMIT),
            name="proj",
        )(xf, gmix, w_in[layer].astype(BF16), tile_g(g_q[layer]), tile_g(g_k[layer]), gmat,
          ra, rm, rp)

        cb = lambda off: off // LANES
        tq, nq, gw = tq_sb, s // tq_sb, SB_GROUPS * LANES
        o_a = pl.pallas_call(
            functools.partial(_sb_kernel, tq=tq, n_grp=SB_GROUPS),
            grid=(b, sb_w // gw, nq),
            in_specs=[
                pl.BlockSpec((tq, gw), lambda bi, j, i, nq=nq: (bi * nq + i, j)),
                pl.BlockSpec((s, gw), lambda bi, j, i: (bi, sb_w // gw + j)),
                pl.BlockSpec((s, gw), lambda bi, j, i: (bi, 2 * sb_w // gw + j)),
            ],
            out_specs=pl.BlockSpec((tq, gw), lambda bi, j, i, nq=nq: (bi * nq + i, j)),
            out_shape=jax.ShapeDtypeStruct((n, sb_w), BF16),
            scratch_shapes=[pltpu.VMEM((SB_GROUPS, 2 * tq, LANES), BF16),
                            pltpu.VMEM((SB_GROUPS, 2 * tq, LANES), F32),
                            pltpu.VMEM((SB_GROUPS, 2 * tq, LANES), F32)],
            compiler_params=pltpu.CompilerParams(
                dimension_semantics=("arbitrary", "arbitrary", "arbitrary"),
                vmem_limit_bytes=VMEM_LIMIT),
            name="sb_attn",
        )(proj, proj, proj)

        tq, nq, ng, gw = tq_da, s // tq_da, DA_GROUPS, DA_GROUPS * LANES
        o_b = pl.pallas_call(
            functools.partial(_da_kernel, tq=tq, n_grp=ng, lambda_init=lambda_init),
            grid=(b, da_w // gw, nq),
            in_specs=[
                pl.BlockSpec((tq, gw), lambda bi, j, i, nq=nq: (bi * nq + i, 3 * sb_w // gw + j)),
                pl.BlockSpec((s, gw), lambda bi, j, i: (bi, (3 * sb_w + da_w) // gw + j)),
                pl.BlockSpec((s, gw), lambda bi, j, i: (bi, (3 * sb_w + 2 * da_w) // gw + j)),
                vmem_full, vmem_full, vmem_full, vmem_full, vmem_full,
            ],
            out_specs=pl.BlockSpec((tq, gw), lambda bi, j, i, nq=nq: (bi * nq + i, j)),
            out_shape=jax.ShapeDtypeStruct((n, da_w), BF16),
            scratch_shapes=[pltpu.VMEM((ng, s, 2 * LANES), BF16),
                            pltpu.VMEM((ng, 2 * tq, LANES), BF16),
                            pltpu.VMEM((ng, 2, 2 * tq, tq), F32),
                            pltpu.VMEM((ng, 2, 2 * tq, LANES), F32)]
            + [pltpu.VMEM((ng, 2 * tq, LANES), F32)] * 3,
            compiler_params=pltpu.CompilerParams(
                dimension_semantics=("arbitrary", "arbitrary", "arbitrary"),
                vmem_limit_bytes=VMEM_LIMIT),
            name="da_attn",
        )(proj, proj, proj, lam_q1[layer][None, :], lam_k1[layer][None, :],
          lam_q2[layer][None, :], lam_k2[layer][None, :], g_sub[layer][None, :])

        xf = pl.pallas_call(
            functools.partial(_post_kernel, d_model=d),
            grid=(n // tp,),
            in_specs=[
                pl.BlockSpec((tp, d), lambda t: (t, 0)),
                pl.BlockSpec((tp, sb_w), lambda t: (t, 0)),
                pl.BlockSpec((tp, da_w), lambda t: (t, 0)),
            ] + [vmem_full] * 10,
            out_specs=pl.BlockSpec((tp, d), lambda t: (t, 0)),
            out_shape=jax.ShapeDtypeStruct((n, d), F32),
            compiler_params=pltpu.CompilerParams(
                dimension_semantics=("arbitrary",), vmem_limit_bytes=VMEM_LIMIT),
            name="post",
        )(xf, o_a, o_b, gmix, w_gate[layer].astype(BF16), b_gate[layer][None, :],
          w_branch_a[layer].astype(BF16), w_branch_b[layer].astype(BF16),
          w_out[layer].astype(BF16), g_ffn[layer][None, :],
          w_ffn_gate[layer].astype(BF16), w_ffn_up[layer].astype(BF16),
          w_ffn_down[layer].astype(BF16))
    return xf.reshape(b, s, d)
```

```python
import functools
import math

import jax
import jax.numpy as jnp
import numpy as np
from jax import lax
from jax.experimental import pallas as pl
from jax.experimental.pallas import tpu as pltpu

F32 = jnp.float32
BF16 = jnp.bfloat16

CHUNK = 64
SB_HEADS = 8
DA_HEADS = 4
HEAD_DIM = 64
ROPE_THETA = 500000.0
ROPE_DIM = HEAD_DIM // 4
EPS = 1e-6
NEG_INF = -1e30
LOG2E = math.log2(math.e)
SB_GROUPS = 4
SB_STOP_LOG2 = 150.0
LANES = 128
VMEM_LIMIT = 56 * 1024 * 1024


def _dot(a, b):
    return jnp.dot(a, b, preferred_element_type=F32)


def _dot_nt(a, b):
    return lax.dot_general(a, b, (((1,), (1,)), ((), ())), preferred_element_type=F32)


def _proj_kernel(x_ref, gmix_ref, win_ref, gq_ref, gk_ref, gmat_ref,
                 ra_ref, rm_ref, rp_ref, out_ref, *, sb_w, da_w):
    x = x_ref[...]
    ms = jnp.mean(x * x, axis=-1, keepdims=True)
    h = x * lax.rsqrt(ms + EPS) * gmix_ref[...]
    proj = _dot(h.astype(BF16), win_ref[...])
    scale = HEAD_DIM ** -0.5

    out_ref[:, 0:sb_w] = (proj[:, 0:sb_w] * (scale * LOG2E)).astype(BF16)
    out_ref[:, sb_w:3 * sb_w] = proj[:, sb_w:3 * sb_w].astype(BF16)

    ra, rm, rp = ra_ref[...], rm_ref[...], rp_ref[...]

    def qk_norm_rope(t, g, mult):
        msq = _dot((t * t).astype(BF16), gmat_ref[...])
        tn = t * lax.rsqrt(msq + EPS) * g
        cols = []
        for j in range(da_w // LANES):
            c = tn[:, j * LANES:(j + 1) * LANES]
            r = (c * ra + pltpu.roll(c, LANES - ROPE_DIM // 2, 1) * rm
                 + pltpu.roll(c, ROPE_DIM // 2, 1) * rp)
            cols.append((r * mult).astype(BF16))
        return jnp.concatenate(cols, axis=1)

    o = 3 * sb_w
    out_ref[:, o:o + da_w] = qk_norm_rope(proj[:, o:o + da_w], gq_ref[...], scale * LOG2E)
    out_ref[:, o + da_w:o + 2 * da_w] = qk_norm_rope(proj[:, o + da_w:o + 2 * da_w], gk_ref[...], 1.0)
    out_ref[:, o + 2 * da_w:o + 3 * da_w] = proj[:, o + 2 * da_w:o + 3 * da_w].astype(BF16)


def _stack_masked(q, n_parts):
    lane = lax.broadcasted_iota(jnp.int32, q.shape, 1)
    zero = jnp.zeros_like(q)
    return jnp.concatenate(
        [jnp.where((lane >= p * HEAD_DIM) & (lane < (p + 1) * HEAD_DIM), q, zero)
         for p in range(n_parts)], axis=0)


def _sb_kernel(q_ref, k_ref, v_ref, o_ref, qs_ref, acc_ref, run_ref, *, tq, n_grp):
    i = pl.program_id(2)
    sub = LANES
    m2 = 2 * tq
    r2 = lax.broadcasted_iota(jnp.int32, (sub, 2 * sub), 0)
    c2 = lax.broadcasted_iota(jnp.int32, (sub, 2 * sub), 1)
    tri = jnp.where((c2 >= sub) | (r2 > c2), 1.0, 0.0).astype(BF16)
    lanes = lambda g: slice(g * LANES, (g + 1) * LANES)

    for g in range(n_grp):
        qs_ref[g] = _stack_masked(q_ref[:, lanes(g)], 2)

    def block(g, blk, run, diagonal=False):
        ks = pl.multiple_of(blk * tq, tq)
        z = _dot_nt(qs_ref[g], k_ref[pl.ds(ks, tq), lanes(g)])
        p = jnp.maximum(z, 0.0) + jnp.log2(1.0 + jnp.exp2(-jnp.abs(z)))
        if diagonal:
            row = lax.broadcasted_iota(jnp.int32, (m2, tq), 0)
            col = lax.broadcasted_iota(jnp.int32, (m2, tq), 1)
            mask = col < jnp.where(row >= tq, row - tq, row)
            p = jnp.where(mask, p, 0.0)
        pb = p.astype(BF16)
        n_sub = tq // sub
        spent = [None] * n_sub
        for c in reversed(range(n_sub)):
            sl = slice(c * sub, (c + 1) * sub)
            cs = _dot(pb[:, sl], tri)
            spent[c] = cs[:, :sub] + run
            run = run + cs[:, sub:]
        a = jnp.exp2((z - p) - jnp.concatenate(spent, axis=1))
        if diagonal:
            a = jnp.where(mask, a, 0.0)
        return run, _dot(a.astype(BF16), v_ref[pl.ds(ks, tq), lanes(g)])

    zeros = jnp.zeros((m2, LANES), F32)

    @pl.when(i == 0)
    def _():
        for g in range(n_grp):
            run_ref[g], acc_ref[g] = block(g, i, zeros, diagonal=True)

    @pl.when(i > 0)
    def _():
        for g in range(n_grp):
            run, out = block(g, i, zeros, diagonal=True)
            run, out2 = block(g, i - 1, run)
            run_ref[g] = run
            acc_ref[g] = out + out2

    def more(carry):
        blk, least = carry
        return jnp.logical_and(blk >= 0, least < SB_STOP_LOG2)

    def step(carry):
        blk, _ = carry
        least = None
        for g in range(n_grp):
            run, out = block(g, blk, run_ref[g])
            run_ref[g] = run
            acc_ref[g] += out
            least = jnp.min(run) if least is None else jnp.minimum(least, jnp.min(run))
        return blk - 1, least

    lax.while_loop(more, step, (i - 2, jnp.min(run_ref[...])))
    lane = lax.broadcasted_iota(jnp.int32, (tq, LANES), 1)
    for g in range(n_grp):
        o_ref[:, lanes(g)] = jnp.where(lane < HEAD_DIM, acc_ref[g, 0:tq, :],
                                       acc_ref[g, tq:m2, :]).astype(o_ref.dtype)


def _da_flat_kernel(q_ref, k_ref, v_ref, lq1_ref, lk1_ref, lq2_ref, lk2_ref, gsub_ref, o_ref,
                    vext_ref, qs_ref, s_ref, mc_ref, m_ref, l_ref, acc_ref, *, tq, n_tiles,
                    lambda_init):
    m2 = 2 * tq
    reps = tq // LANES
    vext_ref[:, 0:LANES] = v_ref[...]
    vext_ref[:, LANES:2 * LANES] = jnp.ones(v_ref.shape, v_ref.dtype)
    lam = (jnp.exp(jnp.sum(lq1_ref[...] * lk1_ref[...], axis=-1, keepdims=True))
           - jnp.exp(jnp.sum(lq2_ref[...] * lk2_ref[...], axis=-1, keepdims=True))
           + lambda_init)
    steps = [(i, blk) for i in range(n_tiles) for blk in range(i + 1)]
    rows = lambda j: slice(j * tq, (j + 1) * tq)

    def scores(t):
        i, blk = steps[t]
        if blk == 0:
            qs_ref[i % 2] = _stack_masked(q_ref[rows(i), :], 2)
        s = _dot_nt(qs_ref[i % 2], k_ref[rows(blk), :])
        s_ref[t % 2] = s
        mc_ref[t % 2] = jnp.broadcast_to(jnp.max(s, axis=1, keepdims=True), (m2, LANES))

    def absorb(t):
        i, blk = steps[t]
        s = s_ref[t % 2]
        if blk == i:
            row = lax.broadcasted_iota(jnp.int32, (m2, tq), 0)
            col = lax.broadcasted_iota(jnp.int32, (m2, tq), 1)
            qrow = jnp.where(row >= tq, row - tq, row)
            s = jnp.where((col // CHUNK) <= (qrow // CHUNK), s, NEG_INF)
            m_cur = jnp.broadcast_to(jnp.max(s, axis=1, keepdims=True), (m2, LANES))
        else:
            m_cur = mc_ref[t % 2]
        m_new = m_cur if blk == 0 else jnp.maximum(m_ref[...], m_cur)
        p = jnp.exp2(s - jnp.concatenate([m_new] * reps, axis=1))
        pv = _dot(p.astype(BF16), vext_ref[rows(blk), :])
        if blk == 0:
            acc, l = pv[:, :LANES], pv[:, LANES:]
        else:
            alpha = jnp.exp2(m_ref[...] - m_new)
            acc = alpha * acc_ref[...] + pv[:, :LANES]
            l = alpha * l_ref[...] + pv[:, LANES:]
        if blk < i:
            acc_ref[...], l_ref[...], m_ref[...] = acc, l, m_new
        else:
            res = acc / l
            o = res[0:tq, :] - lam * res[tq:m2, :]
            ms = jnp.mean(o * o, axis=-1, keepdims=True)
            o = o * lax.rsqrt(ms + EPS) * gsub_ref[...] * (1.0 - lambda_init)
            o_ref[rows(i), :] = o.astype(o_ref.dtype)

    scores(0)
    for t in range(len(steps)):
        if t + 1 < len(steps):
            scores(t + 1)
        absorb(t)


def _post_kernel(x_ref, oa_ref, ob_ref, gmix_ref, wgate_ref, bgate_ref, wa_ref, wb_ref, wout_ref,
                 gffn_ref, wfg_ref, wfu_ref, wfd_ref, out_ref, *, d_model):
    x = x_ref[...]
    ms = jnp.mean(x * x, axis=-1, keepdims=True)
    h = (x * lax.rsqrt(ms + EPS) * gmix_ref[...]).astype(BF16)
    gates = jax.nn.sigmoid(_dot(h, wgate_ref[...]) + bgate_ref[...])
    br_a = _dot(oa_ref[...], wa_ref[...])
    br_b = _dot(ob_ref[...], wb_ref[...])
    merged = gates[:, :d_model] * br_a + gates[:, d_model:] * br_b
    x1 = x + _dot(merged.astype(BF16), wout_ref[...])

    ms2 = jnp.mean(x1 * x1, axis=-1, keepdims=True)
    h2 = (x1 * lax.rsqrt(ms2 + EPS) * gffn_ref[...]).astype(BF16)
    fg = _dot(h2, wfg_ref[...])
    fu = _dot(h2, wfu_ref[...])
    ff = (fg * jax.nn.sigmoid(fg)) * fu
    out_ref[...] = x1 + _dot(ff.astype(BF16), wfd_ref[...])


def _rope_tables(seq):
    half = ROPE_DIM // 2
    pos = np.arange(seq, dtype=np.float64)
    inv_freq = ROPE_THETA ** (-np.arange(0, ROPE_DIM, 2, dtype=np.float64) / ROPE_DIM)
    ang = pos[:, None] * inv_freq[None, :]
    cos, sin = np.cos(ang), np.sin(ang)
    ones = np.ones((seq, HEAD_DIM - ROPE_DIM))
    zeros = np.zeros((seq, HEAD_DIM - ROPE_DIM))
    zh = np.zeros((seq, half))
    ra = np.concatenate([cos, cos, ones], axis=1)
    rm = np.concatenate([-sin, zh, zeros], axis=1)
    rp = np.concatenate([zh, sin, zeros], axis=1)
    tile = lambda t: jnp.asarray(np.tile(t, (1, LANES // HEAD_DIM)), dtype=F32)
    return tile(ra), tile(rm), tile(rp)


def _tiles(seq):
    tm, tp, tq_sb, tq_da = 512, 256, 256, 512
    for t in (tm, tp, tq_sb, tq_da):
        assert seq % t == 0 and t % LANES == 0 and t % CHUNK == 0
    return tm, tp, tq_sb, tq_da


def kernel(x, g_mix, w_in, g_q, g_k, lam_q1, lam_k1, lam_q2, lam_k2, g_sub, w_branch_a, w_branch_b,
           w_gate, b_gate, w_out, g_ffn, w_ffn_gate, w_ffn_up, w_ffn_down):
    b, s, d = x.shape
    depth = g_mix.shape[0]
    sb_w = SB_HEADS * HEAD_DIM
    da_w = DA_HEADS * 2 * HEAD_DIM
    in_w = 3 * sb_w + 3 * da_w
    n = b * s
    d_ff = w_ffn_gate.shape[-1]

    tm, tp, tq_sb, tq_da = _tiles(s)

    ra, rm, rp = _rope_tables(s)
    grp = np.arange(da_w) // HEAD_DIM
    gmat = jnp.asarray(np.where(grp[:, None] == grp[None, :], 1.0 / HEAD_DIM, 0.0), dtype=BF16)
    tile_g = lambda g: jnp.tile(g, da_w // HEAD_DIM)[None, :]
    vmem_full = pl.BlockSpec(memory_space=pltpu.VMEM)
    spt = s // tm

    xf = x.reshape(n, d)
    for layer in range(depth):
        lambda_init = 0.8 - 0.6 * math.exp(-0.3 * layer)
        gmix = g_mix[layer][None, :]

        proj = pl.pallas_call(
            functools.partial(_proj_kernel, sb_w=sb_w, da_w=da_w),
            grid=(n // tm,),
            in_specs=[
                pl.BlockSpec((tm, d), lambda t: (t, 0)),
                vmem_full, vmem_full, vmem_full, vmem_full, vmem_full,
                pl.BlockSpec((tm, LANES), lambda t: (t % spt, 0)),
                pl.BlockSpec((tm, LANES), lambda t: (t % spt, 0)),
                pl.BlockSpec((tm, LANES), lambda t: (t % spt, 0)),
            ],
            out_specs=pl.BlockSpec((tm, in_w), lambda t: (t, 0)),
            out_shape=jax.ShapeDtypeStruct((n, in_w), BF16),
            compiler_params=pltpu.CompilerParams(
                dimension_semantics=("arbitrary",), vmem_limit_bytes=VMEM_LIMIT),
            name="proj",
        )(xf, gmix, w_in[layer].astype(BF16), tile_g(g_q[layer]), tile_g(g_k[layer]), gmat,
          ra, rm, rp)

        cb = lambda off: off // LANES
        tq, nq, gw = tq_sb, s // tq_sb, SB_GROUPS * LANES
        o_a = pl.pallas_call(
            functools.partial(_sb_kernel, tq=tq, n_grp=SB_GROUPS),
            grid=(b, sb_w // gw, nq),
            in_specs=[
                pl.BlockSpec((tq, gw), lambda bi, j, i, nq=nq: (bi * nq + i, j)),
                pl.BlockSpec((s, gw), lambda bi, j, i: (bi, sb_w // gw + j)),
                pl.BlockSpec((s, gw), lambda bi, j, i: (bi, 2 * sb_w // gw + j)),
            ],
            out_specs=pl.BlockSpec((tq, gw), lambda bi, j, i, nq=nq: (bi * nq + i, j)),
            out_shape=jax.ShapeDtypeStruct((n, sb_w), BF16),
            scratch_shapes=[pltpu.VMEM((SB_GROUPS, 2 * tq, LANES), BF16),
                            pltpu.VMEM((SB_GROUPS, 2 * tq, LANES), F32),
                            pltpu.VMEM((SB_GROUPS, 2 * tq, LANES), F32)],
            compiler_params=pltpu.CompilerParams(
                dimension_semantics=("arbitrary", "arbitrary", "arbitrary"),
                vmem_limit_bytes=VMEM_LIMIT),
            name="sb_attn",
        )(proj, proj, proj)

        tq = tq_da
        o_b = pl.pallas_call(
            functools.partial(_da_flat_kernel, tq=tq, n_tiles=s // tq, lambda_init=lambda_init),
            grid=(b, DA_HEADS),
            in_specs=[
                pl.BlockSpec((s, LANES), lambda bi, j: (bi, cb(3 * sb_w) + j)),
                pl.BlockSpec((s, LANES), lambda bi, j: (bi, cb(3 * sb_w + da_w) + j)),
                pl.BlockSpec((s, LANES), lambda bi, j: (bi, cb(3 * sb_w + 2 * da_w) + j)),
                vmem_full, vmem_full, vmem_full, vmem_full, vmem_full,
            ],
            out_specs=pl.BlockSpec((s, LANES), lambda bi, j: (bi, j)),
            out_shape=jax.ShapeDtypeStruct((n, da_w), BF16),
            scratch_shapes=[pltpu.VMEM((s, 2 * LANES), BF16),
                            pltpu.VMEM((2, 2 * tq, LANES), BF16),
                            pltpu.VMEM((2, 2 * tq, tq), F32),
                            pltpu.VMEM((2, 2 * tq, LANES), F32)]
            + [pltpu.VMEM((2 * tq, LANES), F32)] * 3,
            compiler_params=pltpu.CompilerParams(
                dimension_semantics=("arbitrary", "arbitrary"),
                vmem_limit_bytes=VMEM_LIMIT),
            name="da_attn",
        )(proj, proj, proj, lam_q1[layer][None, :], lam_k1[layer][None, :],
          lam_q2[layer][None, :], lam_k2[layer][None, :], g_sub[layer][None, :])

        xf = pl.pallas_call(
            functools.partial(_post_kernel, d_model=d),
            grid=(n // tp,),
            in_specs=[
                pl.BlockSpec((tp, d), lambda t: (t, 0)),
                pl.BlockSpec((tp, sb_w), lambda t: (t, 0)),
                pl.BlockSpec((tp, da_w), lambda t: (t, 0)),
            ] + [vmem_full] * 10,
            out_specs=pl.BlockSpec((tp, d), lambda t: (t, 0)),
            out_shape=jax.ShapeDtypeStruct((n, d), F32),
            compiler_params=pltpu.CompilerParams(
                dimension_semantics=("arbitrary",), vmem_limit_bytes=VMEM_LIMIT),
            name="post",
        )(xf, o_a, o_b, gmix, w_gate[layer].astype(BF16), b_gate[layer][None, :],
          w_branch_a[layer].astype(BF16), w_branch_b[layer].astype(BF16),
          w_out[layer].astype(BF16), g_ffn[layer][None, :],
          w_ffn_gate[layer].astype(BF16), w_ffn_up[layer].astype(BF16),
          w_ffn_down[layer].astype(BF16))
    return xf.reshape(b, s, d)
```

```python
import functools
import math

import jax
import jax.numpy as jnp
import numpy as np
from jax import lax
from jax.experimental import pallas as pl
from jax.experimental.pallas import tpu as pltpu

F32 = jnp.float32
BF16 = jnp.bfloat16

CHUNK = 64
SB_HEADS = 8
DA_HEADS = 4
HEAD_DIM = 64
ROPE_THETA = 500000.0
ROPE_DIM = HEAD_DIM // 4
EPS = 1e-6
NEG_INF = -1e30
LOG2E = math.log2(math.e)
SB_GROUPS = 4
SB_STOP_LOG2 = 150.0
LANES = 128
VMEM_LIMIT = 56 * 1024 * 1024


def _dot(a, b):
    return jnp.dot(a, b, preferred_element_type=F32)


def _dot_nt(a, b):
    return lax.dot_general(a, b, (((1,), (1,)), ((), ())), preferred_element_type=F32)


def _proj_kernel(x_ref, gmix_ref, win_ref, gq_ref, gk_ref, gmat_ref,
                 ra_ref, rm_ref, rp_ref, out_ref, *, sb_w, da_w):
    x = x_ref[...]
    ms = jnp.mean(x * x, axis=-1, keepdims=True)
    h = x * lax.rsqrt(ms + EPS) * gmix_ref[...]
    proj = _dot(h.astype(BF16), win_ref[...])
    scale = HEAD_DIM ** -0.5

    out_ref[:, 0:sb_w] = (proj[:, 0:sb_w] * (scale * LOG2E)).astype(BF16)
    out_ref[:, sb_w:3 * sb_w] = proj[:, sb_w:3 * sb_w].astype(BF16)

    ra, rm, rp = ra_ref[...], rm_ref[...], rp_ref[...]

    def qk_norm_rope(t, g, mult):
        msq = _dot((t * t).astype(BF16), gmat_ref[...])
        tn = t * lax.rsqrt(msq + EPS) * g
        cols = []
        for j in range(da_w // LANES):
            c = tn[:, j * LANES:(j + 1) * LANES]
            r = (c * ra + pltpu.roll(c, LANES - ROPE_DIM // 2, 1) * rm
                 + pltpu.roll(c, ROPE_DIM // 2, 1) * rp)
            cols.append((r * mult).astype(BF16))
        return jnp.concatenate(cols, axis=1)

    o = 3 * sb_w
    out_ref[:, o:o + da_w] = qk_norm_rope(proj[:, o:o + da_w], gq_ref[...], scale * LOG2E)
    out_ref[:, o + da_w:o + 2 * da_w] = qk_norm_rope(proj[:, o + da_w:o + 2 * da_w], gk_ref[...], 1.0)
    out_ref[:, o + 2 * da_w:o + 3 * da_w] = proj[:, o + 2 * da_w:o + 3 * da_w].astype(BF16)


def _stack_masked(q, n_parts):
    lane = lax.broadcasted_iota(jnp.int32, q.shape, 1)
    zero = jnp.zeros_like(q)
    return jnp.concatenate(
        [jnp.where((lane >= p * HEAD_DIM) & (lane < (p + 1) * HEAD_DIM), q, zero)
         for p in range(n_parts)], axis=0)


def _sb_kernel(q_ref, k_ref, v_ref, o_ref, qs_ref, acc_ref, run_ref, *, tq, n_grp):
    i = pl.program_id(2)
    m2 = 2 * tq
    r2 = lax.broadcasted_iota(jnp.int32, (tq, tq), 0)
    c2 = lax.broadcasted_iota(jnp.int32, (tq, tq), 1)
    tri = jnp.where(r2 > c2, 1.0, 0.0).astype(BF16)
    lanes = lambda g: slice(g * LANES, (g + 1) * LANES)

    for g in range(n_grp):
        qs_ref[g] = _stack_masked(q_ref[:, lanes(g)], 2)

    def block(g, blk, run, diagonal=False):
        ks = pl.multiple_of(blk * tq, tq)
        z = _dot_nt(qs_ref[g], k_ref[pl.ds(ks, tq), lanes(g)])
        p = jnp.maximum(z, 0.0) + jnp.log2(1.0 + jnp.exp2(-jnp.abs(z)))
        if diagonal:
            row = lax.broadcasted_iota(jnp.int32, (m2, tq), 0)
            col = lax.broadcasted_iota(jnp.int32, (m2, tq), 1)
            mask = col < jnp.where(row >= tq, row - tq, row)
            p = jnp.where(mask, p, 0.0)
        after = _dot(p.astype(BF16), tri)
        spent = after + jnp.concatenate([run] * (tq // LANES), axis=1)
        run = run + jnp.broadcast_to(after[:, 0:1] + p[:, 0:1], (m2, LANES))
        a = jnp.exp2((z - p) - spent)
        if diagonal:
            a = jnp.where(mask, a, 0.0)
        return run, _dot(a.astype(BF16), v_ref[pl.ds(ks, tq), lanes(g)])

    zeros = jnp.zeros((m2, LANES), F32)

    @pl.when(i == 0)
    def _():
        for g in range(n_grp):
            run_ref[g], acc_ref[g] = block(g, i, zeros, diagonal=True)

    @pl.when(i > 0)
    def _():
        for g in range(n_grp):
            run, out = block(g, i, zeros, diagonal=True)
            run, out2 = block(g, i - 1, run)
            run_ref[g] = run
            acc_ref[g] = out + out2

    def more(carry):
        blk, least = carry
        return jnp.logical_and(blk >= 0, least < SB_STOP_LOG2)

    def step(carry):
        blk, _ = carry
        least = None
        for g in range(n_grp):
            run, out = block(g, blk, run_ref[g])
            run_ref[g] = run
            acc_ref[g] += out
            least = jnp.min(run) if least is None else jnp.minimum(least, jnp.min(run))
        return blk - 1, least

    lax.while_loop(more, step, (i - 2, jnp.min(run_ref[...])))
    lane = lax.broadcasted_iota(jnp.int32, (tq, LANES), 1)
    for g in range(n_grp):
        o_ref[:, lanes(g)] = jnp.where(lane < HEAD_DIM, acc_ref[g, 0:tq, :],
                                       acc_ref[g, tq:m2, :]).astype(o_ref.dtype)


def _da_flat_kernel(q_ref, k_ref, v_ref, lq1_ref, lk1_ref, lq2_ref, lk2_ref, gsub_ref, o_ref,
                    vext_ref, qs_ref, s_ref, mc_ref, m_ref, l_ref, acc_ref, *, tq, n_tiles,
                    lambda_init):
    m2 = 2 * tq
    reps = tq // LANES
    vext_ref[:, 0:LANES] = v_ref[...]
    vext_ref[:, LANES:2 * LANES] = jnp.ones(v_ref.shape, v_ref.dtype)
    lam = (jnp.exp(jnp.sum(lq1_ref[...] * lk1_ref[...], axis=-1, keepdims=True))
           - jnp.exp(jnp.sum(lq2_ref[...] * lk2_ref[...], axis=-1, keepdims=True))
           + lambda_init)
    steps = [(i, blk) for i in range(n_tiles) for blk in range(i + 1)]
    rows = lambda j: slice(j * tq, (j + 1) * tq)

    def scores(t):
        i, blk = steps[t]
        if blk == 0:
            qs_ref[i % 2] = _stack_masked(q_ref[rows(i), :], 2)
        s = _dot_nt(qs_ref[i % 2], k_ref[rows(blk), :])
        s_ref[t % 2] = s
        mc_ref[t % 2] = jnp.broadcast_to(jnp.max(s, axis=1, keepdims=True), (m2, LANES))

    def absorb(t):
        i, blk = steps[t]
        s = s_ref[t % 2]
        if blk == i:
            row = lax.broadcasted_iota(jnp.int32, (m2, tq), 0)
            col = lax.broadcasted_iota(jnp.int32, (m2, tq), 1)
            qrow = jnp.where(row >= tq, row - tq, row)
            s = jnp.where((col // CHUNK) <= (qrow // CHUNK), s, NEG_INF)
            m_cur = jnp.broadcast_to(jnp.max(s, axis=1, keepdims=True), (m2, LANES))
        else:
            m_cur = mc_ref[t % 2]
        m_new = m_cur if blk == 0 else jnp.maximum(m_ref[...], m_cur)
        p = jnp.exp2(s - jnp.concatenate([m_new] * reps, axis=1))
        pv = _dot(p.astype(BF16), vext_ref[rows(blk), :])
        if blk == 0:
            acc, l = pv[:, :LANES], pv[:, LANES:]
        else:
            alpha = jnp.exp2(m_ref[...] - m_new)
            acc = alpha * acc_ref[...] + pv[:, :LANES]
            l = alpha * l_ref[...] + pv[:, LANES:]
        if blk < i:
            acc_ref[...], l_ref[...], m_ref[...] = acc, l, m_new
        else:
            res = acc / l
            o = res[0:tq, :] - lam * res[tq:m2, :]
            ms = jnp.mean(o * o, axis=-1, keepdims=True)
            o = o * lax.rsqrt(ms + EPS) * gsub_ref[...] * (1.0 - lambda_init)
            o_ref[rows(i), :] = o.astype(o_ref.dtype)

    scores(0)
    for t in range(len(steps)):
        if t + 1 < len(steps):
            scores(t + 1)
        absorb(t)


def _post_kernel(x_ref, oa_ref, ob_ref, gmix_ref, wgate_ref, bgate_ref, wa_ref, wb_ref, wout_ref,
                 gffn_ref, wfg_ref, wfu_ref, wfd_ref, out_ref, *, d_model):
    x = x_ref[...]
    ms = jnp.mean(x * x, axis=-1, keepdims=True)
    h = (x * lax.rsqrt(ms + EPS) * gmix_ref[...]).astype(BF16)
    gates = jax.nn.sigmoid(_dot(h, wgate_ref[...]) + bgate_ref[...])
    br_a = _dot(oa_ref[...], wa_ref[...])
    br_b = _dot(ob_ref[...], wb_ref[...])
    merged = gates[:, :d_model] * br_a + gates[:, d_model:] * br_b
    x1 = x + _dot(merged.astype(BF16), wout_ref[...])

    ms2 = jnp.mean(x1 * x1, axis=-1, keepdims=True)
    h2 = (x1 * lax.rsqrt(ms2 + EPS) * gffn_ref[...]).astype(BF16)
    fg = _dot(h2, wfg_ref[...])
    fu = _dot(h2, wfu_ref[...])
    ff = (fg * jax.nn.sigmoid(fg)) * fu
    out_ref[...] = x1 + _dot(ff.astype(BF16), wfd_ref[...])


def _rope_tables(seq):
    half = ROPE_DIM // 2
    pos = np.arange(seq, dtype=np.float64)
    inv_freq = ROPE_THETA ** (-np.arange(0, ROPE_DIM, 2, dtype=np.float64) / ROPE_DIM)
    ang = pos[:, None] * inv_freq[None, :]
    cos, sin = np.cos(ang), np.sin(ang)
    ones = np.ones((seq, HEAD_DIM - ROPE_DIM))
    zeros = np.zeros((seq, HEAD_DIM - ROPE_DIM))
    zh = np.zeros((seq, half))
    ra = np.concatenate([cos, cos, ones], axis=1)
    rm = np.concatenate([-sin, zh, zeros], axis=1)
    rp = np.concatenate([zh, sin, zeros], axis=1)
    tile = lambda t: jnp.asarray(np.tile(t, (1, LANES // HEAD_DIM)), dtype=F32)
    return tile(ra), tile(rm), tile(rp)


def _tiles(seq):
    tm, tp, tq_sb, tq_da = 512, 256, 256, 512
    for t in (tm, tp, tq_sb, tq_da):
        assert seq % t == 0 and t % LANES == 0 and t % CHUNK == 0
    return tm, tp, tq_sb, tq_da


def kernel(x, g_mix, w_in, g_q, g_k, lam_q1, lam_k1, lam_q2, lam_k2, g_sub, w_branch_a, w_branch_b,
           w_gate, b_gate, w_out, g_ffn, w_ffn_gate, w_ffn_up, w_ffn_down):
    b, s, d = x.shape
    depth = g_mix.shape[0]
    sb_w = SB_HEADS * HEAD_DIM
    da_w = DA_HEADS * 2 * HEAD_DIM
    in_w = 3 * sb_w + 3 * da_w
    n = b * s
    d_ff = w_ffn_gate.shape[-1]

    tm, tp, tq_sb, tq_da = _tiles(s)

    ra, rm, rp = _rope_tables(s)
    grp = np.arange(da_w) // HEAD_DIM
    gmat = jnp.asarray(np.where(grp[:, None] == grp[None, :], 1.0 / HEAD_DIM, 0.0), dtype=BF16)
    tile_g = lambda g: jnp.tile(g, da_w // HEAD_DIM)[None, :]
    vmem_full = pl.BlockSpec(memory_space=pltpu.VMEM)
    spt = s // tm

    xf = x.reshape(n, d)
    for layer in range(depth):
        lambda_init = 0.8 - 0.6 * math.exp(-0.3 * layer)
        gmix = g_mix[layer][None, :]

        proj = pl.pallas_call(
            functools.partial(_proj_kernel, sb_w=sb_w, da_w=da_w),
            grid=(n // tm,),
            in_specs=[
                pl.BlockSpec((tm, d), lambda t: (t, 0)),
                vmem_full, vmem_full, vmem_full, vmem_full, vmem_full,
                pl.BlockSpec((tm, LANES), lambda t: (t % spt, 0)),
                pl.BlockSpec((tm, LANES), lambda t: (t % spt, 0)),
                pl.BlockSpec((tm, LANES), lambda t: (t % spt, 0)),
            ],
            out_specs=pl.BlockSpec((tm, in_w), lambda t: (t, 0)),
            out_shape=jax.ShapeDtypeStruct((n, in_w), BF16),
            compiler_params=pltpu.CompilerParams(
                dimension_semantics=("arbitrary",), vmem_limit_bytes=VMEM_LIMIT),
            name="proj",
        )(xf, gmix, w_in[layer].astype(BF16), tile_g(g_q[layer]), tile_g(g_k[layer]), gmat,
          ra, rm, rp)

        cb = lambda off: off // LANES
        tq, nq, gw = tq_sb, s // tq_sb, SB_GROUPS * LANES
        o_a = pl.pallas_call(
            functools.partial(_sb_kernel, tq=tq, n_grp=SB_GROUPS),
            grid=(b, sb_w // gw, nq),
            in_specs=[
                pl.BlockSpec((tq, gw), lambda bi, j, i, nq=nq: (bi * nq + i, j)),
                pl.BlockSpec((s, gw), lambda bi, j, i: (bi, sb_w // gw + j)),
                pl.BlockSpec((s, gw), lambda bi, j, i: (bi, 2 * sb_w // gw + j)),
            ],
            out_specs=pl.BlockSpec((tq, gw), lambda bi, j, i, nq=nq: (bi * nq + i, j)),
            out_shape=jax.ShapeDtypeStruct((n, sb_w), BF16),
            scratch_shapes=[pltpu.VMEM((SB_GROUPS, 2 * tq, LANES), BF16),
                            pltpu.VMEM((SB_GROUPS, 2 * tq, LANES), F32),
                            pltpu.VMEM((SB_GROUPS, 2 * tq, LANES), F32)],
            compiler_params=pltpu.CompilerParams(
                dimension_semantics=("arbitrary", "arbitrary", "arbitrary"),
                vmem_limit_bytes=VMEM_LIMIT),
            name="sb_attn",
        )(proj, proj, proj)

        tq = tq_da
        o_b = pl.pallas_call(
            functools.partial(_da_flat_kernel, tq=tq, n_tiles=s // tq, lambda_init=lambda_init),
            grid=(b, DA_HEADS),
            in_specs=[
                pl.BlockSpec((s, LANES), lambda bi, j: (bi, cb(3 * sb_w) + j)),
                pl.BlockSpec((s, LANES), lambda bi, j: (bi, cb(3 * sb_w + da_w) + j)),
                pl.BlockSpec((s, LANES), lambda bi, j: (bi, cb(3 * sb_w + 2 * da_w) + j)),
                vmem_full, vmem_full, vmem_full, vmem_full, vmem_full,
            ],
            out_specs=pl.BlockSpec((s, LANES), lambda bi, j: (bi, j)),
            out_shape=jax.ShapeDtypeStruct((n, da_w), BF16),
            scratch_shapes=[pltpu.VMEM((s, 2 * LANES), BF16),
                            pltpu.VMEM((2, 2 * tq, LANES), BF16),
                            pltpu.VMEM((2, 2 * tq, tq), F32),
                            pltpu.VMEM((2, 2 * tq, LANES), F32)]
            + [pltpu.VMEM((2 * tq, LANES), F32)] * 3,
            compiler_params=pltpu.CompilerParams(
                dimension_semantics=("arbitrary", "arbitrary"),
                vmem_limit_bytes=VMEM_LIMIT),
            name="da_attn",
        )(proj, proj, proj, lam_q1[layer][None, :], lam_k1[layer][None, :],
          lam_q2[layer][None, :], lam_k2[layer][None, :], g_sub[layer][None, :])

        xf = pl.pallas_call(
            functools.partial(_post_kernel, d_model=d),
            grid=(n // tp,),
            in_specs=[
                pl.BlockSpec((tp, d), lambda t: (t, 0)),
                pl.BlockSpec((tp, sb_w), lambda t: (t, 0)),
                pl.BlockSpec((tp, da_w), lambda t: (t, 0)),
            ] + [vmem_full] * 10,
            out_specs=pl.BlockSpec((tp, d), lambda t: (t, 0)),
            out_shape=jax.ShapeDtypeStruct((n, d), F32),
            compiler_params=pltpu.CompilerParams(
                dimension_semantics=("arbitrary",), vmem_limit_bytes=VMEM_LIMIT),
            name="post",
        )(xf, o_a, o_b, gmix, w_gate[layer].astype(BF16), b_gate[layer][None, :],
          w_branch_a[layer].astype(BF16), w_branch_b[layer].astype(BF16),
          w_out[layer].astype(BF16), g_ffn[layer][None, :],
          w_ffn_gate[layer].astype(BF16), w_ffn_up[layer].astype(BF16),
          w_ffn_down[layer].astype(BF16))
    return xf.reshape(b, s, d)
```

```python
import functools
import math

import jax
import jax.numpy as jnp
import numpy as np
from jax import lax
from jax.experimental import pallas as pl
from jax.experimental.pallas import tpu as pltpu

F32 = jnp.float32
BF16 = jnp.bfloat16

CHUNK = 64
SB_HEADS = 8
DA_HEADS = 4
HEAD_DIM = 64
ROPE_THETA = 500000.0
ROPE_DIM = HEAD_DIM // 4
EPS = 1e-6
NEG_INF = -1e30
LOG2E = math.log2(math.e)
SB_STOP_LOG2 = 150.0
LANES = 128
VMEM_LIMIT = 56 * 1024 * 1024


def _dot(a, b):
    return jnp.dot(a, b, preferred_element_type=F32)


def _dot_nt(a, b):
    return lax.dot_general(a, b, (((1,), (1,)), ((), ())), preferred_element_type=F32)


def _proj_kernel(x_ref, gmix_ref, win_ref, gq_ref, gk_ref, gmat_ref,
                 ra_ref, rm_ref, rp_ref, out_ref, *, sb_w, da_w):
    x = x_ref[...]
    ms = jnp.mean(x * x, axis=-1, keepdims=True)
    h = x * lax.rsqrt(ms + EPS) * gmix_ref[...]
    proj = _dot(h.astype(BF16), win_ref[...])
    scale = HEAD_DIM ** -0.5

    out_ref[:, 0:sb_w] = (proj[:, 0:sb_w] * (scale * LOG2E)).astype(BF16)
    out_ref[:, sb_w:3 * sb_w] = proj[:, sb_w:3 * sb_w].astype(BF16)

    ra, rm, rp = ra_ref[...], rm_ref[...], rp_ref[...]

    def qk_norm_rope(t, g, mult):
        msq = _dot((t * t).astype(BF16), gmat_ref[...])
        tn = t * lax.rsqrt(msq + EPS) * g
        cols = []
        for j in range(da_w // LANES):
            c = tn[:, j * LANES:(j + 1) * LANES]
            r = (c * ra + pltpu.roll(c, LANES - ROPE_DIM // 2, 1) * rm
                 + pltpu.roll(c, ROPE_DIM // 2, 1) * rp)
            cols.append((r * mult).astype(BF16))
        return jnp.concatenate(cols, axis=1)

    o = 3 * sb_w
    out_ref[:, o:o + da_w] = qk_norm_rope(proj[:, o:o + da_w], gq_ref[...], scale * LOG2E)
    out_ref[:, o + da_w:o + 2 * da_w] = qk_norm_rope(proj[:, o + da_w:o + 2 * da_w], gk_ref[...], 1.0)
    out_ref[:, o + 2 * da_w:o + 3 * da_w] = proj[:, o + 2 * da_w:o + 3 * da_w].astype(BF16)


def _stack_masked(q, n_parts):
    lane = lax.broadcasted_iota(jnp.int32, q.shape, 1)
    zero = jnp.zeros_like(q)
    return jnp.concatenate(
        [jnp.where((lane >= p * HEAD_DIM) & (lane < (p + 1) * HEAD_DIM), q, zero)
         for p in range(n_parts)], axis=0)


def _attn_kernel(sbq_ref, sbk_ref, sbv_ref, daq_ref, dak_ref, dav_ref,
                 lq1_ref, lk1_ref, lq2_ref, lk2_ref, gsub_ref, oa_ref, ob_ref,
                 sqs_ref, sacc_ref, srun_ref,
                 vext_ref, dqs_ref, s_ref, mc_ref, m_ref, l_ref, acc_ref,
                 *, tq_sb, tq_da, n_tiles_da, sb_tiles, lambda_init):
    j = pl.program_id(1)
    lanes = lambda g: slice(g * LANES, (g + 1) * LANES)

    tq = tq_sb
    m2 = 2 * tq
    n_grp = sbq_ref.shape[1] // LANES
    r2 = lax.broadcasted_iota(jnp.int32, (tq, tq), 0)
    c2 = lax.broadcasted_iota(jnp.int32, (tq, tq), 1)
    tri = jnp.where(r2 > c2, 1.0, 0.0).astype(BF16)

    def sb_block(seg, g, blk, run, diagonal=False, gate=None):
        ks = pl.multiple_of(blk * tq, tq)
        z = _dot_nt(sqs_ref[seg, g], sbk_ref[pl.ds(ks, tq), lanes(g)])
        p = jnp.maximum(z, 0.0) + jnp.log2(1.0 + jnp.exp2(-jnp.abs(z)))
        if diagonal:
            row = lax.broadcasted_iota(jnp.int32, (m2, tq), 0)
            col = lax.broadcasted_iota(jnp.int32, (m2, tq), 1)
            mask = col < jnp.where(row >= tq, row - tq, row)
            p = jnp.where(mask, p, 0.0)
        if gate is not None:
            p = p * gate
        after = _dot(p.astype(BF16), tri)
        spent = after + jnp.concatenate([run] * (tq // LANES), axis=1)
        run = run + jnp.broadcast_to(after[:, 0:1] + p[:, 0:1], (m2, LANES))
        a = jnp.exp2((z - p) - spent)
        if diagonal:
            a = jnp.where(mask, a, 0.0)
        if gate is not None:
            a = a * gate
        return run, _dot(a.astype(BF16), sbv_ref[pl.ds(ks, tq), lanes(g)])

    def sb_first(seg):
        i = j * sb_tiles + seg
        zeros = jnp.zeros((m2, LANES), F32)
        for g in range(n_grp):
            sqs_ref[seg, g] = _stack_masked(sbq_ref[seg * tq:(seg + 1) * tq, lanes(g)], 2)
            run, out = sb_block(seg, g, i, zeros, diagonal=True)
            if seg == 0:
                run, out2 = sb_block(seg, g, jnp.maximum(i - 1, 0), run,
                                     gate=jnp.where(i > 0, 1.0, 0.0).astype(F32))
            else:
                run, out2 = sb_block(seg, g, i - 1, run)
            srun_ref[seg, g] = run
            sacc_ref[seg, g] = out + out2

    def sb_finish(seg):
        i = j * sb_tiles + seg

        def more(carry):
            blk, least = carry
            return jnp.logical_and(blk >= 0, least < SB_STOP_LOG2)

        def step(carry):
            blk, _ = carry
            least = None
            for g in range(n_grp):
                run, out = sb_block(seg, g, blk, srun_ref[seg, g])
                srun_ref[seg, g] = run
                sacc_ref[seg, g] += out
                least = jnp.min(run) if least is None else jnp.minimum(least, jnp.min(run))
            return blk - 1, least

        lax.while_loop(more, step, (i - 2, jnp.min(srun_ref[seg])))
        lane = lax.broadcasted_iota(jnp.int32, (tq, LANES), 1)
        for g in range(n_grp):
            oa_ref[seg * tq:(seg + 1) * tq, lanes(g)] = jnp.where(
                lane < HEAD_DIM, sacc_ref[seg, g, 0:tq, :], sacc_ref[seg, g, tq:m2, :]
            ).astype(oa_ref.dtype)

    td = tq_da
    d2 = 2 * td
    reps = td // LANES
    vext_ref[:, 0:LANES] = dav_ref[...]
    vext_ref[:, LANES:2 * LANES] = jnp.ones(dav_ref.shape, dav_ref.dtype)
    lam = (jnp.exp(jnp.sum(lq1_ref[...] * lk1_ref[...], axis=-1, keepdims=True))
           - jnp.exp(jnp.sum(lq2_ref[...] * lk2_ref[...], axis=-1, keepdims=True))
           + lambda_init)
    steps = [(i, blk) for i in range(n_tiles_da) for blk in range(i + 1)]
    rows = lambda t: slice(t * td, (t + 1) * td)

    def scores(t):
        i, blk = steps[t]
        if blk == 0:
            dqs_ref[i % 2] = _stack_masked(daq_ref[rows(i), :], 2)
        s = _dot_nt(dqs_ref[i % 2], dak_ref[rows(blk), :])
        s_ref[t % 2] = s
        mc_ref[t % 2] = jnp.broadcast_to(jnp.max(s, axis=1, keepdims=True), (d2, LANES))

    def absorb(t):
        i, blk = steps[t]
        s = s_ref[t % 2]
        if blk == i:
            row = lax.broadcasted_iota(jnp.int32, (d2, td), 0)
            col = lax.broadcasted_iota(jnp.int32, (d2, td), 1)
            qrow = jnp.where(row >= td, row - td, row)
            s = jnp.where((col // CHUNK) <= (qrow // CHUNK), s, NEG_INF)
            m_cur = jnp.broadcast_to(jnp.max(s, axis=1, keepdims=True), (d2, LANES))
        else:
            m_cur = mc_ref[t % 2]
        m_new = m_cur if blk == 0 else jnp.maximum(m_ref[...], m_cur)
        p = jnp.exp2(s - jnp.concatenate([m_new] * reps, axis=1))
        pv = _dot(p.astype(BF16), vext_ref[rows(blk), :])
        if blk == 0:
            acc, l = pv[:, :LANES], pv[:, LANES:]
        else:
            alpha = jnp.exp2(m_ref[...] - m_new)
            acc = alpha * acc_ref[...] + pv[:, :LANES]
            l = alpha * l_ref[...] + pv[:, LANES:]
        if blk < i:
            acc_ref[...], l_ref[...], m_ref[...] = acc, l, m_new
        else:
            res = acc / l
            o = res[0:td, :] - lam * res[td:d2, :]
            ms = jnp.mean(o * o, axis=-1, keepdims=True)
            o = o * lax.rsqrt(ms + EPS) * gsub_ref[...] * (1.0 - lambda_init)
            ob_ref[rows(i), :] = o.astype(ob_ref.dtype)

    every = -(-len(steps) // sb_tiles)
    scores(0)
    for t in range(len(steps)):
        if t % every == 0:
            sb_first(t // every)
        if t + 1 < len(steps):
            scores(t + 1)
        absorb(t)
    for seg in range(sb_tiles):
        sb_finish(seg)


def _post_kernel(x_ref, oa_ref, ob_ref, gmix_ref, wgate_ref, bgate_ref, wa_ref, wb_ref, wout_ref,
                 gffn_ref, wfg_ref, wfu_ref, wfd_ref, out_ref, *, d_model):
    x = x_ref[...]
    ms = jnp.mean(x * x, axis=-1, keepdims=True)
    h = (x * lax.rsqrt(ms + EPS) * gmix_ref[...]).astype(BF16)
    gates = jax.nn.sigmoid(_dot(h, wgate_ref[...]) + bgate_ref[...])
    br_a = _dot(oa_ref[...], wa_ref[...])
    br_b = _dot(ob_ref[...], wb_ref[...])
    merged = gates[:, :d_model] * br_a + gates[:, d_model:] * br_b
    x1 = x + _dot(merged.astype(BF16), wout_ref[...])

    ms2 = jnp.mean(x1 * x1, axis=-1, keepdims=True)
    h2 = (x1 * lax.rsqrt(ms2 + EPS) * gffn_ref[...]).astype(BF16)
    fg = _dot(h2, wfg_ref[...])
    fu = _dot(h2, wfu_ref[...])
    ff = (fg * jax.nn.sigmoid(fg)) * fu
    out_ref[...] = x1 + _dot(ff.astype(BF16), wfd_ref[...])


def _rope_tables(seq):
    half = ROPE_DIM // 2
    pos = np.arange(seq, dtype=np.float64)
    inv_freq = ROPE_THETA ** (-np.arange(0, ROPE_DIM, 2, dtype=np.float64) / ROPE_DIM)
    ang = pos[:, None] * inv_freq[None, :]
    cos, sin = np.cos(ang), np.sin(ang)
    ones = np.ones((seq, HEAD_DIM - ROPE_DIM))
    zeros = np.zeros((seq, HEAD_DIM - ROPE_DIM))
    zh = np.zeros((seq, half))
    ra = np.concatenate([cos, cos, ones], axis=1)
    rm = np.concatenate([-sin, zh, zeros], axis=1)
    rp = np.concatenate([zh, sin, zeros], axis=1)
    tile = lambda t: jnp.asarray(np.tile(t, (1, LANES // HEAD_DIM)), dtype=F32)
    return tile(ra), tile(rm), tile(rp)


def _tiles(seq):
    tm, tp, tq_sb, tq_da = 512, 256, 256, 512
    for t in (tm, tp, tq_sb, tq_da):
        assert seq % t == 0 and t % LANES == 0 and t % CHUNK == 0
    assert seq % (DA_HEADS * tq_sb) == 0
    return tm, tp, tq_sb, tq_da


def kernel(x, g_mix, w_in, g_q, g_k, lam_q1, lam_k1, lam_q2, lam_k2, g_sub, w_branch_a, w_branch_b,
           w_gate, b_gate, w_out, g_ffn, w_ffn_gate, w_ffn_up, w_ffn_down):
    b, s, d = x.shape
    depth = g_mix.shape[0]
    sb_w = SB_HEADS * HEAD_DIM
    da_w = DA_HEADS * 2 * HEAD_DIM
    in_w = 3 * sb_w + 3 * da_w
    n = b * s

    tm, tp, tq_sb, tq_da = _tiles(s)

    ra, rm, rp = _rope_tables(s)
    grp = np.arange(da_w) // HEAD_DIM
    gmat = jnp.asarray(np.where(grp[:, None] == grp[None, :], 1.0 / HEAD_DIM, 0.0), dtype=BF16)
    tile_g = lambda g: jnp.tile(g, da_w // HEAD_DIM)[None, :]
    vmem_full = pl.BlockSpec(memory_space=pltpu.VMEM)
    spt = s // tm

    xf = x.reshape(n, d)
    for layer in range(depth):
        lambda_init = 0.8 - 0.6 * math.exp(-0.3 * layer)
        gmix = g_mix[layer][None, :]

        proj = pl.pallas_call(
            functools.partial(_proj_kernel, sb_w=sb_w, da_w=da_w),
            grid=(n // tm,),
            in_specs=[
                pl.BlockSpec((tm, d), lambda t: (t, 0)),
                vmem_full, vmem_full, vmem_full, vmem_full, vmem_full,
                pl.BlockSpec((tm, LANES), lambda t: (t % spt, 0)),
                pl.BlockSpec((tm, LANES), lambda t: (t % spt, 0)),
                pl.BlockSpec((tm, LANES), lambda t: (t % spt, 0)),
            ],
            out_specs=pl.BlockSpec((tm, in_w), lambda t: (t, 0)),
            out_shape=jax.ShapeDtypeStruct((n, in_w), BF16),
            compiler_params=pltpu.CompilerParams(
                dimension_semantics=("arbitrary",), vmem_limit_bytes=VMEM_LIMIT),
            name="proj",
        )(xf, gmix, w_in[layer].astype(BF16), tile_g(g_q[layer]), tile_g(g_k[layer]), gmat,
          ra, rm, rp)

        sb_tiles = s // (DA_HEADS * tq_sb)
        sb_rows = sb_tiles * tq_sb
        cb = lambda off, width: off // width
        o_a, o_b = pl.pallas_call(
            functools.partial(_attn_kernel, tq_sb=tq_sb, tq_da=tq_da, n_tiles_da=s // tq_da,
                              sb_tiles=sb_tiles, lambda_init=lambda_init),
            grid=(b, DA_HEADS),
            in_specs=[
                pl.BlockSpec((sb_rows, sb_w), lambda bi, j: (bi * DA_HEADS + j, 0)),
                pl.BlockSpec((s, sb_w), lambda bi, j: (bi, cb(sb_w, sb_w)),
                             pipeline_mode=pl.Buffered(1)),
                pl.BlockSpec((s, sb_w), lambda bi, j: (bi, cb(2 * sb_w, sb_w)),
                             pipeline_mode=pl.Buffered(1)),
                pl.BlockSpec((s, LANES), lambda bi, j: (bi, cb(3 * sb_w, LANES) + j)),
                pl.BlockSpec((s, LANES), lambda bi, j: (bi, cb(3 * sb_w + da_w, LANES) + j)),
                pl.BlockSpec((s, LANES), lambda bi, j: (bi, cb(3 * sb_w + 2 * da_w, LANES) + j)),
                vmem_full, vmem_full, vmem_full, vmem_full, vmem_full,
            ],
            out_specs=[
                pl.BlockSpec((sb_rows, sb_w), lambda bi, j: (bi * DA_HEADS + j, 0)),
                pl.BlockSpec((s, LANES), lambda bi, j: (bi, j)),
            ],
            out_shape=[jax.ShapeDtypeStruct((n, sb_w), BF16),
                       jax.ShapeDtypeStruct((n, da_w), BF16)],
            scratch_shapes=[pltpu.VMEM((sb_tiles, sb_w // LANES, 2 * tq_sb, LANES), BF16),
                            pltpu.VMEM((sb_tiles, sb_w // LANES, 2 * tq_sb, LANES), F32),
                            pltpu.VMEM((sb_tiles, sb_w // LANES, 2 * tq_sb, LANES), F32),
                            pltpu.VMEM((s, 2 * LANES), BF16),
                            pltpu.VMEM((2, 2 * tq_da, LANES), BF16),
                            pltpu.VMEM((2, 2 * tq_da, tq_da), F32),
                            pltpu.VMEM((2, 2 * tq_da, LANES), F32)]
            + [pltpu.VMEM((2 * tq_da, LANES), F32)] * 3,
            compiler_params=pltpu.CompilerParams(
                dimension_semantics=("arbitrary", "arbitrary"),
                vmem_limit_bytes=VMEM_LIMIT),
            name="attn",
        )(proj, proj, proj, proj, proj, proj, lam_q1[layer][None, :], lam_k1[layer][None, :],
          lam_q2[layer][None, :], lam_k2[layer][None, :], g_sub[layer][None, :])

        xf = pl.pallas_call(
            functools.partial(_post_kernel, d_model=d),
            grid=(n // tp,),
            in_specs=[
                pl.BlockSpec((tp, d), lambda t: (t, 0)),
                pl.BlockSpec((tp, sb_w), lambda t: (t, 0)),
                pl.BlockSpec((tp, da_w), lambda t: (t, 0)),
            ] + [vmem_full] * 10,
            out_specs=pl.BlockSpec((tp, d), lambda t: (t, 0)),
            out_shape=jax.ShapeDtypeStruct((n, d), F32),
            compiler_params=pltpu.CompilerParams(
                dimension_semantics=("arbitrary",), vmem_limit_bytes=VMEM_LIMIT),
            name="post",
        )(xf, o_a, o_b, gmix, w_gate[layer].astype(BF16), b_gate[layer][None, :],
          w_branch_a[layer].astype(BF16), w_branch_b[layer].astype(BF16),
          w_out[layer].astype(BF16), g_ffn[layer][None, :],
          w_ffn_gate[layer].astype(BF16), w_ffn_up[layer].astype(BF16),
          w_ffn_down[layer].astype(BF16))
    return xf.reshape(b, s, d)
```

```python
import functools
import math

import jax
import jax.numpy as jnp
import numpy as np
from jax import lax
from jax.experimental import pallas as pl
from jax.experimental.pallas import tpu as pltpu

F32 = jnp.float32
BF16 = jnp.bfloat16

CHUNK = 64
SB_HEADS = 8
DA_HEADS = 4
HEAD_DIM = 64
ROPE_THETA = 500000.0
ROPE_DIM = HEAD_DIM // 4
EPS = 1e-6
NEG_INF = -1e30
LOG2E = math.log2(math.e)
SB_GROUPS = 4
SB_STOP_LOG2 = 150.0
SB_EXP2_MAX = 126.0
LANES = 128
VMEM_LIMIT = 56 * 1024 * 1024


def _dot(a, b):
    return jnp.dot(a, b, preferred_element_type=F32)


def _dot_nt(a, b):
    return lax.dot_general(a, b, (((1,), (1,)), ((), ())), preferred_element_type=F32)


def _proj_kernel(x_ref, gmix_ref, win_ref, gq_ref, gk_ref, gmat_ref,
                 ra_ref, rm_ref, rp_ref, out_ref, *, sb_w, da_w):
    x = x_ref[...]
    ms = jnp.mean(x * x, axis=-1, keepdims=True)
    h = x * lax.rsqrt(ms + EPS) * gmix_ref[...]
    proj = _dot(h.astype(BF16), win_ref[...])
    scale = HEAD_DIM ** -0.5

    out_ref[:, 0:sb_w] = (proj[:, 0:sb_w] * (scale * LOG2E)).astype(BF16)
    out_ref[:, sb_w:3 * sb_w] = proj[:, sb_w:3 * sb_w].astype(BF16)

    ra, rm, rp = ra_ref[...], rm_ref[...], rp_ref[...]

    def qk_norm_rope(t, g, mult):
        msq = _dot((t * t).astype(BF16), gmat_ref[...])
        tn = t * lax.rsqrt(msq + EPS) * g
        cols = []
        for j in range(da_w // LANES):
            c = tn[:, j * LANES:(j + 1) * LANES]
            r = (c * ra + pltpu.roll(c, LANES - ROPE_DIM // 2, 1) * rm
                 + pltpu.roll(c, ROPE_DIM // 2, 1) * rp)
            cols.append((r * mult).astype(BF16))
        return jnp.concatenate(cols, axis=1)

    o = 3 * sb_w
    out_ref[:, o:o + da_w] = qk_norm_rope(proj[:, o:o + da_w], gq_ref[...], scale * LOG2E)
    out_ref[:, o + da_w:o + 2 * da_w] = qk_norm_rope(proj[:, o + da_w:o + 2 * da_w], gk_ref[...], 1.0)
    out_ref[:, o + 2 * da_w:o + 3 * da_w] = proj[:, o + 2 * da_w:o + 3 * da_w].astype(BF16)


def _stack_masked(q, n_parts):
    lane = lax.broadcasted_iota(jnp.int32, q.shape, 1)
    zero = jnp.zeros_like(q)
    return jnp.concatenate(
        [jnp.where((lane >= p * HEAD_DIM) & (lane < (p + 1) * HEAD_DIM), q, zero)
         for p in range(n_parts)], axis=0)


def _sb_kernel(q_ref, k_ref, v_ref, o_ref, qs_ref, acc_ref, run_ref, *, tq, n_grp):
    i = pl.program_id(2)
    m2 = 2 * tq
    r2 = lax.broadcasted_iota(jnp.int32, (tq, tq), 0)
    c2 = lax.broadcasted_iota(jnp.int32, (tq, tq), 1)
    tri = jnp.where(r2 > c2, 1.0, 0.0).astype(BF16)
    lanes = lambda g: slice(g * LANES, (g + 1) * LANES)

    for g in range(n_grp):
        qs_ref[g] = _stack_masked(q_ref[:, lanes(g)], 2)

    def block(g, blk, run, diagonal=False):
        ks = pl.multiple_of(blk * tq, tq)
        z = _dot_nt(qs_ref[g], k_ref[pl.ds(ks, tq), lanes(g)])
        if diagonal:
            row = lax.broadcasted_iota(jnp.int32, (m2, tq), 0)
            col = lax.broadcasted_iota(jnp.int32, (m2, tq), 1)
            z = jnp.where(col < jnp.where(row >= tq, row - tq, row), z, NEG_INF)
        p = jnp.maximum(z, jnp.log2(1.0 + jnp.exp2(jnp.minimum(z, SB_EXP2_MAX))))
        after = _dot(p.astype(BF16), tri)
        spent = after + jnp.concatenate([run] * (tq // LANES), axis=1)
        run = run + jnp.broadcast_to(after[:, 0:1] + p[:, 0:1], (m2, LANES))
        a = jnp.exp2((z - p) - spent)
        return run, _dot(a.astype(BF16), v_ref[pl.ds(ks, tq), lanes(g)])

    zeros = jnp.zeros((m2, LANES), F32)

    @pl.when(i == 0)
    def _():
        for g in range(n_grp):
            run_ref[g], acc_ref[g] = block(g, i, zeros, diagonal=True)

    @pl.when(i > 0)
    def _():
        for g in range(n_grp):
            run, out = block(g, i, zeros, diagonal=True)
            run, out2 = block(g, i - 1, run)
            run_ref[g] = run
            acc_ref[g] = out + out2

    def more(carry):
        blk, least = carry
        return jnp.logical_and(blk >= 0, least < SB_STOP_LOG2)

    def step(carry):
        blk, _ = carry
        least = None
        for g in range(n_grp):
            run, out = block(g, blk, run_ref[g])
            run_ref[g] = run
            acc_ref[g] += out
            least = jnp.min(run) if least is None else jnp.minimum(least, jnp.min(run))
        return blk - 1, least

    lax.while_loop(more, step, (i - 2, jnp.min(run_ref[...])))
    lane = lax.broadcasted_iota(jnp.int32, (tq, LANES), 1)
    for g in range(n_grp):
        o_ref[:, lanes(g)] = jnp.where(lane < HEAD_DIM, acc_ref[g, 0:tq, :],
                                       acc_ref[g, tq:m2, :]).astype(o_ref.dtype)


def _da_flat_kernel(q_ref, k_ref, v_ref, lq1_ref, lk1_ref, lq2_ref, lk2_ref, gsub_ref, o_ref,
                    vext_ref, qs_ref, s_ref, mc_ref, m_ref, l_ref, acc_ref, *, tq, n_tiles,
                    lambda_init):
    m2 = 2 * tq
    reps = tq // LANES
    vext_ref[:, 0:LANES] = v_ref[...]
    vext_ref[:, LANES:2 * LANES] = jnp.ones(v_ref.shape, v_ref.dtype)
    lam = (jnp.exp(jnp.sum(lq1_ref[...] * lk1_ref[...], axis=-1, keepdims=True))
           - jnp.exp(jnp.sum(lq2_ref[...] * lk2_ref[...], axis=-1, keepdims=True))
           + lambda_init)
    steps = [(i, blk) for i in range(n_tiles) for blk in range(i + 1)]
    rows = lambda j: slice(j * tq, (j + 1) * tq)

    def scores(t):
        i, blk = steps[t]
        if blk == 0:
            qs_ref[i % 2] = _stack_masked(q_ref[rows(i), :], 2)
        s = _dot_nt(qs_ref[i % 2], k_ref[rows(blk), :])
        s_ref[t % 2] = s
        mc_ref[t % 2] = jnp.broadcast_to(jnp.max(s, axis=1, keepdims=True), (m2, LANES))

    def absorb(t):
        i, blk = steps[t]
        s = s_ref[t % 2]
        if blk == i:
            row = lax.broadcasted_iota(jnp.int32, (m2, tq), 0)
            col = lax.broadcasted_iota(jnp.int32, (m2, tq), 1)
            qrow = jnp.where(row >= tq, row - tq, row)
            s = jnp.where((col // CHUNK) <= (qrow // CHUNK), s, NEG_INF)
            m_cur = jnp.broadcast_to(jnp.max(s, axis=1, keepdims=True), (m2, LANES))
        else:
            m_cur = mc_ref[t % 2]
        m_new = m_cur if blk == 0 else jnp.maximum(m_ref[...], m_cur)
        p = jnp.exp2(s - jnp.concatenate([m_new] * reps, axis=1))
        pv = _dot(p.astype(BF16), vext_ref[rows(blk), :])
        if blk == 0:
            acc, l = pv[:, :LANES], pv[:, LANES:]
        else:
            alpha = jnp.exp2(m_ref[...] - m_new)
            acc = alpha * acc_ref[...] + pv[:, :LANES]
            l = alpha * l_ref[...] + pv[:, LANES:]
        if blk < i:
            acc_ref[...], l_ref[...], m_ref[...] = acc, l, m_new
        else:
            res = acc / l
            o = res[0:tq, :] - lam * res[tq:m2, :]
            ms = jnp.mean(o * o, axis=-1, keepdims=True)
            o = o * lax.rsqrt(ms + EPS) * gsub_ref[...] * (1.0 - lambda_init)
            o_ref[rows(i), :] = o.astype(o_ref.dtype)

    scores(0)
    for t in range(len(steps)):
        if t + 1 < len(steps):
            scores(t + 1)
        absorb(t)


def _post_kernel(x_ref, oa_ref, ob_ref, gmix_ref, wgate_ref, bgate_ref, wa_ref, wb_ref, wout_ref,
                 gffn_ref, wfg_ref, wfu_ref, wfd_ref, out_ref, *, d_model):
    x = x_ref[...]
    ms = jnp.mean(x * x, axis=-1, keepdims=True)
    h = (x * lax.rsqrt(ms + EPS) * gmix_ref[...]).astype(BF16)
    gates = jax.nn.sigmoid(_dot(h, wgate_ref[...]) + bgate_ref[...])
    br_a = _dot(oa_ref[...], wa_ref[...])
    br_b = _dot(ob_ref[...], wb_ref[...])
    merged = gates[:, :d_model] * br_a + gates[:, d_model:] * br_b
    x1 = x + _dot(merged.astype(BF16), wout_ref[...])

    ms2 = jnp.mean(x1 * x1, axis=-1, keepdims=True)
    h2 = (x1 * lax.rsqrt(ms2 + EPS) * gffn_ref[...]).astype(BF16)
    fg = _dot(h2, wfg_ref[...])
    fu = _dot(h2, wfu_ref[...])
    ff = (fg * jax.nn.sigmoid(fg)) * fu
    out_ref[...] = x1 + _dot(ff.astype(BF16), wfd_ref[...])


def _rope_tables(seq):
    half = ROPE_DIM // 2
    pos = np.arange(seq, dtype=np.float64)
    inv_freq = ROPE_THETA ** (-np.arange(0, ROPE_DIM, 2, dtype=np.float64) / ROPE_DIM)
    ang = pos[:, None] * inv_freq[None, :]
    cos, sin = np.cos(ang), np.sin(ang)
    ones = np.ones((seq, HEAD_DIM - ROPE_DIM))
    zeros = np.zeros((seq, HEAD_DIM - ROPE_DIM))
    zh = np.zeros((seq, half))
    ra = np.concatenate([cos, cos, ones], axis=1)
    rm = np.concatenate([-sin, zh, zeros], axis=1)
    rp = np.concatenate([zh, sin, zeros], axis=1)
    tile = lambda t: jnp.asarray(np.tile(t, (1, LANES // HEAD_DIM)), dtype=F32)
    return tile(ra), tile(rm), tile(rp)


def _tiles(seq):
    tm, tp, tq_sb, tq_da = 512, 512, 256, 512
    for t in (tm, tp, tq_sb, tq_da):
        assert seq % t == 0 and t % LANES == 0 and t % CHUNK == 0
    return tm, tp, tq_sb, tq_da


def kernel(x, g_mix, w_in, g_q, g_k, lam_q1, lam_k1, lam_q2, lam_k2, g_sub, w_branch_a, w_branch_b,
           w_gate, b_gate, w_out, g_ffn, w_ffn_gate, w_ffn_up, w_ffn_down):
    b, s, d = x.shape
    depth = g_mix.shape[0]
    sb_w = SB_HEADS * HEAD_DIM
    da_w = DA_HEADS * 2 * HEAD_DIM
    in_w = 3 * sb_w + 3 * da_w
    n = b * s
    d_ff = w_ffn_gate.shape[-1]

    tm, tp, tq_sb, tq_da = _tiles(s)

    ra, rm, rp = _rope_tables(s)
    grp = np.arange(da_w) // HEAD_DIM
    gmat = jnp.asarray(np.where(grp[:, None] == grp[None, :], 1.0 / HEAD_DIM, 0.0), dtype=BF16)
    tile_g = lambda g: jnp.tile(g, da_w // HEAD_DIM)[None, :]
    vmem_full = pl.BlockSpec(memory_space=pltpu.VMEM)
    spt = s // tm

    xf = x.reshape(n, d)
    for layer in range(depth):
        lambda_init = 0.8 - 0.6 * math.exp(-0.3 * layer)
        gmix = g_mix[layer][None, :]

        proj = pl.pallas_call(
            functools.partial(_proj_kernel, sb_w=sb_w, da_w=da_w),
            grid=(n // tm,),
            in_specs=[
                pl.BlockSpec((tm, d), lambda t: (t, 0)),
                vmem_full, vmem_full, vmem_full, vmem_full, vmem_full,
                pl.BlockSpec((tm, LANES), lambda t: (t % spt, 0)),
                pl.BlockSpec((tm, LANES), lambda t: (t % spt, 0)),
                pl.BlockSpec((tm, LANES), lambda t: (t % spt, 0)),
            ],
            out_specs=pl.BlockSpec((tm, in_w), lambda t: (t, 0)),
            out_shape=jax.ShapeDtypeStruct((n, in_w), BF16),
            compiler_params=pltpu.CompilerParams(
                dimension_semantics=("arbitrary",), vmem_limit_bytes=VMEM_LIMIT),
            name="proj",
        )(xf, gmix, w_in[layer].astype(BF16), tile_g(g_q[layer]), tile_g(g_k[layer]), gmat,
          ra, rm, rp)

        cb = lambda off: off // LANES
        tq, nq, gw = tq_sb, s // tq_sb, SB_GROUPS * LANES
        o_a = pl.pallas_call(
            functools.partial(_sb_kernel, tq=tq, n_grp=SB_GROUPS),
            grid=(b, sb_w // gw, nq),
            in_specs=[
                pl.BlockSpec((tq, gw), lambda bi, j, i, nq=nq: (bi * nq + i, j)),
                pl.BlockSpec((s, gw), lambda bi, j, i: (bi, sb_w // gw + j)),
                pl.BlockSpec((s, gw), lambda bi, j, i: (bi, 2 * sb_w // gw + j)),
            ],
            out_specs=pl.BlockSpec((tq, gw), lambda bi, j, i, nq=nq: (bi * nq + i, j)),
            out_shape=jax.ShapeDtypeStruct((n, sb_w), BF16),
            scratch_shapes=[pltpu.VMEM((SB_GROUPS, 2 * tq, LANES), BF16),
                            pltpu.VMEM((SB_GROUPS, 2 * tq, LANES), F32),
                            pltpu.VMEM((SB_GROUPS, 2 * tq, LANES), F32)],
            compiler_params=pltpu.CompilerParams(
                dimension_semantics=("arbitrary", "arbitrary", "arbitrary"),
                vmem_limit_bytes=VMEM_LIMIT),
            name="sb_attn",
        )(proj, proj, proj)

        tq = tq_da
        o_b = pl.pallas_call(
            functools.partial(_da_flat_kernel, tq=tq, n_tiles=s // tq, lambda_init=lambda_init),
            grid=(b, DA_HEADS),
            in_specs=[
                pl.BlockSpec((s, LANES), lambda bi, j: (bi, cb(3 * sb_w) + j)),
                pl.BlockSpec((s, LANES), lambda bi, j: (bi, cb(3 * sb_w + da_w) + j)),
                pl.BlockSpec((s, LANES), lambda bi, j: (bi, cb(3 * sb_w + 2 * da_w) + j)),
                vmem_full, vmem_full, vmem_full, vmem_full, vmem_full,
            ],
            out_specs=pl.BlockSpec((s, LANES), lambda bi, j: (bi, j)),
            out_shape=jax.ShapeDtypeStruct((n, da_w), BF16),
            scratch_shapes=[pltpu.VMEM((s, 2 * LANES), BF16),
                            pltpu.VMEM((2, 2 * tq, LANES), BF16),
                            pltpu.VMEM((2, 2 * tq, tq), F32),
                            pltpu.VMEM((2, 2 * tq, LANES), F32)]
            + [pltpu.VMEM((2 * tq, LANES), F32)] * 3,
            compiler_params=pltpu.CompilerParams(
                dimension_semantics=("arbitrary", "arbitrary"),
                vmem_limit_bytes=VMEM_LIMIT),
            name="da_attn",
        )(proj, proj, proj, lam_q1[layer][None, :], lam_k1[layer][None, :],
          lam_q2[layer][None, :], lam_k2[layer][None, :], g_sub[layer][None, :])

        xf = pl.pallas_call(
            functools.partial(_post_kernel, d_model=d),
            grid=(n // tp,),
            in_specs=[
                pl.BlockSpec((tp, d), lambda t: (t, 0)),
                pl.BlockSpec((tp, sb_w), lambda t: (t, 0)),
                pl.BlockSpec((tp, da_w), lambda t: (t, 0)),
            ] + [vmem_full] * 10,
            out_specs=pl.BlockSpec((tp, d), lambda t: (t, 0)),
            out_shape=jax.ShapeDtypeStruct((n, d), F32),
            compiler_params=pltpu.CompilerParams(
                dimension_semantics=("arbitrary",), vmem_limit_bytes=VMEM_LIMIT),
            name="post",
        )(xf, o_a, o_b, gmix, w_gate[layer].astype(BF16), b_gate[layer][None, :],
          w_branch_a[layer].astype(BF16), w_branch_b[layer].astype(BF16),
          w_out[layer].astype(BF16), g_ffn[layer][None, :],
          w_ffn_gate[layer].astype(BF16), w_ffn_up[layer].astype(BF16),
          w_ffn_down[layer].astype(BF16))
    return xf.reshape(b, s, d)
```

```python
import functools
import math

import jax
import jax.numpy as jnp
import numpy as np
from jax import lax
from jax.experimental import pallas as pl
from jax.experimental.pallas import tpu as pltpu

F32 = jnp.float32
BF16 = jnp.bfloat16

CHUNK = 64
SB_HEADS = 8
DA_HEADS = 4
HEAD_DIM = 64
ROPE_THETA = 500000.0
ROPE_DIM = HEAD_DIM // 4
EPS = 1e-6
NEG_INF = -1e30
LOG2E = math.log2(math.e)
SB_GROUPS = 4
SB_STOP_LOG2 = 150.0
SB_EXP2_MAX = 126.0
LANES = 128
VMEM_LIMIT = 56 * 1024 * 1024


def _dot(a, b):
    return jnp.dot(a, b, preferred_element_type=F32)


def _dot_nt(a, b):
    return lax.dot_general(a, b, (((1,), (1,)), ((), ())), preferred_element_type=F32)


def _proj_kernel(x_ref, gmix_ref, win_ref, gq_ref, gk_ref, gmat_ref,
                 ra_ref, rm_ref, rp_ref, out_ref, *, sb_w, da_w):
    x = x_ref[...]
    ms = jnp.mean(x * x, axis=-1, keepdims=True)
    h = x * lax.rsqrt(ms + EPS) * gmix_ref[...]
    proj = _dot(h.astype(BF16), win_ref[...])
    scale = HEAD_DIM ** -0.5

    out_ref[:, 0:sb_w] = (proj[:, 0:sb_w] * (scale * LOG2E)).astype(BF16)
    out_ref[:, sb_w:3 * sb_w] = proj[:, sb_w:3 * sb_w].astype(BF16)

    ra, rm, rp = ra_ref[...], rm_ref[...], rp_ref[...]

    def qk_norm_rope(t, g, mult):
        msq = _dot((t * t).astype(BF16), gmat_ref[...])
        tn = t * lax.rsqrt(msq + EPS) * g
        cols = []
        for j in range(da_w // LANES):
            c = tn[:, j * LANES:(j + 1) * LANES]
            r = (c * ra + pltpu.roll(c, LANES - ROPE_DIM // 2, 1) * rm
                 + pltpu.roll(c, ROPE_DIM // 2, 1) * rp)
            cols.append((r * mult).astype(BF16))
        return jnp.concatenate(cols, axis=1)

    o = 3 * sb_w
    out_ref[:, o:o + da_w] = qk_norm_rope(proj[:, o:o + da_w], gq_ref[...], scale * LOG2E)
    out_ref[:, o + da_w:o + 2 * da_w] = qk_norm_rope(proj[:, o + da_w:o + 2 * da_w], gk_ref[...], 1.0)
    out_ref[:, o + 2 * da_w:o + 3 * da_w] = proj[:, o + 2 * da_w:o + 3 * da_w].astype(BF16)


def _stack_masked(q, n_parts):
    lane = lax.broadcasted_iota(jnp.int32, q.shape, 1)
    zero = jnp.zeros_like(q)
    return jnp.concatenate(
        [jnp.where((lane >= p * HEAD_DIM) & (lane < (p + 1) * HEAD_DIM), q, zero)
         for p in range(n_parts)], axis=0)


def _sb_kernel(q_ref, k_ref, v_ref, o_ref, qs_ref, acc_ref, run_ref, *, tq, n_grp):
    i = pl.program_id(2)
    m2 = 2 * tq
    r2 = lax.broadcasted_iota(jnp.int32, (tq, tq), 0)
    c2 = lax.broadcasted_iota(jnp.int32, (tq, tq), 1)
    tri = jnp.where(r2 > c2, 1.0, 0.0).astype(BF16)
    lanes = lambda g: slice(g * LANES, (g + 1) * LANES)

    for g in range(n_grp):
        qs_ref[g] = _stack_masked(q_ref[:, lanes(g)], 2)

    def block(g, blk, run, diagonal=False):
        ks = pl.multiple_of(blk * tq, tq)
        z = _dot_nt(qs_ref[g], k_ref[pl.ds(ks, tq), lanes(g)])
        if diagonal:
            row = lax.broadcasted_iota(jnp.int32, (m2, tq), 0)
            col = lax.broadcasted_iota(jnp.int32, (m2, tq), 1)
            z = jnp.where(col < jnp.where(row >= tq, row - tq, row), z, NEG_INF)
        p = jnp.maximum(z, jnp.log2(1.0 + jnp.exp2(jnp.minimum(z, SB_EXP2_MAX))))
        after = _dot(p.astype(BF16), tri)
        spent = after + jnp.concatenate([run] * (tq // LANES), axis=1)
        run = run + jnp.broadcast_to(after[:, 0:1] + p[:, 0:1], (m2, LANES))
        a = jnp.exp2((z - p) - spent)
        return run, _dot(a.astype(BF16), v_ref[pl.ds(ks, tq), lanes(g)])

    zeros = jnp.zeros((m2, LANES), F32)

    @pl.when(i == 0)
    def _():
        for g in range(n_grp):
            run_ref[g], acc_ref[g] = block(g, i, zeros, diagonal=True)

    @pl.when(i > 0)
    def _():
        for g in range(n_grp):
            run, out = block(g, i, zeros, diagonal=True)
            run, out2 = block(g, i - 1, run)
            run_ref[g] = run
            acc_ref[g] = out + out2

    def more(carry):
        blk, least = carry
        return jnp.logical_and(blk >= 0, least < SB_STOP_LOG2)

    def step(carry):
        blk, _ = carry
        least = None
        for g in range(n_grp):
            run, out = block(g, blk, run_ref[g])
            run_ref[g] = run
            acc_ref[g] += out
            least = jnp.min(run) if least is None else jnp.minimum(least, jnp.min(run))
        return blk - 1, least

    lax.while_loop(more, step, (i - 2, jnp.min(run_ref[...])))
    lane = lax.broadcasted_iota(jnp.int32, (tq, LANES), 1)
    for g in range(n_grp):
        o_ref[:, lanes(g)] = jnp.where(lane < HEAD_DIM, acc_ref[g, 0:tq, :],
                                       acc_ref[g, tq:m2, :]).astype(o_ref.dtype)


def _da_flat_kernel(q_ref, k_ref, v_ref, lq1_ref, lk1_ref, lq2_ref, lk2_ref, gsub_ref, o_ref,
                    vext_ref, qs_ref, s_ref, mc_ref, m_ref, l_ref, acc_ref, *, tq, n_tiles,
                    lambda_init):
    m2 = 2 * tq
    reps = tq // LANES
    vext_ref[:, 0:LANES] = v_ref[...]
    vext_ref[:, LANES:2 * LANES] = jnp.ones(v_ref.shape, v_ref.dtype)
    lam = (jnp.exp(jnp.sum(lq1_ref[...] * lk1_ref[...], axis=-1, keepdims=True))
           - jnp.exp(jnp.sum(lq2_ref[...] * lk2_ref[...], axis=-1, keepdims=True))
           + lambda_init)
    steps = [(i, blk) for i in range(n_tiles) for blk in range(i + 1)]
    rows = lambda j: slice(j * tq, (j + 1) * tq)

    def scores(t):
        i, blk = steps[t]
        if blk == 0:
            qs_ref[i % 2] = _stack_masked(q_ref[rows(i), :], 2)
        s = _dot_nt(qs_ref[i % 2], k_ref[rows(blk), :])
        s_ref[t % 2] = s
        mc_ref[t % 2] = jnp.broadcast_to(jnp.max(s, axis=1, keepdims=True), (m2, LANES))

    def absorb(t):
        i, blk = steps[t]
        s = s_ref[t % 2]
        if blk == i:
            row = lax.broadcasted_iota(jnp.int32, (m2, tq), 0)
            col = lax.broadcasted_iota(jnp.int32, (m2, tq), 1)
            qrow = jnp.where(row >= tq, row - tq, row)
            s = jnp.where((col // CHUNK) <= (qrow // CHUNK), s, NEG_INF)
            m_cur = jnp.broadcast_to(jnp.max(s, axis=1, keepdims=True), (m2, LANES))
        else:
            m_cur = mc_ref[t % 2]
        m_new = m_cur if blk == 0 else jnp.maximum(m_ref[...], m_cur)
        p = jnp.exp2(s - jnp.concatenate([m_new] * reps, axis=1))
        pv = _dot(p.astype(BF16), vext_ref[rows(blk), :])
        if blk == 0:
            acc, l = pv[:, :LANES], pv[:, LANES:]
        else:
            alpha = jnp.exp2(m_ref[...] - m_new)
            acc = alpha * acc_ref[...] + pv[:, :LANES]
            l = alpha * l_ref[...] + pv[:, LANES:]
        if blk < i:
            acc_ref[...], l_ref[...], m_ref[...] = acc, l, m_new
        else:
            res = acc / l
            o = res[0:tq, :] - lam * res[tq:m2, :]
            ms = jnp.mean(o * o, axis=-1, keepdims=True)
            o = o * lax.rsqrt(ms + EPS) * gsub_ref[...] * (1.0 - lambda_init)
            o_ref[rows(i), :] = o.astype(o_ref.dtype)

    scores(0)
    for t in range(len(steps)):
        if t + 1 < len(steps):
            scores(t + 1)
        absorb(t)


def _post_kernel(x_ref, oa_ref, ob_ref, gmix_ref, wgate_ref, bgate_ref, wa_ref, wb_ref, wout_ref,
                 gffn_ref, wfg_ref, wfu_ref, wfd_ref, out_ref, *, d_model):
    x = x_ref[...]
    ms = jnp.mean(x * x, axis=-1, keepdims=True)
    h = (x * lax.rsqrt(ms + EPS) * gmix_ref[...]).astype(BF16)
    gates = jax.nn.sigmoid(_dot(h, wgate_ref[...]) + bgate_ref[...])
    br_a = _dot(oa_ref[...], wa_ref[...])
    br_b = _dot(ob_ref[...], wb_ref[...])
    merged = gates[:, :d_model] * br_a + gates[:, d_model:] * br_b
    x1 = x + _dot(merged.astype(BF16), wout_ref[...])

    ms2 = jnp.mean(x1 * x1, axis=-1, keepdims=True)
    h2 = (x1 * lax.rsqrt(ms2 + EPS) * gffn_ref[...]).astype(BF16)
    fg = _dot(h2, wfg_ref[...])
    fu = _dot(h2, wfu_ref[...])
    ff = (fg * jax.nn.sigmoid(fg)) * fu
    out_ref[...] = x1 + _dot(ff.astype(BF16), wfd_ref[...])


def _rope_tables(seq):
    half = ROPE_DIM // 2
    pos = np.arange(seq, dtype=np.float64)
    inv_freq = ROPE_THETA ** (-np.arange(0, ROPE_DIM, 2, dtype=np.float64) / ROPE_DIM)
    ang = pos[:, None] * inv_freq[None, :]
    cos, sin = np.cos(ang), np.sin(ang)
    ones = np.ones((seq, HEAD_DIM - ROPE_DIM))
    zeros = np.zeros((seq, HEAD_DIM - ROPE_DIM))
    zh = np.zeros((seq, half))
    ra = np.concatenate([cos, cos, ones], axis=1)
    rm = np.concatenate([-sin, zh, zeros], axis=1)
    rp = np.concatenate([zh, sin, zeros], axis=1)
    tile = lambda t: jnp.asarray(np.tile(t, (1, LANES // HEAD_DIM)), dtype=F32)
    return tile(ra), tile(rm), tile(rp)


def _tiles(seq):
    tm, tp, tq_sb, tq_da = 1024, 512, 256, 512
    for t in (tm, tp, tq_sb, tq_da):
        assert seq % t == 0 and t % LANES == 0 and t % CHUNK == 0
    return tm, tp, tq_sb, tq_da


def kernel(x, g_mix, w_in, g_q, g_k, lam_q1, lam_k1, lam_q2, lam_k2, g_sub, w_branch_a, w_branch_b,
           w_gate, b_gate, w_out, g_ffn, w_ffn_gate, w_ffn_up, w_ffn_down):
    b, s, d = x.shape
    depth = g_mix.shape[0]
    sb_w = SB_HEADS * HEAD_DIM
    da_w = DA_HEADS * 2 * HEAD_DIM
    in_w = 3 * sb_w + 3 * da_w
    n = b * s
    d_ff = w_ffn_gate.shape[-1]

    tm, tp, tq_sb, tq_da = _tiles(s)

    ra, rm, rp = _rope_tables(s)
    grp = np.arange(da_w) // HEAD_DIM
    gmat = jnp.asarray(np.where(grp[:, None] == grp[None, :], 1.0 / HEAD_DIM, 0.0), dtype=BF16)
    tile_g = lambda g: jnp.tile(g, da_w // HEAD_DIM)[None, :]
    vmem_full = pl.BlockSpec(memory_space=pltpu.VMEM)
    spt = s // tm

    xf = x.reshape(n, d)
    for layer in range(depth):
        lambda_init = 0.8 - 0.6 * math.exp(-0.3 * layer)
        gmix = g_mix[layer][None, :]

        proj = pl.pallas_call(
            functools.partial(_proj_kernel, sb_w=sb_w, da_w=da_w),
            grid=(n // tm,),
            in_specs=[
                pl.BlockSpec((tm, d), lambda t: (t, 0)),
                vmem_full, vmem_full, vmem_full, vmem_full, vmem_full,
                pl.BlockSpec((tm, LANES), lambda t: (t % spt, 0)),
                pl.BlockSpec((tm, LANES), lambda t: (t % spt, 0)),
                pl.BlockSpec((tm, LANES), lambda t: (t % spt, 0)),
            ],
            out_specs=pl.BlockSpec((tm, in_w), lambda t: (t, 0)),
            out_shape=jax.ShapeDtypeStruct((n, in_w), BF16),
            compiler_params=pltpu.CompilerParams(
                dimension_semantics=("arbitrary",), vmem_limit_bytes=VMEM_LIMIT),
            name="proj",
        )(xf, gmix, w_in[layer].astype(BF16), tile_g(g_q[layer]), tile_g(g_k[layer]), gmat,
          ra, rm, rp)

        cb = lambda off: off // LANES
        tq, nq, gw = tq_sb, s // tq_sb, SB_GROUPS * LANES
        o_a = pl.pallas_call(
            functools.partial(_sb_kernel, tq=tq, n_grp=SB_GROUPS),
            grid=(b, sb_w // gw, nq),
            in_specs=[
                pl.BlockSpec((tq, gw), lambda bi, j, i, nq=nq: (bi * nq + i, j)),
                pl.BlockSpec((s, gw), lambda bi, j, i: (bi, sb_w // gw + j)),
                pl.BlockSpec((s, gw), lambda bi, j, i: (bi, 2 * sb_w // gw + j)),
            ],
            out_specs=pl.BlockSpec((tq, gw), lambda bi, j, i, nq=nq: (bi * nq + i, j)),
            out_shape=jax.ShapeDtypeStruct((n, sb_w), BF16),
            scratch_shapes=[pltpu.VMEM((SB_GROUPS, 2 * tq, LANES), BF16),
                            pltpu.VMEM((SB_GROUPS, 2 * tq, LANES), F32),
                            pltpu.VMEM((SB_GROUPS, 2 * tq, LANES), F32)],
            compiler_params=pltpu.CompilerParams(
                dimension_semantics=("arbitrary", "arbitrary", "arbitrary"),
                vmem_limit_bytes=VMEM_LIMIT),
            name="sb_attn",
        )(proj, proj, proj)

        tq = tq_da
        o_b = pl.pallas_call(
            functools.partial(_da_flat_kernel, tq=tq, n_tiles=s // tq, lambda_init=lambda_init),
            grid=(b, DA_HEADS),
            in_specs=[
                pl.BlockSpec((s, LANES), lambda bi, j: (bi, cb(3 * sb_w) + j)),
                pl.BlockSpec((s, LANES), lambda bi, j: (bi, cb(3 * sb_w + da_w) + j)),
                pl.BlockSpec((s, LANES), lambda bi, j: (bi, cb(3 * sb_w + 2 * da_w) + j)),
                vmem_full, vmem_full, vmem_full, vmem_full, vmem_full,
            ],
            out_specs=pl.BlockSpec((s, LANES), lambda bi, j: (bi, j)),
            out_shape=jax.ShapeDtypeStruct((n, da_w), BF16),
            scratch_shapes=[pltpu.VMEM((s, 2 * LANES), BF16),
                            pltpu.VMEM((2, 2 * tq, LANES), BF16),
                            pltpu.VMEM((2, 2 * tq, tq), F32),
                            pltpu.VMEM((2, 2 * tq, LANES), F32)]
            + [pltpu.VMEM((2 * tq, LANES), F32)] * 3,
            compiler_params=pltpu.CompilerParams(
                dimension_semantics=("arbitrary", "arbitrary"),
                vmem_limit_bytes=VMEM_LIMIT),
            name="da_attn",
        )(proj, proj, proj, lam_q1[layer][None, :], lam_k1[layer][None, :],
          lam_q2[layer][None, :], lam_k2[layer][None, :], g_sub[layer][None, :])

        xf = pl.pallas_call(
            functools.partial(_post_kernel, d_model=d),
            grid=(n // tp,),
            in_specs=[
                pl.BlockSpec((tp, d), lambda t: (t, 0)),
                pl.BlockSpec((tp, sb_w), lambda t: (t, 0)),
                pl.BlockSpec((tp, da_w), lambda t: (t, 0)),
            ] + [vmem_full] * 10,
            out_specs=pl.BlockSpec((tp, d), lambda t: (t, 0)),
            out_shape=jax.ShapeDtypeStruct((n, d), F32),
            compiler_params=pltpu.CompilerParams(
                dimension_semantics=("arbitrary",), vmem_limit_bytes=VMEM_LIMIT),
            name="post",
        )(xf, o_a, o_b, gmix, w_gate[layer].astype(BF16), b_gate[layer][None, :],
          w_branch_a[layer].astype(BF16), w_branch_b[layer].astype(BF16),
          w_out[layer].astype(BF16), g_ffn[layer][None, :],
          w_ffn_gate[layer].astype(BF16), w_ffn_up[layer].astype(BF16),
          w_ffn_down[layer].astype(BF16))
    return xf.reshape(b, s, d)
```

```python
import functools
import math

import jax
import jax.numpy as jnp
import numpy as np
from jax import lax
from jax.experimental import pallas as pl
from jax.experimental.pallas import tpu as pltpu

F32 = jnp.float32
BF16 = jnp.bfloat16

CHUNK = 64
SB_HEADS = 8
DA_HEADS = 4
HEAD_DIM = 64
ROPE_THETA = 500000.0
ROPE_DIM = HEAD_DIM // 4
EPS = 1e-6
NEG_INF = -1e30
LOG2E = math.log2(math.e)
SB_GROUPS = 4
SB_TILES = 2
SB_STOP_LOG2 = 150.0
SB_EXP2_MAX = 126.0
LANES = 128
VMEM_LIMIT = 56 * 1024 * 1024


def _dot(a, b):
    return jnp.dot(a, b, preferred_element_type=F32)


def _dot_nt(a, b):
    return lax.dot_general(a, b, (((1,), (1,)), ((), ())), preferred_element_type=F32)


def _proj_kernel(x_ref, gmix_ref, win_ref, gq_ref, gk_ref, gmat_ref,
                 ra_ref, rm_ref, rp_ref, out_ref, *, sb_w, da_w):
    x = x_ref[...]
    ms = jnp.mean(x * x, axis=-1, keepdims=True)
    h = x * lax.rsqrt(ms + EPS) * gmix_ref[...]
    proj = _dot(h.astype(BF16), win_ref[...])
    scale = HEAD_DIM ** -0.5

    out_ref[:, 0:sb_w] = (proj[:, 0:sb_w] * (scale * LOG2E)).astype(BF16)
    out_ref[:, sb_w:3 * sb_w] = proj[:, sb_w:3 * sb_w].astype(BF16)

    ra, rm, rp = ra_ref[...], rm_ref[...], rp_ref[...]

    def qk_norm_rope(t, g, mult):
        msq = _dot((t * t).astype(BF16), gmat_ref[...])
        tn = t * lax.rsqrt(msq + EPS) * g
        cols = []
        for j in range(da_w // LANES):
            c = tn[:, j * LANES:(j + 1) * LANES]
            r = (c * ra + pltpu.roll(c, LANES - ROPE_DIM // 2, 1) * rm
                 + pltpu.roll(c, ROPE_DIM // 2, 1) * rp)
            cols.append((r * mult).astype(BF16))
        return jnp.concatenate(cols, axis=1)

    o = 3 * sb_w
    out_ref[:, o:o + da_w] = qk_norm_rope(proj[:, o:o + da_w], gq_ref[...], scale * LOG2E)
    out_ref[:, o + da_w:o + 2 * da_w] = qk_norm_rope(proj[:, o + da_w:o + 2 * da_w], gk_ref[...], 1.0)
    out_ref[:, o + 2 * da_w:o + 3 * da_w] = proj[:, o + 2 * da_w:o + 3 * da_w].astype(BF16)


def _stack_masked(q, n_parts):
    lane = lax.broadcasted_iota(jnp.int32, q.shape, 1)
    zero = jnp.zeros_like(q)
    return jnp.concatenate(
        [jnp.where((lane >= p * HEAD_DIM) & (lane < (p + 1) * HEAD_DIM), q, zero)
         for p in range(n_parts)], axis=0)


def _sb_kernel(q_ref, k_ref, v_ref, o_ref, qs_ref, acc_ref, run_ref, *, tq, n_grp, n_seg):
    first_tile = pl.program_id(2) * n_seg
    m2 = 2 * tq
    r2 = lax.broadcasted_iota(jnp.int32, (tq, tq), 0)
    c2 = lax.broadcasted_iota(jnp.int32, (tq, tq), 1)
    tri = jnp.where(r2 > c2, 1.0, 0.0).astype(BF16)
    lanes = lambda g: slice(g * LANES, (g + 1) * LANES)
    rows = lambda seg: slice(seg * tq, (seg + 1) * tq)

    def block(seg, g, blk, run, diagonal=False, gate=None):
        ks = pl.multiple_of(blk * tq, tq)
        z = _dot_nt(qs_ref[seg, g], k_ref[pl.ds(ks, tq), lanes(g)])
        if diagonal:
            row = lax.broadcasted_iota(jnp.int32, (m2, tq), 0)
            col = lax.broadcasted_iota(jnp.int32, (m2, tq), 1)
            z = jnp.where(col < jnp.where(row >= tq, row - tq, row), z, NEG_INF)
        if gate is not None:
            z = jnp.where(gate > 0.0, z, NEG_INF)
        p = jnp.maximum(z, jnp.log2(1.0 + jnp.exp2(jnp.minimum(z, SB_EXP2_MAX))))
        after = _dot(p.astype(BF16), tri)
        spent = after + jnp.concatenate([run] * (tq // LANES), axis=1)
        run = run + jnp.broadcast_to(after[:, 0:1] + p[:, 0:1], (m2, LANES))
        a = jnp.exp2((z - p) - spent)
        return run, _dot(a.astype(BF16), v_ref[pl.ds(ks, tq), lanes(g)])

    zeros = jnp.zeros((m2, LANES), F32)
    for seg in range(n_seg):
        i = first_tile + seg
        for g in range(n_grp):
            qs_ref[seg, g] = _stack_masked(q_ref[rows(seg), lanes(g)], 2)
            run, out = block(seg, g, i, zeros, diagonal=True)
            if seg == 0:
                run, out2 = block(seg, g, jnp.maximum(i - 1, 0), run,
                                  gate=jnp.where(i > 0, 1.0, 0.0).astype(F32))
            else:
                run, out2 = block(seg, g, i - 1, run)
            run_ref[seg, g] = run
            acc_ref[seg, g] = out + out2

    def more(carry):
        blk, least = carry
        return jnp.logical_and(blk >= 0, least < SB_STOP_LOG2)

    lane = lax.broadcasted_iota(jnp.int32, (tq, LANES), 1)
    for seg in range(n_seg):
        def step(carry, seg=seg):
            blk, _ = carry
            least = None
            for g in range(n_grp):
                run, out = block(seg, g, blk, run_ref[seg, g])
                run_ref[seg, g] = run
                acc_ref[seg, g] += out
                least = jnp.min(run) if least is None else jnp.minimum(least, jnp.min(run))
            return blk - 1, least

        lax.while_loop(more, step, (first_tile + seg - 2, jnp.min(run_ref[seg])))
        for g in range(n_grp):
            o_ref[rows(seg), lanes(g)] = jnp.where(
                lane < HEAD_DIM, acc_ref[seg, g, 0:tq, :], acc_ref[seg, g, tq:m2, :]
            ).astype(o_ref.dtype)


def _da_flat_kernel(q_ref, k_ref, v_ref, lq1_ref, lk1_ref, lq2_ref, lk2_ref, gsub_ref, o_ref,
                    vext_ref, qs_ref, s_ref, mc_ref, m_ref, l_ref, acc_ref, *, tq, n_tiles,
                    lambda_init):
    m2 = 2 * tq
    reps = tq // LANES
    vext_ref[:, 0:LANES] = v_ref[...]
    vext_ref[:, LANES:2 * LANES] = jnp.ones(v_ref.shape, v_ref.dtype)
    lam = (jnp.exp(jnp.sum(lq1_ref[...] * lk1_ref[...], axis=-1, keepdims=True))
           - jnp.exp(jnp.sum(lq2_ref[...] * lk2_ref[...], axis=-1, keepdims=True))
           + lambda_init)
    steps = [(i, blk) for i in range(n_tiles) for blk in range(i + 1)]
    rows = lambda j: slice(j * tq, (j + 1) * tq)

    def scores(t):
        i, blk = steps[t]
        if blk == 0:
            qs_ref[i % 2] = _stack_masked(q_ref[rows(i), :], 2)
        s = _dot_nt(qs_ref[i % 2], k_ref[rows(blk), :])
        s_ref[t % 2] = s
        mc_ref[t % 2] = jnp.broadcast_to(jnp.max(s, axis=1, keepdims=True), (m2, LANES))

    def absorb(t):
        i, blk = steps[t]
        s = s_ref[t % 2]
        if blk == i:
            row = lax.broadcasted_iota(jnp.int32, (m2, tq), 0)
            col = lax.broadcasted_iota(jnp.int32, (m2, tq), 1)
            qrow = jnp.where(row >= tq, row - tq, row)
            s = jnp.where((col // CHUNK) <= (qrow // CHUNK), s, NEG_INF)
            m_cur = jnp.broadcast_to(jnp.max(s, axis=1, keepdims=True), (m2, LANES))
        else:
            m_cur = mc_ref[t % 2]
        m_new = m_cur if blk == 0 else jnp.maximum(m_ref[...], m_cur)
        p = jnp.exp2(s - jnp.concatenate([m_new] * reps, axis=1))
        pv = _dot(p.astype(BF16), vext_ref[rows(blk), :])
        if blk == 0:
            acc, l = pv[:, :LANES], pv[:, LANES:]
        else:
            alpha = jnp.exp2(m_ref[...] - m_new)
            acc = alpha * acc_ref[...] + pv[:, :LANES]
            l = alpha * l_ref[...] + pv[:, LANES:]
        if blk < i:
            acc_ref[...], l_ref[...], m_ref[...] = acc, l, m_new
        else:
            res = acc / l
            o = res[0:tq, :] - lam * res[tq:m2, :]
            ms = jnp.mean(o * o, axis=-1, keepdims=True)
            o = o * lax.rsqrt(ms + EPS) * gsub_ref[...] * (1.0 - lambda_init)
            o_ref[rows(i), :] = o.astype(o_ref.dtype)

    scores(0)
    for t in range(len(steps)):
        if t + 1 < len(steps):
            scores(t + 1)
        absorb(t)


def _post_kernel(x_ref, oa_ref, ob_ref, gmix_ref, wgate_ref, bgate_ref, wa_ref, wb_ref, wout_ref,
                 gffn_ref, wfg_ref, wfu_ref, wfd_ref, out_ref, *, d_model):
    x = x_ref[...]
    ms = jnp.mean(x * x, axis=-1, keepdims=True)
    h = (x * lax.rsqrt(ms + EPS) * gmix_ref[...]).astype(BF16)
    gates = jax.nn.sigmoid(_dot(h, wgate_ref[...]) + bgate_ref[...])
    br_a = _dot(oa_ref[...], wa_ref[...])
    br_b = _dot(ob_ref[...], wb_ref[...])
    merged = gates[:, :d_model] * br_a + gates[:, d_model:] * br_b
    x1 = x + _dot(merged.astype(BF16), wout_ref[...])

    ms2 = jnp.mean(x1 * x1, axis=-1, keepdims=True)
    h2 = (x1 * lax.rsqrt(ms2 + EPS) * gffn_ref[...]).astype(BF16)
    fg = _dot(h2, wfg_ref[...])
    fu = _dot(h2, wfu_ref[...])
    ff = (fg * jax.nn.sigmoid(fg)) * fu
    out_ref[...] = x1 + _dot(ff.astype(BF16), wfd_ref[...])


def _rope_tables(seq):
    half = ROPE_DIM // 2
    pos = np.arange(seq, dtype=np.float64)
    inv_freq = ROPE_THETA ** (-np.arange(0, ROPE_DIM, 2, dtype=np.float64) / ROPE_DIM)
    ang = pos[:, None] * inv_freq[None, :]
    cos, sin = np.cos(ang), np.sin(ang)
    ones = np.ones((seq, HEAD_DIM - ROPE_DIM))
    zeros = np.zeros((seq, HEAD_DIM - ROPE_DIM))
    zh = np.zeros((seq, half))
    ra = np.concatenate([cos, cos, ones], axis=1)
    rm = np.concatenate([-sin, zh, zeros], axis=1)
    rp = np.concatenate([zh, sin, zeros], axis=1)
    tile = lambda t: jnp.asarray(np.tile(t, (1, LANES // HEAD_DIM)), dtype=F32)
    return tile(ra), tile(rm), tile(rp)


def _tiles(seq):
    tm, tp, tq_sb, tq_da = 1024, 512, 256, 512
    for t in (tm, tp, tq_sb, tq_da):
        assert seq % t == 0 and t % LANES == 0 and t % CHUNK == 0
    return tm, tp, tq_sb, tq_da


def kernel(x, g_mix, w_in, g_q, g_k, lam_q1, lam_k1, lam_q2, lam_k2, g_sub, w_branch_a, w_branch_b,
           w_gate, b_gate, w_out, g_ffn, w_ffn_gate, w_ffn_up, w_ffn_down):
    b, s, d = x.shape
    depth = g_mix.shape[0]
    sb_w = SB_HEADS * HEAD_DIM
    da_w = DA_HEADS * 2 * HEAD_DIM
    in_w = 3 * sb_w + 3 * da_w
    n = b * s
    d_ff = w_ffn_gate.shape[-1]

    tm, tp, tq_sb, tq_da = _tiles(s)

    ra, rm, rp = _rope_tables(s)
    grp = np.arange(da_w) // HEAD_DIM
    gmat = jnp.asarray(np.where(grp[:, None] == grp[None, :], 1.0 / HEAD_DIM, 0.0), dtype=BF16)
    tile_g = lambda g: jnp.tile(g, da_w // HEAD_DIM)[None, :]
    vmem_full = pl.BlockSpec(memory_space=pltpu.VMEM)
    spt = s // tm

    xf = x.reshape(n, d)
    for layer in range(depth):
        lambda_init = 0.8 - 0.6 * math.exp(-0.3 * layer)
        gmix = g_mix[layer][None, :]

        proj = pl.pallas_call(
            functools.partial(_proj_kernel, sb_w=sb_w, da_w=da_w),
            grid=(n // tm,),
            in_specs=[
                pl.BlockSpec((tm, d), lambda t: (t, 0)),
                vmem_full, vmem_full, vmem_full, vmem_full, vmem_full,
                pl.BlockSpec((tm, LANES), lambda t: (t % spt, 0)),
                pl.BlockSpec((tm, LANES), lambda t: (t % spt, 0)),
                pl.BlockSpec((tm, LANES), lambda t: (t % spt, 0)),
            ],
            out_specs=pl.BlockSpec((tm, in_w), lambda t: (t, 0)),
            out_shape=jax.ShapeDtypeStruct((n, in_w), BF16),
            compiler_params=pltpu.CompilerParams(
                dimension_semantics=("arbitrary",), vmem_limit_bytes=VMEM_LIMIT),
            name="proj",
        )(xf, gmix, w_in[layer].astype(BF16), tile_g(g_q[layer]), tile_g(g_k[layer]), gmat,
          ra, rm, rp)

        cb = lambda off: off // LANES
        tq, gw, ns = tq_sb, SB_GROUPS * LANES, SB_TILES
        nq = s // (ns * tq)
        o_a = pl.pallas_call(
            functools.partial(_sb_kernel, tq=tq, n_grp=SB_GROUPS, n_seg=ns),
            grid=(b, sb_w // gw, nq),
            in_specs=[
                pl.BlockSpec((ns * tq, gw), lambda bi, j, i, nq=nq: (bi * nq + i, j)),
                pl.BlockSpec((s, gw), lambda bi, j, i: (bi, sb_w // gw + j)),
                pl.BlockSpec((s, gw), lambda bi, j, i: (bi, 2 * sb_w // gw + j)),
            ],
            out_specs=pl.BlockSpec((ns * tq, gw), lambda bi, j, i, nq=nq: (bi * nq + i, j)),
            out_shape=jax.ShapeDtypeStruct((n, sb_w), BF16),
            scratch_shapes=[pltpu.VMEM((ns, SB_GROUPS, 2 * tq, LANES), BF16),
                            pltpu.VMEM((ns, SB_GROUPS, 2 * tq, LANES), F32),
                            pltpu.VMEM((ns, SB_GROUPS, 2 * tq, LANES), F32)],
            compiler_params=pltpu.CompilerParams(
                dimension_semantics=("arbitrary", "arbitrary", "arbitrary"),
                vmem_limit_bytes=VMEM_LIMIT),
            name="sb_attn",
        )(proj, proj, proj)

        tq = tq_da
        o_b = pl.pallas_call(
            functools.partial(_da_flat_kernel, tq=tq, n_tiles=s // tq, lambda_init=lambda_init),
            grid=(b, DA_HEADS),
            in_specs=[
                pl.BlockSpec((s, LANES), lambda bi, j: (bi, cb(3 * sb_w) + j)),
                pl.BlockSpec((s, LANES), lambda bi, j: (bi, cb(3 * sb_w + da_w) + j)),
                pl.BlockSpec((s, LANES), lambda bi, j: (bi, cb(3 * sb_w + 2 * da_w) + j)),
                vmem_full, vmem_full, vmem_full, vmem_full, vmem_full,
            ],
            out_specs=pl.BlockSpec((s, LANES), lambda bi, j: (bi, j)),
            out_shape=jax.ShapeDtypeStruct((n, da_w), BF16),
            scratch_shapes=[pltpu.VMEM((s, 2 * LANES), BF16),
                            pltpu.VMEM((2, 2 * tq, LANES), BF16),
                            pltpu.VMEM((2, 2 * tq, tq), F32),
                            pltpu.VMEM((2, 2 * tq, LANES), F32)]
            + [pltpu.VMEM((2 * tq, LANES), F32)] * 3,
            compiler_params=pltpu.CompilerParams(
                dimension_semantics=("arbitrary", "arbitrary"),
                vmem_limit_bytes=VMEM_LIMIT),
            name="da_attn",
        )(proj, proj, proj, lam_q1[layer][None, :], lam_k1[layer][None, :],
          lam_q2[layer][None, :], lam_k2[layer][None, :], g_sub[layer][None, :])

        xf = pl.pallas_call(
            functools.partial(_post_kernel, d_model=d),
            grid=(n // tp,),
            in_specs=[
                pl.BlockSpec((tp, d), lambda t: (t, 0)),
                pl.BlockSpec((tp, sb_w), lambda t: (t, 0)),
                pl.BlockSpec((tp, da_w), lambda t: (t, 0)),
            ] + [vmem_full] * 10,
            out_specs=pl.BlockSpec((tp, d), lambda t: (t, 0)),
            out_shape=jax.ShapeDtypeStruct((n, d), F32),
            compiler_params=pltpu.CompilerParams(
                dimension_semantics=("arbitrary",), vmem_limit_bytes=VMEM_LIMIT),
            name="post",
        )(xf, o_a, o_b, gmix, w_gate[layer].astype(BF16), b_gate[layer][None, :],
          w_branch_a[layer].astype(BF16), w_branch_b[layer].astype(BF16),
          w_out[layer].astype(BF16), g_ffn[layer][None, :],
          w_ffn_gate[layer].astype(BF16), w_ffn_up[layer].astype(BF16),
          w_ffn_down[layer].astype(BF16))
    return xf.reshape(b, s, d)
```

```python
import functools
import math

import jax
import jax.numpy as jnp
import numpy as np
from jax import lax
from jax.experimental import pallas as pl
from jax.experimental.pallas import tpu as pltpu

F32 = jnp.float32
BF16 = jnp.bfloat16

CHUNK = 64
SB_HEADS = 8
DA_HEADS = 4
HEAD_DIM = 64
ROPE_THETA = 500000.0
ROPE_DIM = HEAD_DIM // 4
EPS = 1e-6
NEG_INF = -1e30
LOG2E = math.log2(math.e)
SB_GROUPS = 4
SB_TILES = 4
SB_STOP_LOG2 = 150.0
SB_EXP2_MAX = 126.0
LANES = 128
VMEM_LIMIT = 56 * 1024 * 1024


def _dot(a, b):
    return jnp.dot(a, b, preferred_element_type=F32)


def _dot_nt(a, b):
    return lax.dot_general(a, b, (((1,), (1,)), ((), ())), preferred_element_type=F32)


def _proj_kernel(x_ref, gmix_ref, win_ref, gq_ref, gk_ref, gmat_ref,
                 ra_ref, rm_ref, rp_ref, out_ref, *, sb_w, da_w):
    x = x_ref[...]
    ms = jnp.mean(x * x, axis=-1, keepdims=True)
    h = x * lax.rsqrt(ms + EPS) * gmix_ref[...]
    proj = _dot(h.astype(BF16), win_ref[...])
    scale = HEAD_DIM ** -0.5

    out_ref[:, 0:sb_w] = (proj[:, 0:sb_w] * (scale * LOG2E)).astype(BF16)
    out_ref[:, sb_w:3 * sb_w] = proj[:, sb_w:3 * sb_w].astype(BF16)

    ra, rm, rp = ra_ref[...], rm_ref[...], rp_ref[...]

    def qk_norm_rope(t, g, mult):
        msq = _dot((t * t).astype(BF16), gmat_ref[...])
        tn = t * lax.rsqrt(msq + EPS) * g
        cols = []
        for j in range(da_w // LANES):
            c = tn[:, j * LANES:(j + 1) * LANES]
            r = (c * ra + pltpu.roll(c, LANES - ROPE_DIM // 2, 1) * rm
                 + pltpu.roll(c, ROPE_DIM // 2, 1) * rp)
            cols.append((r * mult).astype(BF16))
        return jnp.concatenate(cols, axis=1)

    o = 3 * sb_w
    out_ref[:, o:o + da_w] = qk_norm_rope(proj[:, o:o + da_w], gq_ref[...], scale * LOG2E)
    out_ref[:, o + da_w:o + 2 * da_w] = qk_norm_rope(proj[:, o + da_w:o + 2 * da_w], gk_ref[...], 1.0)
    out_ref[:, o + 2 * da_w:o + 3 * da_w] = proj[:, o + 2 * da_w:o + 3 * da_w].astype(BF16)


def _stack_masked(q, n_parts):
    lane = lax.broadcasted_iota(jnp.int32, q.shape, 1)
    zero = jnp.zeros_like(q)
    return jnp.concatenate(
        [jnp.where((lane >= p * HEAD_DIM) & (lane < (p + 1) * HEAD_DIM), q, zero)
         for p in range(n_parts)], axis=0)


def _sb_kernel(q_ref, k_ref, v_ref, o_ref, qs_ref, acc_ref, run_ref, *, tq, n_grp, n_seg):
    first_tile = pl.program_id(2) * n_seg
    m2 = 2 * tq
    r2 = lax.broadcasted_iota(jnp.int32, (tq, tq), 0)
    c2 = lax.broadcasted_iota(jnp.int32, (tq, tq), 1)
    tri = jnp.where(r2 > c2, 1.0, 0.0).astype(BF16)
    lanes = lambda g: slice(g * LANES, (g + 1) * LANES)
    rows = lambda seg: slice(seg * tq, (seg + 1) * tq)

    def block(seg, g, blk, run, diagonal=False, gate=None):
        ks = pl.multiple_of(blk * tq, tq)
        z = _dot_nt(qs_ref[seg, g], k_ref[pl.ds(ks, tq), lanes(g)])
        if diagonal:
            row = lax.broadcasted_iota(jnp.int32, (m2, tq), 0)
            col = lax.broadcasted_iota(jnp.int32, (m2, tq), 1)
            z = jnp.where(col < jnp.where(row >= tq, row - tq, row), z, NEG_INF)
        if gate is not None:
            z = jnp.where(gate > 0.0, z, NEG_INF)
        p = jnp.maximum(z, jnp.log2(1.0 + jnp.exp2(jnp.minimum(z, SB_EXP2_MAX))))
        after = _dot(p.astype(BF16), tri)
        spent = after + jnp.concatenate([run] * (tq // LANES), axis=1)
        run = run + jnp.broadcast_to(after[:, 0:1] + p[:, 0:1], (m2, LANES))
        a = jnp.exp2((z - p) - spent)
        return run, _dot(a.astype(BF16), v_ref[pl.ds(ks, tq), lanes(g)])

    zeros = jnp.zeros((m2, LANES), F32)
    for seg in range(n_seg):
        i = first_tile + seg
        for g in range(n_grp):
            qs_ref[seg, g] = _stack_masked(q_ref[rows(seg), lanes(g)], 2)
            run, out = block(seg, g, i, zeros, diagonal=True)
            if seg == 0:
                run, out2 = block(seg, g, jnp.maximum(i - 1, 0), run,
                                  gate=jnp.where(i > 0, 1.0, 0.0).astype(F32))
            else:
                run, out2 = block(seg, g, i - 1, run)
            run_ref[seg, g] = run
            acc_ref[seg, g] = out + out2

    def more(carry):
        blk, least = carry
        return jnp.logical_and(blk >= 0, least < SB_STOP_LOG2)

    lane = lax.broadcasted_iota(jnp.int32, (tq, LANES), 1)
    for seg in range(n_seg):
        def step(carry, seg=seg):
            blk, _ = carry
            least = None
            for g in range(n_grp):
                run, out = block(seg, g, blk, run_ref[seg, g])
                run_ref[seg, g] = run
                acc_ref[seg, g] += out
                least = jnp.min(run) if least is None else jnp.minimum(least, jnp.min(run))
            return blk - 1, least

        lax.while_loop(more, step, (first_tile + seg - 2, jnp.min(run_ref[seg])))
        for g in range(n_grp):
            o_ref[rows(seg), lanes(g)] = jnp.where(
                lane < HEAD_DIM, acc_ref[seg, g, 0:tq, :], acc_ref[seg, g, tq:m2, :]
            ).astype(o_ref.dtype)


def _da_flat_kernel(q_ref, k_ref, v_ref, lq1_ref, lk1_ref, lq2_ref, lk2_ref, gsub_ref, o_ref,
                    vext_ref, qs_ref, s_ref, mc_ref, m_ref, l_ref, acc_ref, *, tq, n_tiles,
                    lambda_init):
    m2 = 2 * tq
    reps = tq // LANES
    vext_ref[:, 0:LANES] = v_ref[...]
    vext_ref[:, LANES:2 * LANES] = jnp.ones(v_ref.shape, v_ref.dtype)
    lam = (jnp.exp(jnp.sum(lq1_ref[...] * lk1_ref[...], axis=-1, keepdims=True))
           - jnp.exp(jnp.sum(lq2_ref[...] * lk2_ref[...], axis=-1, keepdims=True))
           + lambda_init)
    steps = [(i, blk) for i in range(n_tiles) for blk in range(i + 1)]
    rows = lambda j: slice(j * tq, (j + 1) * tq)

    def scores(t):
        i, blk = steps[t]
        if blk == 0:
            qs_ref[i % 2] = _stack_masked(q_ref[rows(i), :], 2)
        s = _dot_nt(qs_ref[i % 2], k_ref[rows(blk), :])
        s_ref[t % 2] = s
        mc_ref[t % 2] = jnp.broadcast_to(jnp.max(s, axis=1, keepdims=True), (m2, LANES))

    def absorb(t):
        i, blk = steps[t]
        s = s_ref[t % 2]
        if blk == i:
            row = lax.broadcasted_iota(jnp.int32, (m2, tq), 0)
            col = lax.broadcasted_iota(jnp.int32, (m2, tq), 1)
            qrow = jnp.where(row >= tq, row - tq, row)
            s = jnp.where((col // CHUNK) <= (qrow // CHUNK), s, NEG_INF)
            m_cur = jnp.broadcast_to(jnp.max(s, axis=1, keepdims=True), (m2, LANES))
        else:
            m_cur = mc_ref[t % 2]
        m_new = m_cur if blk == 0 else jnp.maximum(m_ref[...], m_cur)
        p = jnp.exp2(s - jnp.concatenate([m_new] * reps, axis=1))
        pv = _dot(p.astype(BF16), vext_ref[rows(blk), :])
        if blk == 0:
            acc, l = pv[:, :LANES], pv[:, LANES:]
        else:
            alpha = jnp.exp2(m_ref[...] - m_new)
            acc = alpha * acc_ref[...] + pv[:, :LANES]
            l = alpha * l_ref[...] + pv[:, LANES:]
        if blk < i:
            acc_ref[...], l_ref[...], m_ref[...] = acc, l, m_new
        else:
            res = acc / l
            o = res[0:tq, :] - lam * res[tq:m2, :]
            ms = jnp.mean(o * o, axis=-1, keepdims=True)
            o = o * lax.rsqrt(ms + EPS) * gsub_ref[...] * (1.0 - lambda_init)
            o_ref[rows(i), :] = o.astype(o_ref.dtype)

    scores(0)
    for t in range(len(steps)):
        if t + 1 < len(steps):
            scores(t + 1)
        absorb(t)


def _post_kernel(x_ref, oa_ref, ob_ref, gmix_ref, wgate_ref, bgate_ref, wa_ref, wb_ref, wout_ref,
                 gffn_ref, wfg_ref, wfu_ref, wfd_ref, out_ref, *, d_model):
    x = x_ref[...]
    ms = jnp.mean(x * x, axis=-1, keepdims=True)
    h = (x * lax.rsqrt(ms + EPS) * gmix_ref[...]).astype(BF16)
    gates = jax.nn.sigmoid(_dot(h, wgate_ref[...]) + bgate_ref[...])
    br_a = _dot(oa_ref[...], wa_ref[...])
    br_b = _dot(ob_ref[...], wb_ref[...])
    merged = gates[:, :d_model] * br_a + gates[:, d_model:] * br_b
    x1 = x + _dot(merged.astype(BF16), wout_ref[...])

    ms2 = jnp.mean(x1 * x1, axis=-1, keepdims=True)
    h2 = (x1 * lax.rsqrt(ms2 + EPS) * gffn_ref[...]).astype(BF16)
    fg = _dot(h2, wfg_ref[...])
    fu = _dot(h2, wfu_ref[...])
    ff = (fg * jax.nn.sigmoid(fg)) * fu
    out_ref[...] = x1 + _dot(ff.astype(BF16), wfd_ref[...])


def _rope_tables(seq):
    half = ROPE_DIM // 2
    pos = np.arange(seq, dtype=np.float64)
    inv_freq = ROPE_THETA ** (-np.arange(0, ROPE_DIM, 2, dtype=np.float64) / ROPE_DIM)
    ang = pos[:, None] * inv_freq[None, :]
    cos, sin = np.cos(ang), np.sin(ang)
    ones = np.ones((seq, HEAD_DIM - ROPE_DIM))
    zeros = np.zeros((seq, HEAD_DIM - ROPE_DIM))
    zh = np.zeros((seq, half))
    ra = np.concatenate([cos, cos, ones], axis=1)
    rm = np.concatenate([-sin, zh, zeros], axis=1)
    rp = np.concatenate([zh, sin, zeros], axis=1)
    tile = lambda t: jnp.asarray(np.tile(t, (1, LANES // HEAD_DIM)), dtype=F32)
    return tile(ra), tile(rm), tile(rp)


def _tiles(seq):
    tm, tp, tq_sb, tq_da = 1024, 512, 256, 512
    for t in (tm, tp, tq_sb, tq_da):
        assert seq % t == 0 and t % LANES == 0 and t % CHUNK == 0
    return tm, tp, tq_sb, tq_da


def kernel(x, g_mix, w_in, g_q, g_k, lam_q1, lam_k1, lam_q2, lam_k2, g_sub, w_branch_a, w_branch_b,
           w_gate, b_gate, w_out, g_ffn, w_ffn_gate, w_ffn_up, w_ffn_down):
    b, s, d = x.shape
    depth = g_mix.shape[0]
    sb_w = SB_HEADS * HEAD_DIM
    da_w = DA_HEADS * 2 * HEAD_DIM
    in_w = 3 * sb_w + 3 * da_w
    n = b * s
    d_ff = w_ffn_gate.shape[-1]

    tm, tp, tq_sb, tq_da = _tiles(s)

    ra, rm, rp = _rope_tables(s)
    grp = np.arange(da_w) // HEAD_DIM
    gmat = jnp.asarray(np.where(grp[:, None] == grp[None, :], 1.0 / HEAD_DIM, 0.0), dtype=BF16)
    tile_g = lambda g: jnp.tile(g, da_w // HEAD_DIM)[None, :]
    vmem_full = pl.BlockSpec(memory_space=pltpu.VMEM)
    spt = s // tm

    xf = x.reshape(n, d)
    for layer in range(depth):
        lambda_init = 0.8 - 0.6 * math.exp(-0.3 * layer)
        gmix = g_mix[layer][None, :]

        proj = pl.pallas_call(
            functools.partial(_proj_kernel, sb_w=sb_w, da_w=da_w),
            grid=(n // tm,),
            in_specs=[
                pl.BlockSpec((tm, d), lambda t: (t, 0)),
                vmem_full, vmem_full, vmem_full, vmem_full, vmem_full,
                pl.BlockSpec((tm, LANES), lambda t: (t % spt, 0)),
                pl.BlockSpec((tm, LANES), lambda t: (t % spt, 0)),
                pl.BlockSpec((tm, LANES), lambda t: (t % spt, 0)),
            ],
            out_specs=pl.BlockSpec((tm, in_w), lambda t: (t, 0)),
            out_shape=jax.ShapeDtypeStruct((n, in_w), BF16),
            compiler_params=pltpu.CompilerParams(
                dimension_semantics=("arbitrary",), vmem_limit_bytes=VMEM_LIMIT),
            name="proj",
        )(xf, gmix, w_in[layer].astype(BF16), tile_g(g_q[layer]), tile_g(g_k[layer]), gmat,
          ra, rm, rp)

        cb = lambda off: off // LANES
        tq, gw, ns = tq_sb, SB_GROUPS * LANES, SB_TILES
        nq = s // (ns * tq)
        o_a = pl.pallas_call(
            functools.partial(_sb_kernel, tq=tq, n_grp=SB_GROUPS, n_seg=ns),
            grid=(b, sb_w // gw, nq),
            in_specs=[
                pl.BlockSpec((ns * tq, gw), lambda bi, j, i, nq=nq: (bi * nq + i, j)),
                pl.BlockSpec((s, gw), lambda bi, j, i: (bi, sb_w // gw + j)),
                pl.BlockSpec((s, gw), lambda bi, j, i: (bi, 2 * sb_w // gw + j)),
            ],
            out_specs=pl.BlockSpec((ns * tq, gw), lambda bi, j, i, nq=nq: (bi * nq + i, j)),
            out_shape=jax.ShapeDtypeStruct((n, sb_w), BF16),
            scratch_shapes=[pltpu.VMEM((ns, SB_GROUPS, 2 * tq, LANES), BF16),
                            pltpu.VMEM((ns, SB_GROUPS, 2 * tq, LANES), F32),
                            pltpu.VMEM((ns, SB_GROUPS, 2 * tq, LANES), F32)],
            compiler_params=pltpu.CompilerParams(
                dimension_semantics=("arbitrary", "arbitrary", "arbitrary"),
                vmem_limit_bytes=VMEM_LIMIT),
            name="sb_attn",
        )(proj, proj, proj)

        tq = tq_da
        o_b = pl.pallas_call(
            functools.partial(_da_flat_kernel, tq=tq, n_tiles=s // tq, lambda_init=lambda_init),
            grid=(b, DA_HEADS),
            in_specs=[
                pl.BlockSpec((s, LANES), lambda bi, j: (bi, cb(3 * sb_w) + j)),
                pl.BlockSpec((s, LANES), lambda bi, j: (bi, cb(3 * sb_w + da_w) + j)),
                pl.BlockSpec((s, LANES), lambda bi, j: (bi, cb(3 * sb_w + 2 * da_w) + j)),
                vmem_full, vmem_full, vmem_full, vmem_full, vmem_full,
            ],
            out_specs=pl.BlockSpec((s, LANES), lambda bi, j: (bi, j)),
            out_shape=jax.ShapeDtypeStruct((n, da_w), BF16),
            scratch_shapes=[pltpu.VMEM((s, 2 * LANES), BF16),
                            pltpu.VMEM((2, 2 * tq, LANES), BF16),
                            pltpu.VMEM((2, 2 * tq, tq), F32),
                            pltpu.VMEM((2, 2 * tq, LANES), F32)]
            + [pltpu.VMEM((2 * tq, LANES), F32)] * 3,
            compiler_params=pltpu.CompilerParams(
                dimension_semantics=("arbitrary", "arbitrary"),
                vmem_limit_bytes=VMEM_LIMIT),
            name="da_attn",
        )(proj, proj, proj, lam_q1[layer][None, :], lam_k1[layer][None, :],
          lam_q2[layer][None, :], lam_k2[layer][None, :], g_sub[layer][None, :])

        xf = pl.pallas_call(
            functools.partial(_post_kernel, d_model=d),
            grid=(n // tp,),
            in_specs=[
                pl.BlockSpec((tp, d), lambda t: (t, 0)),
                pl.BlockSpec((tp, sb_w), lambda t: (t, 0)),
                pl.BlockSpec((tp, da_w), lambda t: (t, 0)),
            ] + [vmem_full] * 10,
            out_specs=pl.BlockSpec((tp, d), lambda t: (t, 0)),
            out_shape=jax.ShapeDtypeStruct((n, d), F32),
            compiler_params=pltpu.CompilerParams(
                dimension_semantics=("arbitrary",), vmem_limit_bytes=VMEM_LIMIT),
            name="post",
        )(xf, o_a, o_b, gmix, w_gate[layer].astype(BF16), b_gate[layer][None, :],
          w_branch_a[layer].astype(BF16), w_branch_b[layer].astype(BF16),
          w_out[layer].astype(BF16), g_ffn[layer][None, :],
          w_ffn_gate[layer].astype(BF16), w_ffn_up[layer].astype(BF16),
          w_ffn_down[layer].astype(BF16))
    return xf.reshape(b, s, d)
```

```python
import functools
import math

import jax
import jax.numpy as jnp
import numpy as np
from jax import lax
from jax.experimental import pallas as pl
from jax.experimental.pallas import tpu as pltpu

F32 = jnp.float32
BF16 = jnp.bfloat16

CHUNK = 64
SB_HEADS = 8
DA_HEADS = 4
HEAD_DIM = 64
ROPE_THETA = 500000.0
ROPE_DIM = HEAD_DIM // 4
EPS = 1e-6
NEG_INF = -1e30
LOG2E = math.log2(math.e)
SB_GROUPS = 4
SB_TILES = 4
SB_STOP_LOG2 = 150.0
SB_EXP2_MAX = 126.0
LANES = 128
VMEM_LIMIT = 56 * 1024 * 1024


def _dot(a, b):
    return jnp.dot(a, b, preferred_element_type=F32)


def _dot_nt(a, b):
    return lax.dot_general(a, b, (((1,), (1,)), ((), ())), preferred_element_type=F32)


def _proj_kernel(x_ref, gmix_ref, win_ref, gq_ref, gk_ref, gmat_ref,
                 ra_ref, rm_ref, rp_ref, out_ref, *, sb_w, da_w):
    x = x_ref[...]
    ms = jnp.mean(x * x, axis=-1, keepdims=True)
    h = x * lax.rsqrt(ms + EPS) * gmix_ref[...]
    proj = _dot(h.astype(BF16), win_ref[...])
    scale = HEAD_DIM ** -0.5

    out_ref[:, 0:sb_w] = (proj[:, 0:sb_w] * (scale * LOG2E)).astype(BF16)
    out_ref[:, sb_w:3 * sb_w] = proj[:, sb_w:3 * sb_w].astype(BF16)

    ra, rm, rp = ra_ref[...], rm_ref[...], rp_ref[...]

    def qk_norm_rope(t, g, mult):
        msq = _dot((t * t).astype(BF16), gmat_ref[...])
        tn = t * lax.rsqrt(msq + EPS) * g
        cols = []
        for j in range(da_w // LANES):
            c = tn[:, j * LANES:(j + 1) * LANES]
            r = (c * ra + pltpu.roll(c, LANES - ROPE_DIM // 2, 1) * rm
                 + pltpu.roll(c, ROPE_DIM // 2, 1) * rp)
            cols.append((r * mult).astype(BF16))
        return jnp.concatenate(cols, axis=1)

    o = 3 * sb_w
    out_ref[:, o:o + da_w] = qk_norm_rope(proj[:, o:o + da_w], gq_ref[...], scale * LOG2E)
    out_ref[:, o + da_w:o + 2 * da_w] = qk_norm_rope(proj[:, o + da_w:o + 2 * da_w], gk_ref[...], 1.0)
    out_ref[:, o + 2 * da_w:o + 3 * da_w] = proj[:, o + 2 * da_w:o + 3 * da_w].astype(BF16)


def _stack_masked(q, n_parts):
    lane = lax.broadcasted_iota(jnp.int32, q.shape, 1)
    zero = jnp.zeros_like(q)
    return jnp.concatenate(
        [jnp.where((lane >= p * HEAD_DIM) & (lane < (p + 1) * HEAD_DIM), q, zero)
         for p in range(n_parts)], axis=0)


def _sb_kernel(q_ref, k_ref, v_ref, o_ref, qs_ref, acc_ref, run_ref, *, tq, n_grp, n_seg):
    first_tile = pl.program_id(2) * n_seg
    m2 = 2 * tq
    r2 = lax.broadcasted_iota(jnp.int32, (tq, tq), 0)
    c2 = lax.broadcasted_iota(jnp.int32, (tq, tq), 1)
    tri = jnp.where(r2 > c2, 1.0, 0.0).astype(BF16)
    lanes = lambda g: slice(g * LANES, (g + 1) * LANES)
    rows = lambda seg: slice(seg * tq, (seg + 1) * tq)

    def block(seg, g, blk, run, diagonal=False, gate=None):
        ks = pl.multiple_of(blk * tq, tq)
        z = _dot_nt(qs_ref[seg, g], k_ref[pl.ds(ks, tq), lanes(g)])
        if diagonal:
            row = lax.broadcasted_iota(jnp.int32, (m2, tq), 0)
            col = lax.broadcasted_iota(jnp.int32, (m2, tq), 1)
            z = jnp.where(col < jnp.where(row >= tq, row - tq, row), z, NEG_INF)
        if gate is not None:
            z = jnp.where(gate > 0.0, z, NEG_INF)
        p = jnp.maximum(z, jnp.log2(1.0 + jnp.exp2(jnp.minimum(z, SB_EXP2_MAX))))
        after = _dot(p.astype(BF16), tri)
        spent = after + jnp.concatenate([run] * (tq // LANES), axis=1)
        run = run + jnp.broadcast_to(after[:, 0:1] + p[:, 0:1], (m2, LANES))
        a = jnp.exp2((z - p) - spent)
        return run, _dot(a.astype(BF16), v_ref[pl.ds(ks, tq), lanes(g)])

    zeros = jnp.zeros((m2, LANES), F32)
    for seg in range(n_seg):
        i = first_tile + seg
        for g in range(n_grp):
            qs_ref[seg, g] = _stack_masked(q_ref[rows(seg), lanes(g)], 2)
            run, out = block(seg, g, i, zeros, diagonal=True)
            if seg == 0:
                run, out2 = block(seg, g, jnp.maximum(i - 1, 0), run,
                                  gate=jnp.where(i > 0, 1.0, 0.0).astype(F32))
            else:
                run, out2 = block(seg, g, i - 1, run)
            run_ref[seg, g] = run
            acc_ref[seg, g] = out + out2

    def more(carry):
        blk, least = carry
        return jnp.logical_and(blk >= 0, least < SB_STOP_LOG2)

    lane = lax.broadcasted_iota(jnp.int32, (tq, LANES), 1)
    for seg in range(n_seg):
        def step(carry, seg=seg):
            blk, _ = carry
            least = None
            for g in range(n_grp):
                run, out = block(seg, g, blk, run_ref[seg, g])
                run_ref[seg, g] = run
                acc_ref[seg, g] += out
                least = jnp.min(run) if least is None else jnp.minimum(least, jnp.min(run))
            return blk - 1, least

        lax.while_loop(more, step, (first_tile + seg - 2, jnp.min(run_ref[seg])))
        for g in range(n_grp):
            o_ref[rows(seg), lanes(g)] = jnp.where(
                lane < HEAD_DIM, acc_ref[seg, g, 0:tq, :], acc_ref[seg, g, tq:m2, :]
            ).astype(o_ref.dtype)


def _da_flat_kernel(q_ref, k_ref, v_ref, lq1_ref, lk1_ref, lq2_ref, lk2_ref, gsub_ref, o_ref,
                    vext_ref, qs_ref, s_ref, mc_ref, m_ref, l_ref, acc_ref, *, tq, n_tiles,
                    lambda_init):
    m2 = 2 * tq
    vext_ref[:, 0:LANES] = v_ref[...]
    vext_ref[:, LANES:2 * LANES] = jnp.ones(v_ref.shape, v_ref.dtype)
    lam = (jnp.exp(jnp.sum(lq1_ref[...] * lk1_ref[...], axis=-1, keepdims=True))
           - jnp.exp(jnp.sum(lq2_ref[...] * lk2_ref[...], axis=-1, keepdims=True))
           + lambda_init)
    steps = [(i, blk) for i in range(n_tiles) for blk in range(i + 1)]
    rows = lambda j: slice(j * tq, (j + 1) * tq)

    h = tq // 2
    quarter = (slice(0, h), slice(h, tq), slice(tq, tq + h), slice(tq + h, m2))
    upper = lambda x: jnp.concatenate([x[quarter[0]], x[quarter[2]]], axis=0)
    lower = lambda x: jnp.concatenate([x[quarter[1]], x[quarter[3]]], axis=0)
    rowmax = lambda s: jnp.broadcast_to(jnp.max(s, axis=1, keepdims=True), (s.shape[0], LANES))

    def scores(t):
        i, blk = steps[t]
        if blk == 0:
            qs_ref[i % 2] = _stack_masked(q_ref[rows(i), :], 2)
        q = qs_ref[i % 2]
        if blk < i:
            s = _dot_nt(q, k_ref[rows(blk), :])
            s_ref[t % 2] = s
            mc_ref[t % 2] = rowmax(s)
        else:
            k0 = blk * tq
            s_ref[t % 2, :, 0:h] = _dot_nt(q, k_ref[k0:k0 + h, :])
            s2 = _dot_nt(lower(q), k_ref[k0 + h:k0 + tq, :])
            s_ref[t % 2, quarter[1], h:tq] = s2[0:h]
            s_ref[t % 2, quarter[3], h:tq] = s2[h:tq]

    def update(s, m_cur, v, prev):
        m_new = m_cur if prev is None else jnp.maximum(prev[0], m_cur)
        p = jnp.exp2(s - jnp.concatenate([m_new] * (s.shape[1] // LANES), axis=1))
        pv = _dot(p.astype(BF16), v)
        if prev is None:
            return m_new, pv[:, LANES:], pv[:, :LANES]
        alpha = jnp.exp2(prev[0] - m_new)
        return m_new, alpha * prev[1] + pv[:, LANES:], alpha * prev[2] + pv[:, :LANES]

    def absorb(t):
        i, blk = steps[t]
        prev = None if blk == 0 else (m_ref[...], l_ref[...], acc_ref[...])
        if blk < i:
            m_ref[...], l_ref[...], acc_ref[...] = update(
                s_ref[t % 2], mc_ref[t % 2], vext_ref[rows(blk), :], prev)
            return
        k0 = blk * tq
        row = lax.broadcasted_iota(jnp.int32, (tq, h), 0)
        col = lax.broadcasted_iota(jnp.int32, (tq, h), 1)
        seen = (col // CHUNK) <= (jnp.where(row >= h, row - h, row) // CHUNK)
        s1 = s_ref[t % 2, :, 0:h]
        s_up = jnp.where(seen, upper(s1), NEG_INF)
        s2 = jnp.concatenate([s_ref[t % 2, quarter[1], h:tq], s_ref[t % 2, quarter[3], h:tq]],
                             axis=0)
        s_lo = jnp.concatenate([lower(s1), jnp.where(seen, s2, NEG_INF)], axis=1)
        _, l_up, acc_up = update(s_up, rowmax(s_up), vext_ref[k0:k0 + h, :],
                                 None if prev is None else tuple(upper(x) for x in prev))
        _, l_lo, acc_lo = update(s_lo, rowmax(s_lo), vext_ref[k0:k0 + tq, :],
                                 None if prev is None else tuple(lower(x) for x in prev))
        res_up, res_lo = acc_up / l_up, acc_lo / l_lo
        res = [jnp.concatenate([res_up[c * h:(c + 1) * h], res_lo[c * h:(c + 1) * h]], axis=0)
               for c in range(2)]
        o = res[0] - lam * res[1]
        ms = jnp.mean(o * o, axis=-1, keepdims=True)
        o = o * lax.rsqrt(ms + EPS) * gsub_ref[...] * (1.0 - lambda_init)
        o_ref[rows(i), :] = o.astype(o_ref.dtype)

    scores(0)
    for t in range(len(steps)):
        if t + 1 < len(steps):
            scores(t + 1)
        absorb(t)


def _post_kernel(x_ref, oa_ref, ob_ref, gmix_ref, wgate_ref, bgate_ref, wa_ref, wb_ref, wout_ref,
                 gffn_ref, wfg_ref, wfu_ref, wfd_ref, out_ref, *, d_model):
    x = x_ref[...]
    ms = jnp.mean(x * x, axis=-1, keepdims=True)
    h = (x * lax.rsqrt(ms + EPS) * gmix_ref[...]).astype(BF16)
    gates = jax.nn.sigmoid(_dot(h, wgate_ref[...]) + bgate_ref[...])
    br_a = _dot(oa_ref[...], wa_ref[...])
    br_b = _dot(ob_ref[...], wb_ref[...])
    merged = gates[:, :d_model] * br_a + gates[:, d_model:] * br_b
    x1 = x + _dot(merged.astype(BF16), wout_ref[...])

    ms2 = jnp.mean(x1 * x1, axis=-1, keepdims=True)
    h2 = (x1 * lax.rsqrt(ms2 + EPS) * gffn_ref[...]).astype(BF16)
    fg = _dot(h2, wfg_ref[...])
    fu = _dot(h2, wfu_ref[...])
    ff = (fg * jax.nn.sigmoid(fg)) * fu
    out_ref[...] = x1 + _dot(ff.astype(BF16), wfd_ref[...])


def _rope_tables(seq):
    half = ROPE_DIM // 2
    pos = np.arange(seq, dtype=np.float64)
    inv_freq = ROPE_THETA ** (-np.arange(0, ROPE_DIM, 2, dtype=np.float64) / ROPE_DIM)
    ang = pos[:, None] * inv_freq[None, :]
    cos, sin = np.cos(ang), np.sin(ang)
    ones = np.ones((seq, HEAD_DIM - ROPE_DIM))
    zeros = np.zeros((seq, HEAD_DIM - ROPE_DIM))
    zh = np.zeros((seq, half))
    ra = np.concatenate([cos, cos, ones], axis=1)
    rm = np.concatenate([-sin, zh, zeros], axis=1)
    rp = np.concatenate([zh, sin, zeros], axis=1)
    tile = lambda t: jnp.asarray(np.tile(t, (1, LANES // HEAD_DIM)), dtype=F32)
    return tile(ra), tile(rm), tile(rp)


def _tiles(seq):
    tm, tp, tq_sb, tq_da = 1024, 512, 256, 512
    for t in (tm, tp, tq_sb, tq_da):
        assert seq % t == 0 and t % LANES == 0 and t % CHUNK == 0
    return tm, tp, tq_sb, tq_da


def kernel(x, g_mix, w_in, g_q, g_k, lam_q1, lam_k1, lam_q2, lam_k2, g_sub, w_branch_a, w_branch_b,
           w_gate, b_gate, w_out, g_ffn, w_ffn_gate, w_ffn_up, w_ffn_down):
    b, s, d = x.shape
    depth = g_mix.shape[0]
    sb_w = SB_HEADS * HEAD_DIM
    da_w = DA_HEADS * 2 * HEAD_DIM
    in_w = 3 * sb_w + 3 * da_w
    n = b * s
    d_ff = w_ffn_gate.shape[-1]

    tm, tp, tq_sb, tq_da = _tiles(s)

    ra, rm, rp = _rope_tables(s)
    grp = np.arange(da_w) // HEAD_DIM
    gmat = jnp.asarray(np.where(grp[:, None] == grp[None, :], 1.0 / HEAD_DIM, 0.0), dtype=BF16)
    tile_g = lambda g: jnp.tile(g, da_w // HEAD_DIM)[None, :]
    vmem_full = pl.BlockSpec(memory_space=pltpu.VMEM)
    spt = s // tm

    xf = x.reshape(n, d)
    for layer in range(depth):
        lambda_init = 0.8 - 0.6 * math.exp(-0.3 * layer)
        gmix = g_mix[layer][None, :]

        proj = pl.pallas_call(
            functools.partial(_proj_kernel, sb_w=sb_w, da_w=da_w),
            grid=(n // tm,),
            in_specs=[
                pl.BlockSpec((tm, d), lambda t: (t, 0)),
                vmem_full, vmem_full, vmem_full, vmem_full, vmem_full,
                pl.BlockSpec((tm, LANES), lambda t: (t % spt, 0)),
                pl.BlockSpec((tm, LANES), lambda t: (t % spt, 0)),
                pl.BlockSpec((tm, LANES), lambda t: (t % spt, 0)),
            ],
            out_specs=pl.BlockSpec((tm, in_w), lambda t: (t, 0)),
            out_shape=jax.ShapeDtypeStruct((n, in_w), BF16),
            compiler_params=pltpu.CompilerParams(
                dimension_semantics=("arbitrary",), vmem_limit_bytes=VMEM_LIMIT),
            name="proj",
        )(xf, gmix, w_in[layer].astype(BF16), tile_g(g_q[layer]), tile_g(g_k[layer]), gmat,
          ra, rm, rp)

        cb = lambda off: off // LANES
        tq, gw, ns = tq_sb, SB_GROUPS * LANES, SB_TILES
        nq = s // (ns * tq)
        o_a = pl.pallas_call(
            functools.partial(_sb_kernel, tq=tq, n_grp=SB_GROUPS, n_seg=ns),
            grid=(b, sb_w // gw, nq),
            in_specs=[
                pl.BlockSpec((ns * tq, gw), lambda bi, j, i, nq=nq: (bi * nq + i, j)),
                pl.BlockSpec((s, gw), lambda bi, j, i: (bi, sb_w // gw + j)),
                pl.BlockSpec((s, gw), lambda bi, j, i: (bi, 2 * sb_w // gw + j)),
            ],
            out_specs=pl.BlockSpec((ns * tq, gw), lambda bi, j, i, nq=nq: (bi * nq + i, j)),
            out_shape=jax.ShapeDtypeStruct((n, sb_w), BF16),
            scratch_shapes=[pltpu.VMEM((ns, SB_GROUPS, 2 * tq, LANES), BF16),
                            pltpu.VMEM((ns, SB_GROUPS, 2 * tq, LANES), F32),
                            pltpu.VMEM((ns, SB_GROUPS, 2 * tq, LANES), F32)],
            compiler_params=pltpu.CompilerParams(
                dimension_semantics=("arbitrary", "arbitrary", "arbitrary"),
                vmem_limit_bytes=VMEM_LIMIT),
            name="sb_attn",
        )(proj, proj, proj)

        tq = tq_da
        o_b = pl.pallas_call(
            functools.partial(_da_flat_kernel, tq=tq, n_tiles=s // tq, lambda_init=lambda_init),
            grid=(b, DA_HEADS),
            in_specs=[
                pl.BlockSpec((s, LANES), lambda bi, j: (bi, cb(3 * sb_w) + j)),
                pl.BlockSpec((s, LANES), lambda bi, j: (bi, cb(3 * sb_w + da_w) + j)),
                pl.BlockSpec((s, LANES), lambda bi, j: (bi, cb(3 * sb_w + 2 * da_w) + j)),
                vmem_full, vmem_full, vmem_full, vmem_full, vmem_full,
            ],
            out_specs=pl.BlockSpec((s, LANES), lambda bi, j: (bi, j)),
            out_shape=jax.ShapeDtypeStruct((n, da_w), BF16),
            scratch_shapes=[pltpu.VMEM((s, 2 * LANES), BF16),
                            pltpu.VMEM((2, 2 * tq, LANES), BF16),
                            pltpu.VMEM((2, 2 * tq, tq), F32),
                            pltpu.VMEM((2, 2 * tq, LANES), F32)]
            + [pltpu.VMEM((2 * tq, LANES), F32)] * 3,
            compiler_params=pltpu.CompilerParams(
                dimension_semantics=("arbitrary", "arbitrary"),
                vmem_limit_bytes=VMEM_LIMIT),
            name="da_attn",
        )(proj, proj, proj, lam_q1[layer][None, :], lam_k1[layer][None, :],
          lam_q2[layer][None, :], lam_k2[layer][None, :], g_sub[layer][None, :])

        xf = pl.pallas_call(
            functools.partial(_post_kernel, d_model=d),
            grid=(n // tp,),
            in_specs=[
                pl.BlockSpec((tp, d), lambda t: (t, 0)),
                pl.BlockSpec((tp, sb_w), lambda t: (t, 0)),
                pl.BlockSpec((tp, da_w), lambda t: (t, 0)),
            ] + [vmem_full] * 10,
            out_specs=pl.BlockSpec((tp, d), lambda t: (t, 0)),
            out_shape=jax.ShapeDtypeStruct((n, d), F32),
            compiler_params=pltpu.CompilerParams(
                dimension_semantics=("arbitrary",), vmem_limit_bytes=VMEM_LIMIT),
            name="post",
        )(xf, o_a, o_b, gmix, w_gate[layer].astype(BF16), b_gate[layer][None, :],
          w_branch_a[layer].astype(BF16), w_branch_b[layer].astype(BF16),
          w_out[layer].astype(BF16), g_ffn[layer][None, :],
          w_ffn_gate[layer].astype(BF16), w_ffn_up[layer].astype(BF16),
          w_ffn_down[layer].astype(BF16))
    return xf.reshape(b, s, d)
```

```python
import functools
import math

import jax
import jax.numpy as jnp
import numpy as np
from jax import lax
from jax.experimental import pallas as pl
from jax.experimental.pallas import tpu as pltpu

F32 = jnp.float32
BF16 = jnp.bfloat16

CHUNK = 64
SB_HEADS = 8
DA_HEADS = 4
HEAD_DIM = 64
ROPE_THETA = 500000.0
ROPE_DIM = HEAD_DIM // 4
EPS = 1e-6
NEG_INF = -1e30
LOG2E = math.log2(math.e)
SB_GROUPS = 4
SB_TILES = 4
SB_STOP_LOG2 = 150.0
SB_EXP2_MAX = 126.0
LANES = 128
VMEM_LIMIT = 56 * 1024 * 1024


def _dot(a, b):
    return jnp.dot(a, b, preferred_element_type=F32)


def _dot_nt(a, b):
    return lax.dot_general(a, b, (((1,), (1,)), ((), ())), preferred_element_type=F32)


def _proj_kernel(x_ref, gmix_ref, win_ref, gq_ref, gk_ref, gmat_ref,
                 ra_ref, rm_ref, rp_ref, out_ref, *, sb_w, da_w):
    x = x_ref[...]
    ms = jnp.mean(x * x, axis=-1, keepdims=True)
    h = x * lax.rsqrt(ms + EPS) * gmix_ref[...]
    proj = _dot(h.astype(BF16), win_ref[...])
    scale = HEAD_DIM ** -0.5

    out_ref[:, 0:sb_w] = (proj[:, 0:sb_w] * (scale * LOG2E)).astype(BF16)
    out_ref[:, sb_w:3 * sb_w] = proj[:, sb_w:3 * sb_w].astype(BF16)

    ra, rm, rp = ra_ref[...], rm_ref[...], rp_ref[...]

    def qk_norm_rope(t, g, mult):
        msq = _dot((t * t).astype(BF16), gmat_ref[...])
        tn = t * lax.rsqrt(msq + EPS) * g
        cols = []
        for j in range(da_w // LANES):
            c = tn[:, j * LANES:(j + 1) * LANES]
            r = (c * ra + pltpu.roll(c, LANES - ROPE_DIM // 2, 1) * rm
                 + pltpu.roll(c, ROPE_DIM // 2, 1) * rp)
            cols.append((r * mult).astype(BF16))
        return jnp.concatenate(cols, axis=1)

    o = 3 * sb_w
    out_ref[:, o:o + da_w] = qk_norm_rope(proj[:, o:o + da_w], gq_ref[...], scale * LOG2E)
    out_ref[:, o + da_w:o + 2 * da_w] = qk_norm_rope(proj[:, o + da_w:o + 2 * da_w], gk_ref[...], 1.0)
    out_ref[:, o + 2 * da_w:o + 3 * da_w] = proj[:, o + 2 * da_w:o + 3 * da_w].astype(BF16)


def _stack_masked(q, n_parts):
    lane = lax.broadcasted_iota(jnp.int32, q.shape, 1)
    zero = jnp.zeros_like(q)
    return jnp.concatenate(
        [jnp.where((lane >= p * HEAD_DIM) & (lane < (p + 1) * HEAD_DIM), q, zero)
         for p in range(n_parts)], axis=0)


def _sb_kernel(q_ref, k_ref, v_ref, o_ref, qs_ref, acc_ref, run_ref, *, tq, n_grp, n_seg):
    first_tile = pl.program_id(2) * n_seg
    m2 = 2 * tq
    r2 = lax.broadcasted_iota(jnp.int32, (tq, tq), 0)
    c2 = lax.broadcasted_iota(jnp.int32, (tq, tq), 1)
    tri = jnp.where(r2 > c2, 1.0, 0.0).astype(BF16)
    lanes = lambda g: slice(g * LANES, (g + 1) * LANES)
    rows = lambda seg: slice(seg * tq, (seg + 1) * tq)

    def block(seg, g, blk, run, diagonal=False, gate=None):
        ks = pl.multiple_of(blk * tq, tq)
        z = _dot_nt(qs_ref[seg, g], k_ref[pl.ds(ks, tq), lanes(g)])
        if diagonal:
            row = lax.broadcasted_iota(jnp.int32, (m2, tq), 0)
            col = lax.broadcasted_iota(jnp.int32, (m2, tq), 1)
            z = jnp.where(col < jnp.where(row >= tq, row - tq, row), z, NEG_INF)
        if gate is not None:
            z = jnp.where(gate > 0.0, z, NEG_INF)
        p = jnp.maximum(z, jnp.log2(1.0 + jnp.exp2(jnp.minimum(z, SB_EXP2_MAX))))
        after = _dot(p.astype(BF16), tri)
        spent = after + jnp.concatenate([run] * (tq // LANES), axis=1)
        run = run + jnp.broadcast_to(after[:, 0:1] + p[:, 0:1], (m2, LANES))
        a = jnp.exp2((z - p) - spent)
        return run, _dot(a.astype(BF16), v_ref[pl.ds(ks, tq), lanes(g)])

    zeros = jnp.zeros((m2, LANES), F32)
    for seg in range(n_seg):
        i = first_tile + seg
        for g in range(n_grp):
            qs_ref[seg, g] = _stack_masked(q_ref[rows(seg), lanes(g)], 2)
            run, out = block(seg, g, i, zeros, diagonal=True)
            if seg == 0:
                run, out2 = block(seg, g, jnp.maximum(i - 1, 0), run,
                                  gate=jnp.where(i > 0, 1.0, 0.0).astype(F32))
            else:
                run, out2 = block(seg, g, i - 1, run)
            run_ref[seg, g] = run
            acc_ref[seg, g] = out + out2

    def more(carry):
        blk, least = carry
        return jnp.logical_and(blk >= 0, least < SB_STOP_LOG2)

    lane = lax.broadcasted_iota(jnp.int32, (tq, LANES), 1)
    for seg in range(n_seg):
        def step(carry, seg=seg):
            blk, _ = carry
            least = None
            for g in range(n_grp):
                run, out = block(seg, g, blk, run_ref[seg, g])
                run_ref[seg, g] = run
                acc_ref[seg, g] += out
                least = jnp.min(run) if least is None else jnp.minimum(least, jnp.min(run))
            return blk - 1, least

        lax.while_loop(more, step, (first_tile + seg - 2, jnp.min(run_ref[seg])))
        for g in range(n_grp):
            o_ref[rows(seg), lanes(g)] = jnp.where(
                lane < HEAD_DIM, acc_ref[seg, g, 0:tq, :], acc_ref[seg, g, tq:m2, :]
            ).astype(o_ref.dtype)


def _da_flat_kernel(q_ref, k_ref, v_ref, lq1_ref, lk1_ref, lq2_ref, lk2_ref, gsub_ref, *refs,
                    tq, n_tiles, n_cast, lambda_init):
    w32_refs, o_ref, w16_refs = refs[:n_cast], refs[n_cast], refs[n_cast + 1:2 * n_cast + 1]
    vext_ref, qs_ref, s_ref, mc_ref, m_ref, l_ref, acc_ref = refs[2 * n_cast + 1:]
    for w32, w16 in zip(w32_refs, w16_refs):
        w16[...] = w32[...].astype(w16.dtype)

    m2 = 2 * tq
    vext_ref[:, 0:LANES] = v_ref[...]
    vext_ref[:, LANES:2 * LANES] = jnp.ones(v_ref.shape, v_ref.dtype)
    lam = (jnp.exp(jnp.sum(lq1_ref[...] * lk1_ref[...], axis=-1, keepdims=True))
           - jnp.exp(jnp.sum(lq2_ref[...] * lk2_ref[...], axis=-1, keepdims=True))
           + lambda_init)
    steps = [(i, blk) for i in range(n_tiles) for blk in range(i + 1)]
    rows = lambda j: slice(j * tq, (j + 1) * tq)

    h = tq // 2
    quarter = (slice(0, h), slice(h, tq), slice(tq, tq + h), slice(tq + h, m2))
    upper = lambda x: jnp.concatenate([x[quarter[0]], x[quarter[2]]], axis=0)
    lower = lambda x: jnp.concatenate([x[quarter[1]], x[quarter[3]]], axis=0)
    rowmax = lambda s: jnp.broadcast_to(jnp.max(s, axis=1, keepdims=True), (s.shape[0], LANES))

    def scores(t):
        i, blk = steps[t]
        if blk == 0:
            qs_ref[i % 2] = _stack_masked(q_ref[rows(i), :], 2)
        q = qs_ref[i % 2]
        if blk < i:
            s = _dot_nt(q, k_ref[rows(blk), :])
            s_ref[t % 2] = s
            mc_ref[t % 2] = rowmax(s)
        else:
            k0 = blk * tq
            s_ref[t % 2, :, 0:h] = _dot_nt(q, k_ref[k0:k0 + h, :])
            s2 = _dot_nt(lower(q), k_ref[k0 + h:k0 + tq, :])
            s_ref[t % 2, quarter[1], h:tq] = s2[0:h]
            s_ref[t % 2, quarter[3], h:tq] = s2[h:tq]

    def update(s, m_cur, v, prev):
        m_new = m_cur if prev is None else jnp.maximum(prev[0], m_cur)
        p = jnp.exp2(s - jnp.concatenate([m_new] * (s.shape[1] // LANES), axis=1))
        pv = _dot(p.astype(BF16), v)
        if prev is None:
            return m_new, pv[:, LANES:], pv[:, :LANES]
        alpha = jnp.exp2(prev[0] - m_new)
        return m_new, alpha * prev[1] + pv[:, LANES:], alpha * prev[2] + pv[:, :LANES]

    def absorb(t):
        i, blk = steps[t]
        prev = None if blk == 0 else (m_ref[...], l_ref[...], acc_ref[...])
        if blk < i:
            m_ref[...], l_ref[...], acc_ref[...] = update(
                s_ref[t % 2], mc_ref[t % 2], vext_ref[rows(blk), :], prev)
            return
        k0 = blk * tq
        row = lax.broadcasted_iota(jnp.int32, (tq, h), 0)
        col = lax.broadcasted_iota(jnp.int32, (tq, h), 1)
        seen = (col // CHUNK) <= (jnp.where(row >= h, row - h, row) // CHUNK)
        s1 = s_ref[t % 2, :, 0:h]
        s_up = jnp.where(seen, upper(s1), NEG_INF)
        s2 = jnp.concatenate([s_ref[t % 2, quarter[1], h:tq], s_ref[t % 2, quarter[3], h:tq]],
                             axis=0)
        s_lo = jnp.concatenate([lower(s1), jnp.where(seen, s2, NEG_INF)], axis=1)
        _, l_up, acc_up = update(s_up, rowmax(s_up), vext_ref[k0:k0 + h, :],
                                 None if prev is None else tuple(upper(x) for x in prev))
        _, l_lo, acc_lo = update(s_lo, rowmax(s_lo), vext_ref[k0:k0 + tq, :],
                                 None if prev is None else tuple(lower(x) for x in prev))
        res_up, res_lo = acc_up / l_up, acc_lo / l_lo
        res = [jnp.concatenate([res_up[c * h:(c + 1) * h], res_lo[c * h:(c + 1) * h]], axis=0)
               for c in range(2)]
        o = res[0] - lam * res[1]
        ms = jnp.mean(o * o, axis=-1, keepdims=True)
        o = o * lax.rsqrt(ms + EPS) * gsub_ref[...] * (1.0 - lambda_init)
        o_ref[rows(i), :] = o.astype(o_ref.dtype)

    scores(0)
    for t in range(len(steps)):
        if t + 1 < len(steps):
            scores(t + 1)
        absorb(t)


def _post_kernel(x_ref, oa_ref, ob_ref, gmix_ref, wgate_ref, bgate_ref, wa_ref, wb_ref, wout_ref,
                 gffn_ref, wfg_ref, wfu_ref, wfd_ref, out_ref, *, d_model):
    x = x_ref[...]
    ms = jnp.mean(x * x, axis=-1, keepdims=True)
    h = (x * lax.rsqrt(ms + EPS) * gmix_ref[...]).astype(BF16)
    gates = jax.nn.sigmoid(_dot(h, wgate_ref[...]) + bgate_ref[...])
    br_a = _dot(oa_ref[...], wa_ref[...])
    br_b = _dot(ob_ref[...], wb_ref[...])
    merged = gates[:, :d_model] * br_a + gates[:, d_model:] * br_b
    x1 = x + _dot(merged.astype(BF16), wout_ref[...])

    ms2 = jnp.mean(x1 * x1, axis=-1, keepdims=True)
    h2 = (x1 * lax.rsqrt(ms2 + EPS) * gffn_ref[...]).astype(BF16)
    fg = _dot(h2, wfg_ref[...])
    fu = _dot(h2, wfu_ref[...])
    ff = (fg * jax.nn.sigmoid(fg)) * fu
    out_ref[...] = x1 + _dot(ff.astype(BF16), wfd_ref[...])


def _rope_tables(seq):
    half = ROPE_DIM // 2
    pos = np.arange(seq, dtype=np.float64)
    inv_freq = ROPE_THETA ** (-np.arange(0, ROPE_DIM, 2, dtype=np.float64) / ROPE_DIM)
    ang = pos[:, None] * inv_freq[None, :]
    cos, sin = np.cos(ang), np.sin(ang)
    ones = np.ones((seq, HEAD_DIM - ROPE_DIM))
    zeros = np.zeros((seq, HEAD_DIM - ROPE_DIM))
    zh = np.zeros((seq, half))
    ra = np.concatenate([cos, cos, ones], axis=1)
    rm = np.concatenate([-sin, zh, zeros], axis=1)
    rp = np.concatenate([zh, sin, zeros], axis=1)
    tile = lambda t: jnp.asarray(np.tile(t, (1, LANES // HEAD_DIM)), dtype=F32)
    return tile(ra), tile(rm), tile(rp)


def _tiles(seq):
    tm, tp, tq_sb, tq_da = 1024, 512, 256, 512
    for t in (tm, tp, tq_sb, tq_da):
        assert seq % t == 0 and t % LANES == 0 and t % CHUNK == 0
    return tm, tp, tq_sb, tq_da


def kernel(x, g_mix, w_in, g_q, g_k, lam_q1, lam_k1, lam_q2, lam_k2, g_sub, w_branch_a, w_branch_b,
           w_gate, b_gate, w_out, g_ffn, w_ffn_gate, w_ffn_up, w_ffn_down):
    b, s, d = x.shape
    depth = g_mix.shape[0]
    sb_w = SB_HEADS * HEAD_DIM
    da_w = DA_HEADS * 2 * HEAD_DIM
    in_w = 3 * sb_w + 3 * da_w
    n = b * s
    d_ff = w_ffn_gate.shape[-1]

    tm, tp, tq_sb, tq_da = _tiles(s)

    ra, rm, rp = _rope_tables(s)
    grp = np.arange(da_w) // HEAD_DIM
    gmat = jnp.asarray(np.where(grp[:, None] == grp[None, :], 1.0 / HEAD_DIM, 0.0), dtype=BF16)
    tile_g = lambda g: jnp.tile(g, da_w // HEAD_DIM)[None, :]
    vmem_full = pl.BlockSpec(memory_space=pltpu.VMEM)
    spt = s // tm

    xf = x.reshape(n, d)
    for layer in range(depth):
        lambda_init = 0.8 - 0.6 * math.exp(-0.3 * layer)
        gmix = g_mix[layer][None, :]

        proj = pl.pallas_call(
            functools.partial(_proj_kernel, sb_w=sb_w, da_w=da_w),
            grid=(n // tm,),
            in_specs=[
                pl.BlockSpec((tm, d), lambda t: (t, 0)),
                vmem_full, vmem_full, vmem_full, vmem_full, vmem_full,
                pl.BlockSpec((tm, LANES), lambda t: (t % spt, 0)),
                pl.BlockSpec((tm, LANES), lambda t: (t % spt, 0)),
                pl.BlockSpec((tm, LANES), lambda t: (t % spt, 0)),
            ],
            out_specs=pl.BlockSpec((tm, in_w), lambda t: (t, 0)),
            out_shape=jax.ShapeDtypeStruct((n, in_w), BF16),
            compiler_params=pltpu.CompilerParams(
                dimension_semantics=("arbitrary",), vmem_limit_bytes=VMEM_LIMIT),
            name="proj",
        )(xf, gmix, w_in[layer].astype(BF16), tile_g(g_q[layer]), tile_g(g_k[layer]), gmat,
          ra, rm, rp)

        cb = lambda off: off // LANES
        tq, gw, ns = tq_sb, SB_GROUPS * LANES, SB_TILES
        nq = s // (ns * tq)
        o_a = pl.pallas_call(
            functools.partial(_sb_kernel, tq=tq, n_grp=SB_GROUPS, n_seg=ns),
            grid=(b, sb_w // gw, nq),
            in_specs=[
                pl.BlockSpec((ns * tq, gw), lambda bi, j, i, nq=nq: (bi * nq + i, j)),
                pl.BlockSpec((s, gw), lambda bi, j, i: (bi, sb_w // gw + j)),
                pl.BlockSpec((s, gw), lambda bi, j, i: (bi, 2 * sb_w // gw + j)),
            ],
            out_specs=pl.BlockSpec((ns * tq, gw), lambda bi, j, i, nq=nq: (bi * nq + i, j)),
            out_shape=jax.ShapeDtypeStruct((n, sb_w), BF16),
            scratch_shapes=[pltpu.VMEM((ns, SB_GROUPS, 2 * tq, LANES), BF16),
                            pltpu.VMEM((ns, SB_GROUPS, 2 * tq, LANES), F32),
                            pltpu.VMEM((ns, SB_GROUPS, 2 * tq, LANES), F32)],
            compiler_params=pltpu.CompilerParams(
                dimension_semantics=("arbitrary", "arbitrary", "arbitrary"),
                vmem_limit_bytes=VMEM_LIMIT),
            name="sb_attn",
        )(proj, proj, proj)

        tq = tq_da
        post_w = [w_gate[layer], w_branch_a[layer], w_branch_b[layer], w_out[layer],
                  w_ffn_gate[layer], w_ffn_up[layer], w_ffn_down[layer]]
        n_steps = b * DA_HEADS
        for w in post_w:
            assert w.shape[0] % (16 * n_steps) == 0
        w_spec = lambda w: pl.BlockSpec((w.shape[0] // n_steps, w.shape[1]),
                                        lambda bi, j: (bi * DA_HEADS + j, 0))
        o_b, *post_w16 = pl.pallas_call(
            functools.partial(_da_flat_kernel, tq=tq, n_tiles=s // tq, n_cast=len(post_w),
                              lambda_init=lambda_init),
            grid=(b, DA_HEADS),
            in_specs=[
                pl.BlockSpec((s, LANES), lambda bi, j: (bi, cb(3 * sb_w) + j)),
                pl.BlockSpec((s, LANES), lambda bi, j: (bi, cb(3 * sb_w + da_w) + j)),
                pl.BlockSpec((s, LANES), lambda bi, j: (bi, cb(3 * sb_w + 2 * da_w) + j)),
                vmem_full, vmem_full, vmem_full, vmem_full, vmem_full,
            ] + [w_spec(w) for w in post_w],
            out_specs=[pl.BlockSpec((s, LANES), lambda bi, j: (bi, j))]
            + [w_spec(w) for w in post_w],
            out_shape=[jax.ShapeDtypeStruct((n, da_w), BF16)]
            + [jax.ShapeDtypeStruct(w.shape, BF16) for w in post_w],
            scratch_shapes=[pltpu.VMEM((s, 2 * LANES), BF16),
                            pltpu.VMEM((2, 2 * tq, LANES), BF16),
                            pltpu.VMEM((2, 2 * tq, tq), F32),
                            pltpu.VMEM((2, 2 * tq, LANES), F32)]
            + [pltpu.VMEM((2 * tq, LANES), F32)] * 3,
            compiler_params=pltpu.CompilerParams(
                dimension_semantics=("arbitrary", "arbitrary"),
                vmem_limit_bytes=VMEM_LIMIT),
            name="da_attn",
        )(proj, proj, proj, lam_q1[layer][None, :], lam_k1[layer][None, :],
          lam_q2[layer][None, :], lam_k2[layer][None, :], g_sub[layer][None, :], *post_w)
        wgate16, wa16, wb16, wout16, wfg16, wfu16, wfd16 = post_w16

        xf = pl.pallas_call(
            functools.partial(_post_kernel, d_model=d),
            grid=(n // tp,),
            in_specs=[
                pl.BlockSpec((tp, d), lambda t: (t, 0)),
                pl.BlockSpec((tp, sb_w), lambda t: (t, 0)),
                pl.BlockSpec((tp, da_w), lambda t: (t, 0)),
            ] + [vmem_full] * 10,
            out_specs=pl.BlockSpec((tp, d), lambda t: (t, 0)),
            out_shape=jax.ShapeDtypeStruct((n, d), F32),
            compiler_params=pltpu.CompilerParams(
                dimension_semantics=("arbitrary",), vmem_limit_bytes=VMEM_LIMIT),
            name="post",
        )(xf, o_a, o_b, gmix, wgate16, b_gate[layer][None, :], wa16, wb16, wout16,
          g_ffn[layer][None, :], wfg16, wfu16, wfd16)
    return xf.reshape(b, s, d)
```

```python
import functools
import math

import jax
import jax.numpy as jnp
import numpy as np
from jax import lax
from jax.experimental import pallas as pl
from jax.experimental.pallas import tpu as pltpu

F32 = jnp.float32
BF16 = jnp.bfloat16

CHUNK = 64
SB_HEADS = 8
DA_HEADS = 4
HEAD_DIM = 64
ROPE_THETA = 500000.0
ROPE_DIM = HEAD_DIM // 4
EPS = 1e-6
NEG_INF = -1e30
LOG2E = math.log2(math.e)
SB_STOP_LOG2 = 150.0
SB_EXP2_MAX = 126.0
LANES = 128
VMEM_LIMIT = 62 * 1024 * 1024


def _dot(a, b):
    return jnp.dot(a, b, preferred_element_type=F32)


def _dot_nt(a, b):
    return lax.dot_general(a, b, (((1,), (1,)), ((), ())), preferred_element_type=F32)


def _proj_kernel(x_ref, gmix_ref, win_ref, gq_ref, gk_ref, gmat_ref,
                 ra_ref, rm_ref, rp_ref, out_ref, *, sb_w, da_w):
    x = x_ref[...]
    ms = jnp.mean(x * x, axis=-1, keepdims=True)
    h = x * lax.rsqrt(ms + EPS) * gmix_ref[...]
    proj = _dot(h.astype(BF16), win_ref[...])
    scale = HEAD_DIM ** -0.5

    out_ref[:, 0:sb_w] = (proj[:, 0:sb_w] * (scale * LOG2E)).astype(BF16)
    out_ref[:, sb_w:3 * sb_w] = proj[:, sb_w:3 * sb_w].astype(BF16)

    ra, rm, rp = ra_ref[...], rm_ref[...], rp_ref[...]

    def qk_norm_rope(t, g, mult):
        msq = _dot((t * t).astype(BF16), gmat_ref[...])
        tn = t * lax.rsqrt(msq + EPS) * g
        cols = []
        for j in range(da_w // LANES):
            c = tn[:, j * LANES:(j + 1) * LANES]
            r = (c * ra + pltpu.roll(c, LANES - ROPE_DIM // 2, 1) * rm
                 + pltpu.roll(c, ROPE_DIM // 2, 1) * rp)
            cols.append((r * mult).astype(BF16))
        return jnp.concatenate(cols, axis=1)

    o = 3 * sb_w
    out_ref[:, o:o + da_w] = qk_norm_rope(proj[:, o:o + da_w], gq_ref[...], scale * LOG2E)
    out_ref[:, o + da_w:o + 2 * da_w] = qk_norm_rope(proj[:, o + da_w:o + 2 * da_w], gk_ref[...], 1.0)
    out_ref[:, o + 2 * da_w:o + 3 * da_w] = proj[:, o + 2 * da_w:o + 3 * da_w].astype(BF16)


def _stack_masked(q, n_parts):
    lane = lax.broadcasted_iota(jnp.int32, q.shape, 1)
    zero = jnp.zeros_like(q)
    return jnp.concatenate(
        [jnp.where((lane >= p * HEAD_DIM) & (lane < (p + 1) * HEAD_DIM), q, zero)
         for p in range(n_parts)], axis=0)


def _sb_program(q_ref, k_ref, v_ref, o_ref, qs_ref, acc_ref, run_ref, first_tile, *, tq, n_grp,
                n_seg):
    m2 = 2 * tq
    r2 = lax.broadcasted_iota(jnp.int32, (tq, tq), 0)
    c2 = lax.broadcasted_iota(jnp.int32, (tq, tq), 1)
    tri = jnp.where(r2 > c2, 1.0, 0.0).astype(BF16)
    lanes = lambda g: slice(g * LANES, (g + 1) * LANES)
    rows = lambda seg: slice(seg * tq, (seg + 1) * tq)

    def block(seg, g, blk, run, diagonal=False, gate=None):
        ks = pl.multiple_of(blk * tq, tq)
        z = _dot_nt(qs_ref[seg, g], k_ref[pl.ds(ks, tq), lanes(g)])
        if diagonal:
            row = lax.broadcasted_iota(jnp.int32, (m2, tq), 0)
            col = lax.broadcasted_iota(jnp.int32, (m2, tq), 1)
            z = jnp.where(col < jnp.where(row >= tq, row - tq, row), z, NEG_INF)
        if gate is not None:
            z = jnp.where(gate > 0.0, z, NEG_INF)
        p = jnp.maximum(z, jnp.log2(1.0 + jnp.exp2(jnp.minimum(z, SB_EXP2_MAX))))
        after = _dot(p.astype(BF16), tri)
        spent = after + jnp.concatenate([run] * (tq // LANES), axis=1)
        run = run + jnp.broadcast_to(after[:, 0:1] + p[:, 0:1], (m2, LANES))
        a = jnp.exp2((z - p) - spent)
        return run, _dot(a.astype(BF16), v_ref[pl.ds(ks, tq), lanes(g)])

    def first_pass(seg, g):
        i = first_tile + seg
        zeros = jnp.zeros((m2, LANES), F32)
        qs_ref[seg, g] = _stack_masked(q_ref[rows(seg), lanes(g)], 2)
        run, out = block(seg, g, i, zeros, diagonal=True)
        if seg == 0:
            run, out2 = block(seg, g, jnp.maximum(i - 1, 0), run,
                              gate=jnp.where(i > 0, 1.0, 0.0).astype(F32))
        else:
            run, out2 = block(seg, g, i - 1, run)
        run_ref[seg, g] = run
        acc_ref[seg, g] = out + out2

    def more(carry):
        blk, least = carry
        return jnp.logical_and(blk >= 0, least < SB_STOP_LOG2)

    def finish():
        lane = lax.broadcasted_iota(jnp.int32, (tq, LANES), 1)
        for seg in range(n_seg):
            def step(carry, seg=seg):
                blk, _ = carry
                least = None
                for g in range(n_grp):
                    run, out = block(seg, g, blk, run_ref[seg, g])
                    run_ref[seg, g] = run
                    acc_ref[seg, g] += out
                    least = jnp.min(run) if least is None else jnp.minimum(least, jnp.min(run))
                return blk - 1, least

            lax.while_loop(more, step, (first_tile + seg - 2, jnp.min(run_ref[seg])))
            for g in range(n_grp):
                o_ref[rows(seg), lanes(g)] = jnp.where(
                    lane < HEAD_DIM, acc_ref[seg, g, 0:tq, :], acc_ref[seg, g, tq:m2, :]
                ).astype(o_ref.dtype)

    pieces = [functools.partial(first_pass, seg, g) for seg in range(n_seg) for g in range(n_grp)]
    return pieces, finish


def _da_program(q_ref, k_ref, v_ref, lq1_ref, lk1_ref, lq2_ref, lk2_ref, gsub_ref, o_ref,
                vext_ref, qs_ref, s_ref, mc_ref, m_ref, l_ref, acc_ref, *, tq, n_tiles,
                lambda_init):
    m2 = 2 * tq
    vext_ref[:, 0:LANES] = v_ref[...]
    vext_ref[:, LANES:2 * LANES] = jnp.ones(v_ref.shape, v_ref.dtype)
    lam = (jnp.exp(jnp.sum(lq1_ref[...] * lk1_ref[...], axis=-1, keepdims=True))
           - jnp.exp(jnp.sum(lq2_ref[...] * lk2_ref[...], axis=-1, keepdims=True))
           + lambda_init)
    steps = [(i, blk) for i in range(n_tiles) for blk in range(i + 1)]
    rows = lambda j: slice(j * tq, (j + 1) * tq)

    h = tq // 2
    quarter = (slice(0, h), slice(h, tq), slice(tq, tq + h), slice(tq + h, m2))
    upper = lambda x: jnp.concatenate([x[quarter[0]], x[quarter[2]]], axis=0)
    lower = lambda x: jnp.concatenate([x[quarter[1]], x[quarter[3]]], axis=0)
    rowmax = lambda s: jnp.broadcast_to(jnp.max(s, axis=1, keepdims=True), (s.shape[0], LANES))

    def scores(t):
        i, blk = steps[t]
        if blk == 0:
            qs_ref[i % 2] = _stack_masked(q_ref[rows(i), :], 2)
        q = qs_ref[i % 2]
        if blk < i:
            s = _dot_nt(q, k_ref[rows(blk), :])
            s_ref[t % 2] = s
            mc_ref[t % 2] = rowmax(s)
        else:
            k0 = blk * tq
            s_ref[t % 2, :, 0:h] = _dot_nt(q, k_ref[k0:k0 + h, :])
            s2 = _dot_nt(lower(q), k_ref[k0 + h:k0 + tq, :])
            s_ref[t % 2, quarter[1], h:tq] = s2[0:h]
            s_ref[t % 2, quarter[3], h:tq] = s2[h:tq]

    def update(s, m_cur, v, prev):
        m_new = m_cur if prev is None else jnp.maximum(prev[0], m_cur)
        p = jnp.exp2(s - jnp.concatenate([m_new] * (s.shape[1] // LANES), axis=1))
        pv = _dot(p.astype(BF16), v)
        if prev is None:
            return m_new, pv[:, LANES:], pv[:, :LANES]
        alpha = jnp.exp2(prev[0] - m_new)
        return m_new, alpha * prev[1] + pv[:, LANES:], alpha * prev[2] + pv[:, :LANES]

    def absorb(t):
        i, blk = steps[t]
        prev = None if blk == 0 else (m_ref[...], l_ref[...], acc_ref[...])
        if blk < i:
            m_ref[...], l_ref[...], acc_ref[...] = update(
                s_ref[t % 2], mc_ref[t % 2], vext_ref[rows(blk), :], prev)
            return
        k0 = blk * tq
        row = lax.broadcasted_iota(jnp.int32, (tq, h), 0)
        col = lax.broadcasted_iota(jnp.int32, (tq, h), 1)
        seen = (col // CHUNK) <= (jnp.where(row >= h, row - h, row) // CHUNK)
        s1 = s_ref[t % 2, :, 0:h]
        s_up = jnp.where(seen, upper(s1), NEG_INF)
        s2 = jnp.concatenate([s_ref[t % 2, quarter[1], h:tq], s_ref[t % 2, quarter[3], h:tq]],
                             axis=0)
        s_lo = jnp.concatenate([lower(s1), jnp.where(seen, s2, NEG_INF)], axis=1)
        _, l_up, acc_up = update(s_up, rowmax(s_up), vext_ref[k0:k0 + h, :],
                                 None if prev is None else tuple(upper(x) for x in prev))
        _, l_lo, acc_lo = update(s_lo, rowmax(s_lo), vext_ref[k0:k0 + tq, :],
                                 None if prev is None else tuple(lower(x) for x in prev))
        res_up, res_lo = acc_up / l_up, acc_lo / l_lo
        res = [jnp.concatenate([res_up[c * h:(c + 1) * h], res_lo[c * h:(c + 1) * h]], axis=0)
               for c in range(2)]
        o = res[0] - lam * res[1]
        ms = jnp.mean(o * o, axis=-1, keepdims=True)
        o = o * lax.rsqrt(ms + EPS) * gsub_ref[...] * (1.0 - lambda_init)
        o_ref[rows(i), :] = o.astype(o_ref.dtype)

    def step(t):
        if t == 0:
            scores(0)
        if t + 1 < len(steps):
            scores(t + 1)
        absorb(t)

    return [functools.partial(step, t) for t in range(len(steps))]


def _attn_kernel(sbq_ref, sbk_ref, sbv_ref, daq_ref, dak_ref, dav_ref, lq1_ref, lk1_ref, lq2_ref,
                 lk2_ref, gsub_ref, *refs, tq_sb, sb_groups, sb_tiles, tq_da, n_tiles_da, n_cast,
                 lambda_init):
    w32_refs, w16_refs = refs[:n_cast], refs[n_cast + 2:2 * n_cast + 2]
    oa_ref, ob_ref = refs[n_cast:n_cast + 2]
    sqs_ref, sacc_ref, srun_ref, vext_ref, dqs_ref, s_ref, mc_ref, m_ref, l_ref, acc_ref = (
        refs[2 * n_cast + 2:])
    for w32, w16 in zip(w32_refs, w16_refs):
        w16[...] = w32[...].astype(w16.dtype)

    pieces, finish = _sb_program(sbq_ref, sbk_ref, sbv_ref, oa_ref, sqs_ref, sacc_ref, srun_ref,
                                 pl.program_id(1) * sb_tiles, tq=tq_sb, n_grp=sb_groups,
                                 n_seg=sb_tiles)
    sweep = _da_program(daq_ref, dak_ref, dav_ref, lq1_ref, lk1_ref, lq2_ref, lk2_ref, gsub_ref,
                        ob_ref, vext_ref, dqs_ref, s_ref, mc_ref, m_ref, l_ref, acc_ref,
                        tq=tq_da, n_tiles=n_tiles_da, lambda_init=lambda_init)
    every = len(sweep) // len(pieces)
    assert every >= 1
    for t, step in enumerate(sweep):
        if t % every == 0 and t // every < len(pieces):
            pieces[t // every]()
        step()
    finish()


def _post_kernel(x_ref, oa_ref, ob_ref, gmix_ref, wgate_ref, bgate_ref, wa_ref, wb_ref, wout_ref,
                 gffn_ref, wfg_ref, wfu_ref, wfd_ref, out_ref, *, d_model):
    x = x_ref[...]
    ms = jnp.mean(x * x, axis=-1, keepdims=True)
    h = (x * lax.rsqrt(ms + EPS) * gmix_ref[...]).astype(BF16)
    gates = jax.nn.sigmoid(_dot(h, wgate_ref[...]) + bgate_ref[...])
    br_a = _dot(oa_ref[...], wa_ref[...])
    br_b = _dot(ob_ref[...], wb_ref[...])
    merged = gates[:, :d_model] * br_a + gates[:, d_model:] * br_b
    x1 = x + _dot(merged.astype(BF16), wout_ref[...])

    ms2 = jnp.mean(x1 * x1, axis=-1, keepdims=True)
    h2 = (x1 * lax.rsqrt(ms2 + EPS) * gffn_ref[...]).astype(BF16)
    fg = _dot(h2, wfg_ref[...])
    fu = _dot(h2, wfu_ref[...])
    ff = (fg * jax.nn.sigmoid(fg)) * fu
    out_ref[...] = x1 + _dot(ff.astype(BF16), wfd_ref[...])


def _rope_tables(seq):
    half = ROPE_DIM // 2
    pos = np.arange(seq, dtype=np.float64)
    inv_freq = ROPE_THETA ** (-np.arange(0, ROPE_DIM, 2, dtype=np.float64) / ROPE_DIM)
    ang = pos[:, None] * inv_freq[None, :]
    cos, sin = np.cos(ang), np.sin(ang)
    ones = np.ones((seq, HEAD_DIM - ROPE_DIM))
    zeros = np.zeros((seq, HEAD_DIM - ROPE_DIM))
    zh = np.zeros((seq, half))
    ra = np.concatenate([cos, cos, ones], axis=1)
    rm = np.concatenate([-sin, zh, zeros], axis=1)
    rp = np.concatenate([zh, sin, zeros], axis=1)
    tile = lambda t: jnp.asarray(np.tile(t, (1, LANES // HEAD_DIM)), dtype=F32)
    return tile(ra), tile(rm), tile(rp)


def _tiles(seq):
    tm, tp, tq_sb, tq_da = 1024, 512, 256, 512
    for t in (tm, tp, tq_sb, tq_da):
        assert seq % t == 0 and t % LANES == 0 and t % CHUNK == 0
    return tm, tp, tq_sb, tq_da


def kernel(x, g_mix, w_in, g_q, g_k, lam_q1, lam_k1, lam_q2, lam_k2, g_sub, w_branch_a, w_branch_b,
           w_gate, b_gate, w_out, g_ffn, w_ffn_gate, w_ffn_up, w_ffn_down):
    b, s, d = x.shape
    depth = g_mix.shape[0]
    sb_w = SB_HEADS * HEAD_DIM
    da_w = DA_HEADS * 2 * HEAD_DIM
    in_w = 3 * sb_w + 3 * da_w
    n = b * s
    d_ff = w_ffn_gate.shape[-1]

    tm, tp, tq_sb, tq_da = _tiles(s)

    ra, rm, rp = _rope_tables(s)
    grp = np.arange(da_w) // HEAD_DIM
    gmat = jnp.asarray(np.where(grp[:, None] == grp[None, :], 1.0 / HEAD_DIM, 0.0), dtype=BF16)
    tile_g = lambda g: jnp.tile(g, da_w // HEAD_DIM)[None, :]
    vmem_full = pl.BlockSpec(memory_space=pltpu.VMEM)
    spt = s // tm

    xf = x.reshape(n, d)
    for layer in range(depth):
        lambda_init = 0.8 - 0.6 * math.exp(-0.3 * layer)
        gmix = g_mix[layer][None, :]

        proj = pl.pallas_call(
            functools.partial(_proj_kernel, sb_w=sb_w, da_w=da_w),
            grid=(n // tm,),
            in_specs=[
                pl.BlockSpec((tm, d), lambda t: (t, 0)),
                vmem_full, vmem_full, vmem_full, vmem_full, vmem_full,
                pl.BlockSpec((tm, LANES), lambda t: (t % spt, 0)),
                pl.BlockSpec((tm, LANES), lambda t: (t % spt, 0)),
                pl.BlockSpec((tm, LANES), lambda t: (t % spt, 0)),
            ],
            out_specs=pl.BlockSpec((tm, in_w), lambda t: (t, 0)),
            out_shape=jax.ShapeDtypeStruct((n, in_w), BF16),
            compiler_params=pltpu.CompilerParams(
                dimension_semantics=("arbitrary",), vmem_limit_bytes=VMEM_LIMIT),
            name="proj",
        )(xf, gmix, w_in[layer].astype(BF16), tile_g(g_q[layer]), tile_g(g_k[layer]), gmat,
          ra, rm, rp)

        cb = lambda off: off // LANES
        sb_tiles = s // (DA_HEADS * tq_sb)
        sb_groups = sb_w // LANES
        post_w = [w_gate[layer], w_branch_a[layer], w_branch_b[layer], w_out[layer],
                  w_ffn_gate[layer], w_ffn_up[layer], w_ffn_down[layer]]
        n_steps = b * DA_HEADS
        for w in post_w:
            assert w.shape[0] % (16 * n_steps) == 0
        step_rows = lambda bi, j: (bi * DA_HEADS + j, 0)
        w_spec = lambda w: pl.BlockSpec((w.shape[0] // n_steps, w.shape[1]), step_rows)
        once = dict(pipeline_mode=pl.Buffered(1))
        o_a, o_b, *post_w16 = pl.pallas_call(
            functools.partial(_attn_kernel, tq_sb=tq_sb, sb_groups=sb_groups, sb_tiles=sb_tiles,
                              tq_da=tq_da, n_tiles_da=s // tq_da, n_cast=len(post_w),
                              lambda_init=lambda_init),
            grid=(b, DA_HEADS),
            in_specs=[
                pl.BlockSpec((sb_tiles * tq_sb, sb_w), step_rows),
                pl.BlockSpec((s, sb_w), lambda bi, j: (bi, 1), **once),
                pl.BlockSpec((s, sb_w), lambda bi, j: (bi, 2), **once),
                pl.BlockSpec((s, LANES), lambda bi, j: (bi, cb(3 * sb_w) + j)),
                pl.BlockSpec((s, LANES), lambda bi, j: (bi, cb(3 * sb_w + da_w) + j)),
                pl.BlockSpec((s, LANES), lambda bi, j: (bi, cb(3 * sb_w + 2 * da_w) + j)),
                vmem_full, vmem_full, vmem_full, vmem_full, vmem_full,
            ] + [w_spec(w) for w in post_w],
            out_specs=[pl.BlockSpec((sb_tiles * tq_sb, sb_w), step_rows),
                       pl.BlockSpec((s, LANES), lambda bi, j: (bi, j))]
            + [w_spec(w) for w in post_w],
            out_shape=[jax.ShapeDtypeStruct((n, sb_w), BF16),
                       jax.ShapeDtypeStruct((n, da_w), BF16)]
            + [jax.ShapeDtypeStruct(w.shape, BF16) for w in post_w],
            scratch_shapes=[pltpu.VMEM((sb_tiles, sb_groups, 2 * tq_sb, LANES), BF16),
                            pltpu.VMEM((sb_tiles, sb_groups, 2 * tq_sb, LANES), F32),
                            pltpu.VMEM((sb_tiles, sb_groups, 2 * tq_sb, LANES), F32),
                            pltpu.VMEM((s, 2 * LANES), BF16),
                            pltpu.VMEM((2, 2 * tq_da, LANES), BF16),
                            pltpu.VMEM((2, 2 * tq_da, tq_da), F32),
                            pltpu.VMEM((2, 2 * tq_da, LANES), F32)]
            + [pltpu.VMEM((2 * tq_da, LANES), F32)] * 3,
            compiler_params=pltpu.CompilerParams(
                dimension_semantics=("arbitrary", "arbitrary"),
                vmem_limit_bytes=VMEM_LIMIT),
            name="attn",
        )(proj, proj, proj, proj, proj, proj, lam_q1[layer][None, :], lam_k1[layer][None, :],
          lam_q2[layer][None, :], lam_k2[layer][None, :], g_sub[layer][None, :], *post_w)
        wgate16, wa16, wb16, wout16, wfg16, wfu16, wfd16 = post_w16

        xf = pl.pallas_call(
            functools.partial(_post_kernel, d_model=d),
            grid=(n // tp,),
            in_specs=[
                pl.BlockSpec((tp, d), lambda t: (t, 0)),
                pl.BlockSpec((tp, sb_w), lambda t: (t, 0)),
                pl.BlockSpec((tp, da_w), lambda t: (t, 0)),
            ] + [vmem_full] * 10,
            out_specs=pl.BlockSpec((tp, d), lambda t: (t, 0)),
            out_shape=jax.ShapeDtypeStruct((n, d), F32),
            compiler_params=pltpu.CompilerParams(
                dimension_semantics=("arbitrary",), vmem_limit_bytes=VMEM_LIMIT),
            name="post",
        )(xf, o_a, o_b, gmix, wgate16, b_gate[layer][None, :], wa16, wb16, wout16,
          g_ffn[layer][None, :], wfg16, wfu16, wfd16)
    return xf.reshape(b, s, d)
```

```python
import functools
import math

import jax
import jax.numpy as jnp
import numpy as np
from jax import lax
from jax.experimental import pallas as pl
from jax.experimental.pallas import tpu as pltpu

F32 = jnp.float32
BF16 = jnp.bfloat16

CHUNK = 64
SB_HEADS = 8
DA_HEADS = 4
HEAD_DIM = 64
ROPE_THETA = 500000.0
ROPE_DIM = HEAD_DIM // 4
EPS = 1e-6
NEG_INF = -1e30
LOG2E = math.log2(math.e)
SB_GROUPS = 4
SB_TILES = 4
SB_STOP_LOG2 = 150.0
SB_EXP2_MAX = 126.0
LANES = 128
VMEM_LIMIT = 56 * 1024 * 1024


def _dot(a, b):
    return jnp.dot(a, b, preferred_element_type=F32)


def _dot_nt(a, b):
    return lax.dot_general(a, b, (((1,), (1,)), ((), ())), preferred_element_type=F32)


def _proj_kernel(x_ref, gmix_ref, win_ref, gq_ref, gk_ref, gmat_ref,
                 ra_ref, rm_ref, rp_ref, out_ref, *, sb_w, da_w):
    x = x_ref[...]
    ms = jnp.mean(x * x, axis=-1, keepdims=True)
    h = (x * lax.rsqrt(ms + EPS) * gmix_ref[...]).astype(BF16)
    scale = HEAD_DIM ** -0.5
    o = 3 * sb_w
    ra, rm, rp = ra_ref[...], rm_ref[...], rp_ref[...]

    def qk_norm_rope(t, g, mult):
        msq = _dot((t * t).astype(BF16), gmat_ref[...])
        tn = t * lax.rsqrt(msq + EPS) * g
        cols = []
        for j in range(da_w // LANES):
            c = tn[:, j * LANES:(j + 1) * LANES]
            r = (c * ra + pltpu.roll(c, LANES - ROPE_DIM // 2, 1) * rm
                 + pltpu.roll(c, ROPE_DIM // 2, 1) * rp)
            cols.append((r * mult).astype(BF16))
        return jnp.concatenate(cols, axis=1)

    qk = _dot(h, win_ref[:, o:o + 2 * da_w])
    out_ref[:, o:o + da_w] = qk_norm_rope(qk[:, 0:da_w], gq_ref[...], scale * LOG2E)
    out_ref[:, o + da_w:o + 2 * da_w] = qk_norm_rope(qk[:, da_w:2 * da_w], gk_ref[...], 1.0)
    sb = _dot(h, win_ref[:, 0:o])
    out_ref[:, 0:sb_w] = (sb[:, 0:sb_w] * (scale * LOG2E)).astype(BF16)
    out_ref[:, sb_w:o] = sb[:, sb_w:o].astype(BF16)
    out_ref[:, o + 2 * da_w:o + 3 * da_w] = _dot(h, win_ref[:, o + 2 * da_w:o + 3 * da_w]).astype(BF16)


def _stack_masked(q, n_parts):
    lane = lax.broadcasted_iota(jnp.int32, q.shape, 1)
    zero = jnp.zeros_like(q)
    return jnp.concatenate(
        [jnp.where((lane >= p * HEAD_DIM) & (lane < (p + 1) * HEAD_DIM), q, zero)
         for p in range(n_parts)], axis=0)


def _sb_kernel(q_ref, k_ref, v_ref, o_ref, qs_ref, acc_ref, run_ref, *, tq, n_grp, n_seg):
    first_tile = pl.program_id(2) * n_seg
    m2 = 2 * tq
    r2 = lax.broadcasted_iota(jnp.int32, (tq, tq), 0)
    c2 = lax.broadcasted_iota(jnp.int32, (tq, tq), 1)
    tri = jnp.where(r2 > c2, 1.0, 0.0).astype(BF16)
    lanes = lambda g: slice(g * LANES, (g + 1) * LANES)
    rows = lambda seg: slice(seg * tq, (seg + 1) * tq)

    def block(seg, g, blk, run, diagonal=False, gate=None):
        ks = pl.multiple_of(blk * tq, tq)
        z = _dot_nt(qs_ref[seg, g], k_ref[pl.ds(ks, tq), lanes(g)])
        if diagonal:
            row = lax.broadcasted_iota(jnp.int32, (m2, tq), 0)
            col = lax.broadcasted_iota(jnp.int32, (m2, tq), 1)
            z = jnp.where(col < jnp.where(row >= tq, row - tq, row), z, NEG_INF)
        if gate is not None:
            z = jnp.where(gate > 0.0, z, NEG_INF)
        p = jnp.maximum(z, jnp.log2(1.0 + jnp.exp2(jnp.minimum(z, SB_EXP2_MAX))))
        after = _dot(p.astype(BF16), tri)
        spent = after + jnp.concatenate([run] * (tq // LANES), axis=1)
        run = run + jnp.broadcast_to(after[:, 0:1] + p[:, 0:1], (m2, LANES))
        a = jnp.exp2((z - p) - spent)
        return run, _dot(a.astype(BF16), v_ref[pl.ds(ks, tq), lanes(g)])

    zeros = jnp.zeros((m2, LANES), F32)
    least0 = []
    for seg in range(n_seg):
        i = first_tile + seg
        least = None
        for g in range(n_grp):
            qs_ref[seg, g] = _stack_masked(q_ref[rows(seg), lanes(g)], 2)
            run, out = block(seg, g, i, zeros, diagonal=True)
            if seg == 0:
                run, out2 = block(seg, g, jnp.maximum(i - 1, 0), run,
                                  gate=jnp.where(i > 0, 1.0, 0.0).astype(F32))
            else:
                run, out2 = block(seg, g, i - 1, run)
            run_ref[seg, g] = run
            acc_ref[seg, g] = out + out2
            least = jnp.min(run) if least is None else jnp.minimum(least, jnp.min(run))
        least0.append(least)

    def more(carry):
        blk, least = carry
        return jnp.logical_and(blk >= 0, least < SB_STOP_LOG2)

    for seg in range(n_seg):
        def step(carry, seg=seg):
            blk, _ = carry
            least = None
            for g in range(n_grp):
                run, out = block(seg, g, blk, run_ref[seg, g])
                run_ref[seg, g] = run
                acc_ref[seg, g] += out
                least = jnp.min(run) if least is None else jnp.minimum(least, jnp.min(run))
            return blk - 1, least

        lax.while_loop(more, step, (first_tile + seg - 2, least0[seg]))

    lane = lax.broadcasted_iota(jnp.int32, (tq, LANES), 1)
    for seg in range(n_seg):
        for g in range(n_grp):
            o_ref[rows(seg), lanes(g)] = jnp.where(
                lane < HEAD_DIM, acc_ref[seg, g, 0:tq, :], acc_ref[seg, g, tq:m2, :]
            ).astype(o_ref.dtype)


def _da_flat_kernel(q_ref, k_ref, v_ref, lq1_ref, lk1_ref, lq2_ref, lk2_ref, gsub_ref, *refs,
                    tq, n_tiles, n_cast, lambda_init):
    w32_refs, o_ref, w16_refs = refs[:n_cast], refs[n_cast], refs[n_cast + 1:2 * n_cast + 1]
    vext_ref, qs_ref, s_ref, mc_ref, m_ref, l_ref, acc_ref = refs[2 * n_cast + 1:]
    for w32, w16 in zip(w32_refs, w16_refs):
        w16[...] = w32[...].astype(w16.dtype)

    m2 = 2 * tq
    vext_ref[:, 0:LANES] = v_ref[...]
    vext_ref[:, LANES:2 * LANES] = jnp.ones(v_ref.shape, v_ref.dtype)
    lam = (jnp.exp(jnp.sum(lq1_ref[...] * lk1_ref[...], axis=-1, keepdims=True))
           - jnp.exp(jnp.sum(lq2_ref[...] * lk2_ref[...], axis=-1, keepdims=True))
           + lambda_init)
    steps = [(i, blk) for i in range(n_tiles) for blk in range(i + 1)]
    rows = lambda j: slice(j * tq, (j + 1) * tq)

    h = tq // 2
    quarter = (slice(0, h), slice(h, tq), slice(tq, tq + h), slice(tq + h, m2))
    upper = lambda x: jnp.concatenate([x[quarter[0]], x[quarter[2]]], axis=0)
    lower = lambda x: jnp.concatenate([x[quarter[1]], x[quarter[3]]], axis=0)
    rowmax = lambda s: jnp.broadcast_to(jnp.max(s, axis=1, keepdims=True), (s.shape[0], LANES))

    def scores(t):
        i, blk = steps[t]
        if blk == 0:
            qs_ref[i % 2] = _stack_masked(q_ref[rows(i), :], 2)
        q = qs_ref[i % 2]
        if blk < i:
            s = _dot_nt(q, k_ref[rows(blk), :])
            s_ref[t % 2] = s
            mc_ref[t % 2] = rowmax(s)
        else:
            k0 = blk * tq
            s_ref[t % 2, :, 0:h] = _dot_nt(q, k_ref[k0:k0 + h, :])
            s2 = _dot_nt(lower(q), k_ref[k0 + h:k0 + tq, :])
            s_ref[t % 2, quarter[1], h:tq] = s2[0:h]
            s_ref[t % 2, quarter[3], h:tq] = s2[h:tq]

    def update(s, m_cur, v, prev):
        m_new = m_cur if prev is None else jnp.maximum(prev[0], m_cur)
        p = jnp.exp2(s - jnp.concatenate([m_new] * (s.shape[1] // LANES), axis=1))
        pv = _dot(p.astype(BF16), v)
        if prev is None:
            return m_new, pv[:, LANES:], pv[:, :LANES]
        alpha = jnp.exp2(prev[0] - m_new)
        return m_new, alpha * prev[1] + pv[:, LANES:], alpha * prev[2] + pv[:, :LANES]

    def absorb(t):
        i, blk = steps[t]
        prev = None if blk == 0 else (m_ref[...], l_ref[...], acc_ref[...])
        if blk < i:
            m_ref[...], l_ref[...], acc_ref[...] = update(
                s_ref[t % 2], mc_ref[t % 2], vext_ref[rows(blk), :], prev)
            return
        k0 = blk * tq
        row = lax.broadcasted_iota(jnp.int32, (tq, h), 0)
        col = lax.broadcasted_iota(jnp.int32, (tq, h), 1)
        seen = (col // CHUNK) <= (jnp.where(row >= h, row - h, row) // CHUNK)
        s1 = s_ref[t % 2, :, 0:h]
        s_up = jnp.where(seen, upper(s1), NEG_INF)
        s2 = jnp.concatenate([s_ref[t % 2, quarter[1], h:tq], s_ref[t % 2, quarter[3], h:tq]],
                             axis=0)
        s_lo = jnp.concatenate([lower(s1), jnp.where(seen, s2, NEG_INF)], axis=1)
        _, l_up, acc_up = update(s_up, rowmax(s_up), vext_ref[k0:k0 + h, :],
                                 None if prev is None else tuple(upper(x) for x in prev))
        _, l_lo, acc_lo = update(s_lo, rowmax(s_lo), vext_ref[k0:k0 + tq, :],
                                 None if prev is None else tuple(lower(x) for x in prev))
        res_up, res_lo = acc_up / l_up, acc_lo / l_lo
        res = [jnp.concatenate([res_up[c * h:(c + 1) * h], res_lo[c * h:(c + 1) * h]], axis=0)
               for c in range(2)]
        o = res[0] - lam * res[1]
        ms = jnp.mean(o * o, axis=-1, keepdims=True)
        o = o * lax.rsqrt(ms + EPS) * gsub_ref[...] * (1.0 - lambda_init)
        o_ref[rows(i), :] = o.astype(o_ref.dtype)

    scores(0)
    for t in range(len(steps)):
        if t + 1 < len(steps):
            scores(t + 1)
        absorb(t)


def _post_kernel(x_ref, oa_ref, ob_ref, gmix_ref, wgate_ref, bgate_ref, wa_ref, wb_ref, wout_ref,
                 gffn_ref, wfg_ref, wfu_ref, wfd_ref, out_ref, *, d_model):
    x = x_ref[...]
    ms = jnp.mean(x * x, axis=-1, keepdims=True)
    h = (x * lax.rsqrt(ms + EPS) * gmix_ref[...]).astype(BF16)
    gates = jax.nn.sigmoid(_dot(h, wgate_ref[...]) + bgate_ref[...])
    br_a = _dot(oa_ref[...], wa_ref[...])
    br_b = _dot(ob_ref[...], wb_ref[...])
    merged = gates[:, :d_model] * br_a + gates[:, d_model:] * br_b
    x1 = x + _dot(merged.astype(BF16), wout_ref[...])

    ms2 = jnp.mean(x1 * x1, axis=-1, keepdims=True)
    h2 = (x1 * lax.rsqrt(ms2 + EPS) * gffn_ref[...]).astype(BF16)
    fg = _dot(h2, wfg_ref[...])
    fu = _dot(h2, wfu_ref[...])
    ff = (fg * jax.nn.sigmoid(fg)) * fu
    out_ref[...] = x1 + _dot(ff.astype(BF16), wfd_ref[...])


def _rope_tables(seq):
    half = ROPE_DIM // 2
    pos = np.arange(seq, dtype=np.float64)
    inv_freq = ROPE_THETA ** (-np.arange(0, ROPE_DIM, 2, dtype=np.float64) / ROPE_DIM)
    ang = pos[:, None] * inv_freq[None, :]
    cos, sin = np.cos(ang), np.sin(ang)
    ones = np.ones((seq, HEAD_DIM - ROPE_DIM))
    zeros = np.zeros((seq, HEAD_DIM - ROPE_DIM))
    zh = np.zeros((seq, half))
    ra = np.concatenate([cos, cos, ones], axis=1)
    rm = np.concatenate([-sin, zh, zeros], axis=1)
    rp = np.concatenate([zh, sin, zeros], axis=1)
    tile = lambda t: jnp.asarray(np.tile(t, (1, LANES // HEAD_DIM)), dtype=F32)
    return tile(ra), tile(rm), tile(rp)


def _tiles(seq):
    tm, tp, tq_sb, tq_da = 1024, 512, 256, 512
    for t in (tm, tp, tq_sb, tq_da):
        assert seq % t == 0 and t % LANES == 0 and t % CHUNK == 0
    return tm, tp, tq_sb, tq_da


def kernel(x, g_mix, w_in, g_q, g_k, lam_q1, lam_k1, lam_q2, lam_k2, g_sub, w_branch_a, w_branch_b,
           w_gate, b_gate, w_out, g_ffn, w_ffn_gate, w_ffn_up, w_ffn_down):
    b, s, d = x.shape
    depth = g_mix.shape[0]
    sb_w = SB_HEADS * HEAD_DIM
    da_w = DA_HEADS * 2 * HEAD_DIM
    in_w = 3 * sb_w + 3 * da_w
    n = b * s
    d_ff = w_ffn_gate.shape[-1]

    tm, tp, tq_sb, tq_da = _tiles(s)

    ra, rm, rp = _rope_tables(s)
    grp = np.arange(da_w) // HEAD_DIM
    gmat = jnp.asarray(np.where(grp[:, None] == grp[None, :], 1.0 / HEAD_DIM, 0.0), dtype=BF16)
    tile_g = lambda g: jnp.tile(g, da_w // HEAD_DIM)[None, :]
    vmem_full = pl.BlockSpec(memory_space=pltpu.VMEM)
    spt = s // tm

    xf = x.reshape(n, d)
    for layer in range(depth):
        lambda_init = 0.8 - 0.6 * math.exp(-0.3 * layer)
        gmix = g_mix[layer][None, :]

        proj = pl.pallas_call(
            functools.partial(_proj_kernel, sb_w=sb_w, da_w=da_w),
            grid=(n // tm,),
            in_specs=[
                pl.BlockSpec((tm, d), lambda t: (t, 0)),
                vmem_full, vmem_full, vmem_full, vmem_full, vmem_full,
                pl.BlockSpec((tm, LANES), lambda t: (t % spt, 0)),
                pl.BlockSpec((tm, LANES), lambda t: (t % spt, 0)),
                pl.BlockSpec((tm, LANES), lambda t: (t % spt, 0)),
            ],
            out_specs=pl.BlockSpec((tm, in_w), lambda t: (t, 0)),
            out_shape=jax.ShapeDtypeStruct((n, in_w), BF16),
            compiler_params=pltpu.CompilerParams(
                dimension_semantics=("arbitrary",), vmem_limit_bytes=VMEM_LIMIT),
            name="proj",
        )(xf, gmix, w_in[layer].astype(BF16), tile_g(g_q[layer]), tile_g(g_k[layer]), gmat,
          ra, rm, rp)

        cb = lambda off: off // LANES
        tq, gw, ns = tq_sb, SB_GROUPS * LANES, SB_TILES
        nq = s // (ns * tq)
        o_a = pl.pallas_call(
            functools.partial(_sb_kernel, tq=tq, n_grp=SB_GROUPS, n_seg=ns),
            grid=(b, sb_w // gw, nq),
            in_specs=[
                pl.BlockSpec((ns * tq, gw), lambda bi, j, i, nq=nq: (bi * nq + i, j)),
                pl.BlockSpec((s, gw), lambda bi, j, i: (bi, sb_w // gw + j)),
                pl.BlockSpec((s, gw), lambda bi, j, i: (bi, 2 * sb_w // gw + j)),
            ],
            out_specs=pl.BlockSpec((ns * tq, gw), lambda bi, j, i, nq=nq: (bi * nq + i, j)),
            out_shape=jax.ShapeDtypeStruct((n, sb_w), BF16),
            scratch_shapes=[pltpu.VMEM((ns, SB_GROUPS, 2 * tq, LANES), BF16),
                            pltpu.VMEM((ns, SB_GROUPS, 2 * tq, LANES), F32),
                            pltpu.VMEM((ns, SB_GROUPS, 2 * tq, LANES), F32)],
            compiler_params=pltpu.CompilerParams(
                dimension_semantics=("arbitrary", "arbitrary", "arbitrary"),
                vmem_limit_bytes=VMEM_LIMIT),
            name="sb_attn",
        )(proj, proj, proj)

        tq = tq_da
        post_w = [w_gate[layer], w_branch_a[layer], w_branch_b[layer], w_out[layer],
                  w_ffn_gate[layer], w_ffn_up[layer], w_ffn_down[layer]]
        n_steps = b * DA_HEADS
        for w in post_w:
            assert w.shape[0] % (16 * n_steps) == 0
        w_spec = lambda w: pl.BlockSpec((w.shape[0] // n_steps, w.shape[1]),
                                        lambda bi, j: (bi * DA_HEADS + j, 0))
        o_b, *post_w16 = pl.pallas_call(
            functools.partial(_da_flat_kernel, tq=tq, n_tiles=s // tq, n_cast=len(post_w),
                              lambda_init=lambda_init),
            grid=(b, DA_HEADS),
            in_specs=[
                pl.BlockSpec((s, LANES), lambda bi, j: (bi, cb(3 * sb_w) + j)),
                pl.BlockSpec((s, LANES), lambda bi, j: (bi, cb(3 * sb_w + da_w) + j)),
                pl.BlockSpec((s, LANES), lambda bi, j: (bi, cb(3 * sb_w + 2 * da_w) + j)),
                vmem_full, vmem_full, vmem_full, vmem_full, vmem_full,
            ] + [w_spec(w) for w in post_w],
            out_specs=[pl.BlockSpec((s, LANES), lambda bi, j: (bi, j))]
            + [w_spec(w) for w in post_w],
            out_shape=[jax.ShapeDtypeStruct((n, da_w), BF16)]
            + [jax.ShapeDtypeStruct(w.shape, BF16) for w in post_w],
            scratch_shapes=[pltpu.VMEM((s, 2 * LANES), BF16),
                            pltpu.VMEM((2, 2 * tq, LANES), BF16),
                            pltpu.VMEM((2, 2 * tq, tq), F32),
                            pltpu.VMEM((2, 2 * tq, LANES), F32)]
            + [pltpu.VMEM((2 * tq, LANES), F32)] * 3,
            compiler_params=pltpu.CompilerParams(
                dimension_semantics=("arbitrary", "arbitrary"),
                vmem_limit_bytes=VMEM_LIMIT),
            name="da_attn",
        )(proj, proj, proj, lam_q1[layer][None, :], lam_k1[layer][None, :],
          lam_q2[layer][None, :], lam_k2[layer][None, :], g_sub[layer][None, :], *post_w)
        wgate16, wa16, wb16, wout16, wfg16, wfu16, wfd16 = post_w16

        xf = pl.pallas_call(
            functools.partial(_post_kernel, d_model=d),
            grid=(n // tp,),
            in_specs=[
                pl.BlockSpec((tp, d), lambda t: (t, 0)),
                pl.BlockSpec((tp, sb_w), lambda t: (t, 0)),
                pl.BlockSpec((tp, da_w), lambda t: (t, 0)),
            ] + [vmem_full] * 10,
            out_specs=pl.BlockSpec((tp, d), lambda t: (t, 0)),
            out_shape=jax.ShapeDtypeStruct((n, d), F32),
            compiler_params=pltpu.CompilerParams(
                dimension_semantics=("arbitrary",), vmem_limit_bytes=VMEM_LIMIT),
            name="post",
        )(xf, o_a, o_b, gmix, wgate16, b_gate[layer][None, :], wa16, wb16, wout16,
          g_ffn[layer][None, :], wfg16, wfu16, wfd16)
    return xf.reshape(b, s, d)
```

```python
import functools
import math

import jax
import jax.numpy as jnp
import numpy as np
from jax import lax
from jax.experimental import pallas as pl
from jax.experimental.pallas import tpu as pltpu

F32 = jnp.float32
BF16 = jnp.bfloat16

CHUNK = 64
SB_HEADS = 8
DA_HEADS = 4
HEAD_DIM = 64
ROPE_THETA = 500000.0
ROPE_DIM = HEAD_DIM // 4
EPS = 1e-6
NEG_INF = -1e30
LOG2E = math.log2(math.e)
SB_GROUPS = 4
SB_TILES = 4
SB_STOP_LOG2 = 152.0
SB_EXP2_MAX = 126.0
LANES = 128
VMEM_LIMIT = 56 * 1024 * 1024


def _dot(a, b):
    return jnp.dot(a, b, preferred_element_type=F32)


def _dot_nt(a, b):
    return lax.dot_general(a, b, (((1,), (1,)), ((), ())), preferred_element_type=F32)


def _proj_kernel(x_ref, gmix_ref, win_ref, gq_ref, gk_ref, gmat_ref,
                 ra_ref, rm_ref, rp_ref, out_ref, *, sb_w, da_w):
    x = x_ref[...]
    ms = jnp.mean(x * x, axis=-1, keepdims=True)
    h = (x * lax.rsqrt(ms + EPS) * gmix_ref[...]).astype(BF16)
    scale = HEAD_DIM ** -0.5
    o = 3 * sb_w
    ra, rm, rp = ra_ref[...], rm_ref[...], rp_ref[...]

    def qk_norm_rope(t, g, mult):
        msq = _dot((t * t).astype(BF16), gmat_ref[...])
        tn = t * lax.rsqrt(msq + EPS) * g
        cols = []
        for j in range(da_w // LANES):
            c = tn[:, j * LANES:(j + 1) * LANES]
            r = (c * ra + pltpu.roll(c, LANES - ROPE_DIM // 2, 1) * rm
                 + pltpu.roll(c, ROPE_DIM // 2, 1) * rp)
            cols.append((r * mult).astype(BF16))
        return jnp.concatenate(cols, axis=1)

    qk = _dot(h, win_ref[:, o:o + 2 * da_w])
    out_ref[:, o:o + da_w] = qk_norm_rope(qk[:, 0:da_w], gq_ref[...], scale * LOG2E)
    out_ref[:, o + da_w:o + 2 * da_w] = qk_norm_rope(qk[:, da_w:2 * da_w], gk_ref[...], 1.0)
    sb = _dot(h, win_ref[:, 0:o])
    out_ref[:, 0:sb_w] = (sb[:, 0:sb_w] * (scale * LOG2E)).astype(BF16)
    out_ref[:, sb_w:o] = sb[:, sb_w:o].astype(BF16)
    out_ref[:, o + 2 * da_w:o + 3 * da_w] = _dot(h, win_ref[:, o + 2 * da_w:o + 3 * da_w]).astype(BF16)


def _stack_masked(q, n_parts):
    lane = lax.broadcasted_iota(jnp.int32, q.shape, 1)
    zero = jnp.zeros_like(q)
    return jnp.concatenate(
        [jnp.where((lane >= p * HEAD_DIM) & (lane < (p + 1) * HEAD_DIM), q, zero)
         for p in range(n_parts)], axis=0)


def _sb_kernel(q_ref, k_ref, v_ref, o_ref, qs_ref, acc_ref, run_ref, *, tq, n_grp, n_seg):
    first_tile = pl.program_id(2) * n_seg
    m2 = 2 * tq
    r2 = lax.broadcasted_iota(jnp.int32, (tq, tq), 0)
    c2 = lax.broadcasted_iota(jnp.int32, (tq, tq), 1)
    tri = jnp.where(r2 > c2, 1.0, 0.0).astype(BF16)
    lanes = lambda g: slice(g * LANES, (g + 1) * LANES)
    rows = lambda seg: slice(seg * tq, (seg + 1) * tq)

    def block(seg, g, blk, run, diagonal=False, gate=None):
        ks = pl.multiple_of(blk * tq, tq)
        z = _dot_nt(qs_ref[seg, g], k_ref[pl.ds(ks, tq), lanes(g)])
        if diagonal:
            row = lax.broadcasted_iota(jnp.int32, (m2, tq), 0)
            col = lax.broadcasted_iota(jnp.int32, (m2, tq), 1)
            z = jnp.where(col < jnp.where(row >= tq, row - tq, row), z, NEG_INF)
        if gate is not None:
            z = jnp.where(gate > 0.0, z, NEG_INF)
        p = jnp.maximum(z, jnp.log2(1.0 + jnp.exp2(jnp.minimum(z, SB_EXP2_MAX))))
        after = _dot(p.astype(BF16), tri)
        spent = after + jnp.concatenate([run] * (tq // LANES), axis=1)
        run = run + jnp.broadcast_to(after[:, 0:1] + p[:, 0:1], (m2, LANES))
        a = jnp.exp2((z - p) - spent)
        return run, _dot(a.astype(BF16), v_ref[pl.ds(ks, tq), lanes(g)])

    zeros = jnp.zeros((m2, LANES), F32)
    least0 = []
    for seg in range(n_seg):
        i = first_tile + seg
        least = None
        for g in range(n_grp):
            qs_ref[seg, g] = _stack_masked(q_ref[rows(seg), lanes(g)], 2)
            run, out = block(seg, g, i, zeros, diagonal=True)
            if seg == 0:
                run, out2 = block(seg, g, jnp.maximum(i - 1, 0), run,
                                  gate=jnp.where(i > 0, 1.0, 0.0).astype(F32))
            else:
                run, out2 = block(seg, g, i - 1, run)
            run_ref[seg, g] = run
            acc_ref[seg, g] = out + out2
            least = jnp.min(run) if least is None else jnp.minimum(least, jnp.min(run))
        least0.append(least)

    def more(carry):
        blk, least = carry
        return jnp.logical_and(blk >= 0, least < SB_STOP_LOG2)

    for seg in range(n_seg):
        def step(carry, seg=seg):
            blk, _ = carry
            least = None
            for g in range(n_grp):
                run, out = block(seg, g, blk, run_ref[seg, g])
                run_ref[seg, g] = run
                acc_ref[seg, g] += out
                least = jnp.min(run) if least is None else jnp.minimum(least, jnp.min(run))
            return blk - 1, least

        lax.while_loop(more, step, (first_tile + seg - 2, least0[seg]))

    lane = lax.broadcasted_iota(jnp.int32, (tq, LANES), 1)
    for seg in range(n_seg):
        for g in range(n_grp):
            o_ref[rows(seg), lanes(g)] = jnp.where(
                lane < HEAD_DIM, acc_ref[seg, g, 0:tq, :], acc_ref[seg, g, tq:m2, :]
            ).astype(o_ref.dtype)


def _da_flat_kernel(q_ref, k_ref, v_ref, lq1_ref, lk1_ref, lq2_ref, lk2_ref, gsub_ref, *refs,
                    tq, n_tiles, n_cast, lambda_init):
    w32_refs, o_ref, w16_refs = refs[:n_cast], refs[n_cast], refs[n_cast + 1:2 * n_cast + 1]
    vext_ref, qs_ref, s_ref, mc_ref, m_ref, l_ref, acc_ref = refs[2 * n_cast + 1:]
    for w32, w16 in zip(w32_refs, w16_refs):
        w16[...] = w32[...].astype(w16.dtype)

    m2 = 2 * tq
    vext_ref[:, 0:LANES] = v_ref[...]
    vext_ref[:, LANES:2 * LANES] = jnp.ones(v_ref.shape, v_ref.dtype)
    lam = (jnp.exp(jnp.sum(lq1_ref[...] * lk1_ref[...], axis=-1, keepdims=True))
           - jnp.exp(jnp.sum(lq2_ref[...] * lk2_ref[...], axis=-1, keepdims=True))
           + lambda_init)
    steps = [(i, blk) for i in range(n_tiles) for blk in range(i + 1)]
    rows = lambda j: slice(j * tq, (j + 1) * tq)

    h = tq // 2
    quarter = (slice(0, h), slice(h, tq), slice(tq, tq + h), slice(tq + h, m2))
    upper = lambda x: jnp.concatenate([x[quarter[0]], x[quarter[2]]], axis=0)
    lower = lambda x: jnp.concatenate([x[quarter[1]], x[quarter[3]]], axis=0)
    rowmax = lambda s: jnp.broadcast_to(jnp.max(s, axis=1, keepdims=True), (s.shape[0], LANES))

    def scores(t):
        i, blk = steps[t]
        if blk == 0:
            qs_ref[i % 2] = _stack_masked(q_ref[rows(i), :], 2)
        q = qs_ref[i % 2]
        if blk < i:
            s = _dot_nt(q, k_ref[rows(blk), :])
            s_ref[t % 2] = s
            mc_ref[t % 2] = rowmax(s)
        else:
            k0 = blk * tq
            s_ref[t % 2, :, 0:h] = _dot_nt(q, k_ref[k0:k0 + h, :])
            s2 = _dot_nt(lower(q), k_ref[k0 + h:k0 + tq, :])
            s_ref[t % 2, quarter[1], h:tq] = s2[0:h]
            s_ref[t % 2, quarter[3], h:tq] = s2[h:tq]

    def update(s, m_cur, v, prev):
        m_new = m_cur if prev is None else jnp.maximum(prev[0], m_cur)
        p = jnp.exp2(s - jnp.concatenate([m_new] * (s.shape[1] // LANES), axis=1))
        pv = _dot(p.astype(BF16), v)
        if prev is None:
            return m_new, pv[:, LANES:], pv[:, :LANES]
        alpha = jnp.exp2(prev[0] - m_new)
        return m_new, alpha * prev[1] + pv[:, LANES:], alpha * prev[2] + pv[:, :LANES]

    def absorb(t):
        i, blk = steps[t]
        prev = None if blk == 0 else (m_ref[...], l_ref[...], acc_ref[...])
        if blk < i:
            m_ref[...], l_ref[...], acc_ref[...] = update(
                s_ref[t % 2], mc_ref[t % 2], vext_ref[rows(blk), :], prev)
            return
        k0 = blk * tq
        row = lax.broadcasted_iota(jnp.int32, (tq, h), 0)
        col = lax.broadcasted_iota(jnp.int32, (tq, h), 1)
        seen = (col // CHUNK) <= (jnp.where(row >= h, row - h, row) // CHUNK)
        s1 = s_ref[t % 2, :, 0:h]
        s_up = jnp.where(seen, upper(s1), NEG_INF)
        s2 = jnp.concatenate([s_ref[t % 2, quarter[1], h:tq], s_ref[t % 2, quarter[3], h:tq]],
                             axis=0)
        s_lo = jnp.concatenate([lower(s1), jnp.where(seen, s2, NEG_INF)], axis=1)
        _, l_up, acc_up = update(s_up, rowmax(s_up), vext_ref[k0:k0 + h, :],
                                 None if prev is None else tuple(upper(x) for x in prev))
        _, l_lo, acc_lo = update(s_lo, rowmax(s_lo), vext_ref[k0:k0 + tq, :],
                                 None if prev is None else tuple(lower(x) for x in prev))
        res_up, res_lo = acc_up / l_up, acc_lo / l_lo
        res = [jnp.concatenate([res_up[c * h:(c + 1) * h], res_lo[c * h:(c + 1) * h]], axis=0)
               for c in range(2)]
        o = res[0] - lam * res[1]
        ms = jnp.mean(o * o, axis=-1, keepdims=True)
        o = o * lax.rsqrt(ms + EPS) * gsub_ref[...] * (1.0 - lambda_init)
        o_ref[rows(i), :] = o.astype(o_ref.dtype)

    scores(0)
    for t in range(len(steps)):
        if t + 1 < len(steps):
            scores(t + 1)
        absorb(t)


def _post_kernel(x_ref, oa_ref, ob_ref, gmix_ref, wgate_ref, bgate_ref, wa_ref, wb_ref, wout_ref,
                 gffn_ref, wfg_ref, wfu_ref, wfd_ref, out_ref, *, d_model):
    x = x_ref[...]
    ms = jnp.mean(x * x, axis=-1, keepdims=True)
    h = (x * lax.rsqrt(ms + EPS) * gmix_ref[...]).astype(BF16)
    gates = jax.nn.sigmoid(_dot(h, wgate_ref[...]) + bgate_ref[...])
    br_a = _dot(oa_ref[...], wa_ref[...])
    br_b = _dot(ob_ref[...], wb_ref[...])
    merged = gates[:, :d_model] * br_a + gates[:, d_model:] * br_b
    x1 = x + _dot(merged.astype(BF16), wout_ref[...])

    ms2 = jnp.mean(x1 * x1, axis=-1, keepdims=True)
    h2 = (x1 * lax.rsqrt(ms2 + EPS) * gffn_ref[...]).astype(BF16)
    fg = _dot(h2, wfg_ref[...])
    fu = _dot(h2, wfu_ref[...])
    ff = (fg * jax.nn.sigmoid(fg)) * fu
    out_ref[...] = x1 + _dot(ff.astype(BF16), wfd_ref[...])


def _rope_tables(seq):
    half = ROPE_DIM // 2
    pos = np.arange(seq, dtype=np.float64)
    inv_freq = ROPE_THETA ** (-np.arange(0, ROPE_DIM, 2, dtype=np.float64) / ROPE_DIM)
    ang = pos[:, None] * inv_freq[None, :]
    cos, sin = np.cos(ang), np.sin(ang)
    ones = np.ones((seq, HEAD_DIM - ROPE_DIM))
    zeros = np.zeros((seq, HEAD_DIM - ROPE_DIM))
    zh = np.zeros((seq, half))
    ra = np.concatenate([cos, cos, ones], axis=1)
    rm = np.concatenate([-sin, zh, zeros], axis=1)
    rp = np.concatenate([zh, sin, zeros], axis=1)
    tile = lambda t: jnp.asarray(np.tile(t, (1, LANES // HEAD_DIM)), dtype=F32)
    return tile(ra), tile(rm), tile(rp)


def _tiles(seq):
    tm, tp, tq_sb, tq_da = 1024, 512, 256, 512
    for t in (tm, tp, tq_sb, tq_da):
        assert seq % t == 0 and t % LANES == 0 and t % CHUNK == 0
    return tm, tp, tq_sb, tq_da


def kernel(x, g_mix, w_in, g_q, g_k, lam_q1, lam_k1, lam_q2, lam_k2, g_sub, w_branch_a, w_branch_b,
           w_gate, b_gate, w_out, g_ffn, w_ffn_gate, w_ffn_up, w_ffn_down):
    b, s, d = x.shape
    depth = g_mix.shape[0]
    sb_w = SB_HEADS * HEAD_DIM
    da_w = DA_HEADS * 2 * HEAD_DIM
    in_w = 3 * sb_w + 3 * da_w
    n = b * s

    tm, tp, tq_sb, tq_da = _tiles(s)

    ra, rm, rp = _rope_tables(s)
    grp = np.arange(da_w) // HEAD_DIM
    gmat = jnp.asarray(np.where(grp[:, None] == grp[None, :], 1.0 / HEAD_DIM, 0.0), dtype=BF16)
    tile_g = lambda g: jnp.tile(g, da_w // HEAD_DIM)[None, :]
    vmem_full = pl.BlockSpec(memory_space=pltpu.VMEM)
    spt = s // tm

    xf = x.reshape(n, d)
    for layer in range(depth):
        lambda_init = 0.8 - 0.6 * math.exp(-0.3 * layer)
        gmix = g_mix[layer][None, :]

        proj = pl.pallas_call(
            functools.partial(_proj_kernel, sb_w=sb_w, da_w=da_w),
            grid=(n // tm,),
            in_specs=[
                pl.BlockSpec((tm, d), lambda t: (t, 0)),
                vmem_full, vmem_full, vmem_full, vmem_full, vmem_full,
                pl.BlockSpec((tm, LANES), lambda t: (t % spt, 0)),
                pl.BlockSpec((tm, LANES), lambda t: (t % spt, 0)),
                pl.BlockSpec((tm, LANES), lambda t: (t % spt, 0)),
            ],
            out_specs=pl.BlockSpec((tm, in_w), lambda t: (t, 0)),
            out_shape=jax.ShapeDtypeStruct((n, in_w), BF16),
            compiler_params=pltpu.CompilerParams(
                dimension_semantics=("arbitrary",), vmem_limit_bytes=VMEM_LIMIT),
            name="proj",
        )(xf, gmix, w_in[layer].astype(BF16), tile_g(g_q[layer]), tile_g(g_k[layer]), gmat,
          ra, rm, rp)

        cb = lambda off: off // LANES
        tq, gw, ns = tq_sb, SB_GROUPS * LANES, SB_TILES
        nq = s // (ns * tq)
        o_a = pl.pallas_call(
            functools.partial(_sb_kernel, tq=tq, n_grp=SB_GROUPS, n_seg=ns),
            grid=(b, sb_w // gw, nq),
            in_specs=[
                pl.BlockSpec((ns * tq, gw), lambda bi, j, i, nq=nq: (bi * nq + i, j)),
                pl.BlockSpec((s, gw), lambda bi, j, i: (bi, sb_w // gw + j)),
                pl.BlockSpec((s, gw), lambda bi, j, i: (bi, 2 * sb_w // gw + j)),
            ],
            out_specs=pl.BlockSpec((ns * tq, gw), lambda bi, j, i, nq=nq: (bi * nq + i, j)),
            out_shape=jax.ShapeDtypeStruct((n, sb_w), BF16),
            scratch_shapes=[pltpu.VMEM((ns, SB_GROUPS, 2 * tq, LANES), BF16),
                            pltpu.VMEM((ns, SB_GROUPS, 2 * tq, LANES), F32),
                            pltpu.VMEM((ns, SB_GROUPS, 2 * tq, LANES), F32)],
            compiler_params=pltpu.CompilerParams(
                dimension_semantics=("arbitrary", "arbitrary", "arbitrary"),
                vmem_limit_bytes=VMEM_LIMIT),
            name="sb_attn",
        )(proj, proj, proj)

        tq = tq_da
        post_w = [w_gate[layer], w_branch_a[layer], w_branch_b[layer], w_out[layer],
                  w_ffn_gate[layer], w_ffn_up[layer], w_ffn_down[layer]]
        n_steps = b * DA_HEADS
        for w in post_w:
            assert w.shape[0] % (16 * n_steps) == 0
        w_spec = lambda w: pl.BlockSpec((w.shape[0] // n_steps, w.shape[1]),
                                        lambda bi, j: (bi * DA_HEADS + j, 0))
        o_b, *post_w16 = pl.pallas_call(
            functools.partial(_da_flat_kernel, tq=tq, n_tiles=s // tq, n_cast=len(post_w),
                              lambda_init=lambda_init),
            grid=(b, DA_HEADS),
            in_specs=[
                pl.BlockSpec((s, LANES), lambda bi, j: (bi, cb(3 * sb_w) + j)),
                pl.BlockSpec((s, LANES), lambda bi, j: (bi, cb(3 * sb_w + da_w) + j)),
                pl.BlockSpec((s, LANES), lambda bi, j: (bi, cb(3 * sb_w + 2 * da_w) + j)),
                vmem_full, vmem_full, vmem_full, vmem_full, vmem_full,
            ] + [w_spec(w) for w in post_w],
            out_specs=[pl.BlockSpec((s, LANES), lambda bi, j: (bi, j))]
            + [w_spec(w) for w in post_w],
            out_shape=[jax.ShapeDtypeStruct((n, da_w), BF16)]
            + [jax.ShapeDtypeStruct(w.shape, BF16) for w in post_w],
            scratch_shapes=[pltpu.VMEM((s, 2 * LANES), BF16),
                            pltpu.VMEM((2, 2 * tq, LANES), BF16),
                            pltpu.VMEM((2, 2 * tq, tq), F32),
                            pltpu.VMEM((2, 2 * tq, LANES), F32)]
            + [pltpu.VMEM((2 * tq, LANES), F32)] * 3,
            compiler_params=pltpu.CompilerParams(
                dimension_semantics=("arbitrary", "arbitrary"),
                vmem_limit_bytes=VMEM_LIMIT),
            name="da_attn",
        )(proj, proj, proj, lam_q1[layer][None, :], lam_k1[layer][None, :],
          lam_q2[layer][None, :], lam_k2[layer][None, :], g_sub[layer][None, :], *post_w)
        wgate16, wa16, wb16, wout16, wfg16, wfu16, wfd16 = post_w16

        xf = pl.pallas_call(
            functools.partial(_post_kernel, d_model=d),
            grid=(n // tp,),
            in_specs=[
                pl.BlockSpec((tp, d), lambda t: (t, 0)),
                pl.BlockSpec((tp, sb_w), lambda t: (t, 0)),
                pl.BlockSpec((tp, da_w), lambda t: (t, 0)),
            ] + [vmem_full] * 10,
            out_specs=pl.BlockSpec((tp, d), lambda t: (t, 0)),
            out_shape=jax.ShapeDtypeStruct((n, d), F32),
            compiler_params=pltpu.CompilerParams(
                dimension_semantics=("arbitrary",), vmem_limit_bytes=VMEM_LIMIT),
            name="post",
        )(xf, o_a, o_b, gmix, wgate16, b_gate[layer][None, :], wa16, wb16, wout16,
          g_ffn[layer][None, :], wfg16, wfu16, wfd16)
    return xf.reshape(b, s, d)
```

```python
import functools
import math

import jax
import jax.numpy as jnp
import numpy as np
from jax import lax
from jax.experimental import pallas as pl
from jax.experimental.pallas import tpu as pltpu

F32 = jnp.float32
BF16 = jnp.bfloat16

CHUNK = 64
SB_HEADS = 8
DA_HEADS = 4
HEAD_DIM = 64
ROPE_THETA = 500000.0
ROPE_DIM = HEAD_DIM // 4
EPS = 1e-6
NEG_INF = -1e30
LOG2E = math.log2(math.e)
SB_GROUPS = 4
SB_TILES = 4
SB_STOP_LOG2 = 152.0
SB_EXP2_MAX = 126.0
LANES = 128
VMEM_LIMIT = 56 * 1024 * 1024


def _dot(a, b):
    return jnp.dot(a, b, preferred_element_type=F32)


def _dot_nt(a, b):
    return lax.dot_general(a, b, (((1,), (1,)), ((), ())), preferred_element_type=F32)


def _proj_kernel(x_ref, gmix_ref, win_ref, gq_ref, gk_ref, gmat_ref,
                 ra_ref, rm_ref, rp_ref, out_ref, *, sb_w, da_w):
    x = x_ref[...]
    ms = jnp.mean(x * x, axis=-1, keepdims=True)
    h = (x * lax.rsqrt(ms + EPS) * gmix_ref[...]).astype(BF16)
    scale = HEAD_DIM ** -0.5
    o = 3 * sb_w
    ra, rm, rp = ra_ref[...], rm_ref[...], rp_ref[...]

    def qk_norm_rope(t, g, mult):
        msq = _dot((t * t).astype(BF16), gmat_ref[...])
        tn = t * lax.rsqrt(msq + EPS) * g
        cols = []
        for j in range(da_w // LANES):
            c = tn[:, j * LANES:(j + 1) * LANES]
            r = (c * ra + pltpu.roll(c, LANES - ROPE_DIM // 2, 1) * rm
                 + pltpu.roll(c, ROPE_DIM // 2, 1) * rp)
            cols.append((r * mult).astype(BF16))
        return jnp.concatenate(cols, axis=1)

    qk = _dot(h, win_ref[:, o:o + 2 * da_w])
    out_ref[:, o:o + da_w] = qk_norm_rope(qk[:, 0:da_w], gq_ref[...], scale * LOG2E)
    out_ref[:, o + da_w:o + 2 * da_w] = qk_norm_rope(qk[:, da_w:2 * da_w], gk_ref[...], 1.0)
    sb = _dot(h, win_ref[:, 0:o])
    out_ref[:, 0:sb_w] = (sb[:, 0:sb_w] * (scale * LOG2E)).astype(BF16)
    out_ref[:, sb_w:o] = sb[:, sb_w:o].astype(BF16)
    out_ref[:, o + 2 * da_w:o + 3 * da_w] = _dot(h, win_ref[:, o + 2 * da_w:o + 3 * da_w]).astype(BF16)


def _stack_masked(q, n_parts):
    lane = lax.broadcasted_iota(jnp.int32, q.shape, 1)
    zero = jnp.zeros_like(q)
    return jnp.concatenate(
        [jnp.where((lane >= p * HEAD_DIM) & (lane < (p + 1) * HEAD_DIM), q, zero)
         for p in range(n_parts)], axis=0)


def _sb_kernel(q_ref, k_ref, v_ref, o_ref, qs_ref, acc_ref, run_ref, *, tq, n_grp, n_seg):
    first_tile = pl.program_id(2) * n_seg
    m2 = 2 * tq
    r2 = lax.broadcasted_iota(jnp.int32, (tq, tq), 0)
    c2 = lax.broadcasted_iota(jnp.int32, (tq, tq), 1)
    tri = jnp.where(r2 > c2, 1.0, 0.0).astype(BF16)
    lanes = lambda g: slice(g * LANES, (g + 1) * LANES)
    rows = lambda seg: slice(seg * tq, (seg + 1) * tq)

    def block(seg, g, blk, run, diagonal=False, gate=None):
        ks = pl.multiple_of(blk * tq, tq)
        z = _dot_nt(qs_ref[seg, g], k_ref[pl.ds(ks, tq), lanes(g)])
        if diagonal:
            row = lax.broadcasted_iota(jnp.int32, (m2, tq), 0)
            col = lax.broadcasted_iota(jnp.int32, (m2, tq), 1)
            z = jnp.where(col < jnp.where(row >= tq, row - tq, row), z, NEG_INF)
        if gate is not None:
            z = jnp.where(gate > 0.0, z, NEG_INF)
        p = jnp.maximum(z, jnp.log2(1.0 + jnp.exp2(jnp.minimum(z, SB_EXP2_MAX))))
        after = _dot(p.astype(BF16), tri)
        spent = after + jnp.concatenate([run] * (tq // LANES), axis=1)
        run = run + jnp.broadcast_to(after[:, 0:1] + p[:, 0:1], (m2, LANES))
        a = jnp.exp2((z - p) - spent)
        return run, _dot(a.astype(BF16), v_ref[pl.ds(ks, tq), lanes(g)])

    def pair(seg, g, i):
        ks = pl.multiple_of((i - 1) * tq, tq)
        z = _dot_nt(qs_ref[seg, g], k_ref[pl.ds(ks, 2 * tq), lanes(g)])
        row = lax.broadcasted_iota(jnp.int32, (m2, 2 * tq), 0)
        col = lax.broadcasted_iota(jnp.int32, (m2, 2 * tq), 1)
        z = jnp.where(col < jnp.where(row >= tq, row - tq, row) + tq, z, NEG_INF)
        p = jnp.maximum(z, jnp.log2(1.0 + jnp.exp2(jnp.minimum(z, SB_EXP2_MAX))))
        p_prev, p_diag = p[:, 0:tq], p[:, tq:2 * tq]
        after = _dot(jnp.concatenate([p_diag, p_prev], axis=0).astype(BF16), tri)
        after_diag, after_prev = after[0:m2], after[m2:2 * m2]
        run = jnp.broadcast_to(after_diag[:, 0:1] + p_diag[:, 0:1], (m2, LANES))
        spent = jnp.concatenate(
            [after_prev + jnp.concatenate([run] * (tq // LANES), axis=1), after_diag], axis=1)
        run = run + jnp.broadcast_to(after_prev[:, 0:1] + p_prev[:, 0:1], (m2, LANES))
        a = jnp.exp2((z - p) - spent)
        return run, _dot(a.astype(BF16), v_ref[pl.ds(ks, 2 * tq), lanes(g)])

    zeros = jnp.zeros((m2, LANES), F32)
    least0 = []
    for seg in range(n_seg):
        i = first_tile + seg
        least = None
        for g in range(n_grp):
            qs_ref[seg, g] = _stack_masked(q_ref[rows(seg), lanes(g)], 2)
            if seg == 0:
                run, out = block(seg, g, i, zeros, diagonal=True)
                run, out2 = block(seg, g, jnp.maximum(i - 1, 0), run,
                                  gate=jnp.where(i > 0, 1.0, 0.0).astype(F32))
                out = out + out2
            else:
                run, out = pair(seg, g, i)
            run_ref[seg, g] = run
            acc_ref[seg, g] = out
            least = jnp.min(run) if least is None else jnp.minimum(least, jnp.min(run))
        least0.append(least)

    def more(carry):
        blk, least = carry
        return jnp.logical_and(blk >= 0, least < SB_STOP_LOG2)

    for seg in range(n_seg):
        def step(carry, seg=seg):
            blk, _ = carry
            least = None
            for g in range(n_grp):
                run, out = block(seg, g, blk, run_ref[seg, g])
                run_ref[seg, g] = run
                acc_ref[seg, g] += out
                least = jnp.min(run) if least is None else jnp.minimum(least, jnp.min(run))
            return blk - 1, least

        lax.while_loop(more, step, (first_tile + seg - 2, least0[seg]))

    lane = lax.broadcasted_iota(jnp.int32, (tq, LANES), 1)
    for seg in range(n_seg):
        for g in range(n_grp):
            o_ref[rows(seg), lanes(g)] = jnp.where(
                lane < HEAD_DIM, acc_ref[seg, g, 0:tq, :], acc_ref[seg, g, tq:m2, :]
            ).astype(o_ref.dtype)


def _da_flat_kernel(q_ref, k_ref, v_ref, lq1_ref, lk1_ref, lq2_ref, lk2_ref, gsub_ref, *refs,
                    tq, n_tiles, n_cast, lambda_init):
    w32_refs, o_ref, w16_refs = refs[:n_cast], refs[n_cast], refs[n_cast + 1:2 * n_cast + 1]
    vext_ref, qs_ref, s_ref, mc_ref, m_ref, l_ref, acc_ref = refs[2 * n_cast + 1:]
    for w32, w16 in zip(w32_refs, w16_refs):
        w16[...] = w32[...].astype(w16.dtype)

    m2 = 2 * tq
    vext_ref[:, 0:LANES] = v_ref[...]
    vext_ref[:, LANES:2 * LANES] = jnp.ones(v_ref.shape, v_ref.dtype)
    lam = (jnp.exp(jnp.sum(lq1_ref[...] * lk1_ref[...], axis=-1, keepdims=True))
           - jnp.exp(jnp.sum(lq2_ref[...] * lk2_ref[...], axis=-1, keepdims=True))
           + lambda_init)
    steps = [(i, blk) for i in range(n_tiles) for blk in range(i + 1)]
    rows = lambda j: slice(j * tq, (j + 1) * tq)

    h = tq // 2
    quarter = (slice(0, h), slice(h, tq), slice(tq, tq + h), slice(tq + h, m2))
    upper = lambda x: jnp.concatenate([x[quarter[0]], x[quarter[2]]], axis=0)
    lower = lambda x: jnp.concatenate([x[quarter[1]], x[quarter[3]]], axis=0)
    rowmax = lambda s: jnp.broadcast_to(jnp.max(s, axis=1, keepdims=True), (s.shape[0], LANES))

    def scores(t):
        i, blk = steps[t]
        if blk == 0:
            qs_ref[i % 2] = _stack_masked(q_ref[rows(i), :], 2)
        q = qs_ref[i % 2]
        if blk < i:
            s = _dot_nt(q, k_ref[rows(blk), :])
            s_ref[t % 2] = s
            mc_ref[t % 2] = rowmax(s)
        else:
            k0 = blk * tq
            s_ref[t % 2, :, 0:h] = _dot_nt(q, k_ref[k0:k0 + h, :])
            s2 = _dot_nt(lower(q), k_ref[k0 + h:k0 + tq, :])
            s_ref[t % 2, quarter[1], h:tq] = s2[0:h]
            s_ref[t % 2, quarter[3], h:tq] = s2[h:tq]

    def update(s, m_cur, v, prev):
        m_new = m_cur if prev is None else jnp.maximum(prev[0], m_cur)
        p = jnp.exp2(s - jnp.concatenate([m_new] * (s.shape[1] // LANES), axis=1))
        pv = _dot(p.astype(BF16), v)
        if prev is None:
            return m_new, pv[:, LANES:], pv[:, :LANES]
        alpha = jnp.exp2(prev[0] - m_new)
        return m_new, alpha * prev[1] + pv[:, LANES:], alpha * prev[2] + pv[:, :LANES]

    def absorb(t):
        i, blk = steps[t]
        prev = None if blk == 0 else (m_ref[...], l_ref[...], acc_ref[...])
        if blk < i:
            m_ref[...], l_ref[...], acc_ref[...] = update(
                s_ref[t % 2], mc_ref[t % 2], vext_ref[rows(blk), :], prev)
            return
        k0 = blk * tq
        row = lax.broadcasted_iota(jnp.int32, (tq, h), 0)
        col = lax.broadcasted_iota(jnp.int32, (tq, h), 1)
        seen = (col // CHUNK) <= (jnp.where(row >= h, row - h, row) // CHUNK)
        s1 = s_ref[t % 2, :, 0:h]
        s_up = jnp.where(seen, upper(s1), NEG_INF)
        s2 = jnp.concatenate([s_ref[t % 2, quarter[1], h:tq], s_ref[t % 2, quarter[3], h:tq]],
                             axis=0)
        s_lo = jnp.concatenate([lower(s1), jnp.where(seen, s2, NEG_INF)], axis=1)
        _, l_up, acc_up = update(s_up, rowmax(s_up), vext_ref[k0:k0 + h, :],
                                 None if prev is None else tuple(upper(x) for x in prev))
        _, l_lo, acc_lo = update(s_lo, rowmax(s_lo), vext_ref[k0:k0 + tq, :],
                                 None if prev is None else tuple(lower(x) for x in prev))
        res_up, res_lo = acc_up / l_up, acc_lo / l_lo
        res = [jnp.concatenate([res_up[c * h:(c + 1) * h], res_lo[c * h:(c + 1) * h]], axis=0)
               for c in range(2)]
        o = res[0] - lam * res[1]
        ms = jnp.mean(o * o, axis=-1, keepdims=True)
        o = o * lax.rsqrt(ms + EPS) * gsub_ref[...] * (1.0 - lambda_init)
        o_ref[rows(i), :] = o.astype(o_ref.dtype)

    scores(0)
    for t in range(len(steps)):
        if t + 1 < len(steps):
            scores(t + 1)
        absorb(t)


def _post_kernel(x_ref, oa_ref, ob_ref, gmix_ref, wgate_ref, bgate_ref, wa_ref, wb_ref, wout_ref,
                 gffn_ref, wfg_ref, wfu_ref, wfd_ref, out_ref, *, d_model):
    x = x_ref[...]
    ms = jnp.mean(x * x, axis=-1, keepdims=True)
    h = (x * lax.rsqrt(ms + EPS) * gmix_ref[...]).astype(BF16)
    gates = jax.nn.sigmoid(_dot(h, wgate_ref[...]) + bgate_ref[...])
    br_a = _dot(oa_ref[...], wa_ref[...])
    br_b = _dot(ob_ref[...], wb_ref[...])
    merged = gates[:, :d_model] * br_a + gates[:, d_model:] * br_b
    x1 = x + _dot(merged.astype(BF16), wout_ref[...])

    ms2 = jnp.mean(x1 * x1, axis=-1, keepdims=True)
    h2 = (x1 * lax.rsqrt(ms2 + EPS) * gffn_ref[...]).astype(BF16)
    fg = _dot(h2, wfg_ref[...])
    fu = _dot(h2, wfu_ref[...])
    ff = (fg * jax.nn.sigmoid(fg)) * fu
    out_ref[...] = x1 + _dot(ff.astype(BF16), wfd_ref[...])


def _rope_tables(seq):
    half = ROPE_DIM // 2
    pos = np.arange(seq, dtype=np.float64)
    inv_freq = ROPE_THETA ** (-np.arange(0, ROPE_DIM, 2, dtype=np.float64) / ROPE_DIM)
    ang = pos[:, None] * inv_freq[None, :]
    cos, sin = np.cos(ang), np.sin(ang)
    ones = np.ones((seq, HEAD_DIM - ROPE_DIM))
    zeros = np.zeros((seq, HEAD_DIM - ROPE_DIM))
    zh = np.zeros((seq, half))
    ra = np.concatenate([cos, cos, ones], axis=1)
    rm = np.concatenate([-sin, zh, zeros], axis=1)
    rp = np.concatenate([zh, sin, zeros], axis=1)
    tile = lambda t: jnp.asarray(np.tile(t, (1, LANES // HEAD_DIM)), dtype=F32)
    return tile(ra), tile(rm), tile(rp)


def _tiles(seq):
    tm, tp, tq_sb, tq_da = 1024, 512, 256, 512
    for t in (tm, tp, tq_sb, tq_da):
        assert seq % t == 0 and t % LANES == 0 and t % CHUNK == 0
    return tm, tp, tq_sb, tq_da


def kernel(x, g_mix, w_in, g_q, g_k, lam_q1, lam_k1, lam_q2, lam_k2, g_sub, w_branch_a, w_branch_b,
           w_gate, b_gate, w_out, g_ffn, w_ffn_gate, w_ffn_up, w_ffn_down):
    b, s, d = x.shape
    depth = g_mix.shape[0]
    sb_w = SB_HEADS * HEAD_DIM
    da_w = DA_HEADS * 2 * HEAD_DIM
    in_w = 3 * sb_w + 3 * da_w
    n = b * s

    tm, tp, tq_sb, tq_da = _tiles(s)

    ra, rm, rp = _rope_tables(s)
    grp = np.arange(da_w) // HEAD_DIM
    gmat = jnp.asarray(np.where(grp[:, None] == grp[None, :], 1.0 / HEAD_DIM, 0.0), dtype=BF16)
    tile_g = lambda g: jnp.tile(g, da_w // HEAD_DIM)[None, :]
    vmem_full = pl.BlockSpec(memory_space=pltpu.VMEM)
    spt = s // tm

    xf = x.reshape(n, d)
    for layer in range(depth):
        lambda_init = 0.8 - 0.6 * math.exp(-0.3 * layer)
        gmix = g_mix[layer][None, :]

        proj = pl.pallas_call(
            functools.partial(_proj_kernel, sb_w=sb_w, da_w=da_w),
            grid=(n // tm,),
            in_specs=[
                pl.BlockSpec((tm, d), lambda t: (t, 0)),
                vmem_full, vmem_full, vmem_full, vmem_full, vmem_full,
                pl.BlockSpec((tm, LANES), lambda t: (t % spt, 0)),
                pl.BlockSpec((tm, LANES), lambda t: (t % spt, 0)),
                pl.BlockSpec((tm, LANES), lambda t: (t % spt, 0)),
            ],
            out_specs=pl.BlockSpec((tm, in_w), lambda t: (t, 0)),
            out_shape=jax.ShapeDtypeStruct((n, in_w), BF16),
            compiler_params=pltpu.CompilerParams(
                dimension_semantics=("arbitrary",), vmem_limit_bytes=VMEM_LIMIT),
            name="proj",
        )(xf, gmix, w_in[layer].astype(BF16), tile_g(g_q[layer]), tile_g(g_k[layer]), gmat,
          ra, rm, rp)

        cb = lambda off: off // LANES
        tq, gw, ns = tq_sb, SB_GROUPS * LANES, SB_TILES
        nq = s // (ns * tq)
        o_a = pl.pallas_call(
            functools.partial(_sb_kernel, tq=tq, n_grp=SB_GROUPS, n_seg=ns),
            grid=(b, sb_w // gw, nq),
            in_specs=[
                pl.BlockSpec((ns * tq, gw), lambda bi, j, i, nq=nq: (bi * nq + i, j)),
                pl.BlockSpec((s, gw), lambda bi, j, i: (bi, sb_w // gw + j)),
                pl.BlockSpec((s, gw), lambda bi, j, i: (bi, 2 * sb_w // gw + j)),
            ],
            out_specs=pl.BlockSpec((ns * tq, gw), lambda bi, j, i, nq=nq: (bi * nq + i, j)),
            out_shape=jax.ShapeDtypeStruct((n, sb_w), BF16),
            scratch_shapes=[pltpu.VMEM((ns, SB_GROUPS, 2 * tq, LANES), BF16),
                            pltpu.VMEM((ns, SB_GROUPS, 2 * tq, LANES), F32),
                            pltpu.VMEM((ns, SB_GROUPS, 2 * tq, LANES), F32)],
            compiler_params=pltpu.CompilerParams(
                dimension_semantics=("arbitrary", "arbitrary", "arbitrary"),
                vmem_limit_bytes=VMEM_LIMIT),
            name="sb_attn",
        )(proj, proj, proj)

        tq = tq_da
        post_w = [w_gate[layer], w_branch_a[layer], w_branch_b[layer], w_out[layer],
                  w_ffn_gate[layer], w_ffn_up[layer], w_ffn_down[layer]]
        n_steps = b * DA_HEADS
        for w in post_w:
            assert w.shape[0] % (16 * n_steps) == 0
        w_spec = lambda w: pl.BlockSpec((w.shape[0] // n_steps, w.shape[1]),
                                        lambda bi, j: (bi * DA_HEADS + j, 0))
        o_b, *post_w16 = pl.pallas_call(
            functools.partial(_da_flat_kernel, tq=tq, n_tiles=s // tq, n_cast=len(post_w),
                              lambda_init=lambda_init),
            grid=(b, DA_HEADS),
            in_specs=[
                pl.BlockSpec((s, LANES), lambda bi, j: (bi, cb(3 * sb_w) + j)),
                pl.BlockSpec((s, LANES), lambda bi, j: (bi, cb(3 * sb_w + da_w) + j)),
                pl.BlockSpec((s, LANES), lambda bi, j: (bi, cb(3 * sb_w + 2 * da_w) + j)),
                vmem_full, vmem_full, vmem_full, vmem_full, vmem_full,
            ] + [w_spec(w) for w in post_w],
            out_specs=[pl.BlockSpec((s, LANES), lambda bi, j: (bi, j))]
            + [w_spec(w) for w in post_w],
            out_shape=[jax.ShapeDtypeStruct((n, da_w), BF16)]
            + [jax.ShapeDtypeStruct(w.shape, BF16) for w in post_w],
            scratch_shapes=[pltpu.VMEM((s, 2 * LANES), BF16),
                            pltpu.VMEM((2, 2 * tq, LANES), BF16),
                            pltpu.VMEM((2, 2 * tq, tq), F32),
                            pltpu.VMEM((2, 2 * tq, LANES), F32)]
            + [pltpu.VMEM((2 * tq, LANES), F32)] * 3,
            compiler_params=pltpu.CompilerParams(
                dimension_semantics=("arbitrary", "arbitrary"),
                vmem_limit_bytes=VMEM_LIMIT),
            name="da_attn",
        )(proj, proj, proj, lam_q1[layer][None, :], lam_k1[layer][None, :],
          lam_q2[layer][None, :], lam_k2[layer][None, :], g_sub[layer][None, :], *post_w)
        wgate16, wa16, wb16, wout16, wfg16, wfu16, wfd16 = post_w16

        xf = pl.pallas_call(
            functools.partial(_post_kernel, d_model=d),
            grid=(n // tp,),
            in_specs=[
                pl.BlockSpec((tp, d), lambda t: (t, 0)),
                pl.BlockSpec((tp, sb_w), lambda t: (t, 0)),
                pl.BlockSpec((tp, da_w), lambda t: (t, 0)),
            ] + [vmem_full] * 10,
            out_specs=pl.BlockSpec((tp, d), lambda t: (t, 0)),
            out_shape=jax.ShapeDtypeStruct((n, d), F32),
            compiler_params=pltpu.CompilerParams(
                dimension_semantics=("arbitrary",), vmem_limit_bytes=VMEM_LIMIT),
            name="post",
        )(xf, o_a, o_b, gmix, wgate16, b_gate[layer][None, :], wa16, wb16, wout16,
          g_ffn[layer][None, :], wfg16, wfu16, wfd16)
    return xf.reshape(b, s, d)
```

```python
import functools
import math

import jax
import jax.numpy as jnp
import numpy as np
from jax import lax
from jax.experimental import pallas as pl
from jax.experimental.pallas import tpu as pltpu

F32 = jnp.float32
BF16 = jnp.bfloat16

CHUNK = 64
SB_HEADS = 8
DA_HEADS = 4
HEAD_DIM = 64
ROPE_THETA = 500000.0
ROPE_DIM = HEAD_DIM // 4
EPS = 1e-6
NEG_INF = -1e30
LOG2E = math.log2(math.e)
SB_GROUPS = 4
SB_TILES = 4
SB_STOP_LOG2 = 152.0
SB_EXP2_MAX = 126.0
LANES = 128
VMEM_LIMIT = 56 * 1024 * 1024


def _dot(a, b):
    return jnp.dot(a, b, preferred_element_type=F32)


def _dot_nt(a, b):
    return lax.dot_general(a, b, (((1,), (1,)), ((), ())), preferred_element_type=F32)


def _proj_kernel(x_ref, gmix_ref, win_ref, gq_ref, gk_ref, gmat_ref,
                 ra_ref, rm_ref, rp_ref, out_ref, *, sb_w, da_w):
    x = x_ref[...]
    ms = jnp.mean(x * x, axis=-1, keepdims=True)
    h = (x * lax.rsqrt(ms + EPS) * gmix_ref[...]).astype(BF16)
    scale = HEAD_DIM ** -0.5
    o = 3 * sb_w
    ra, rm, rp = ra_ref[...], rm_ref[...], rp_ref[...]

    def qk_norm_rope(t, g, mult):
        msq = _dot((t * t).astype(BF16), gmat_ref[...])
        tn = t * lax.rsqrt(msq + EPS) * g
        cols = []
        for j in range(da_w // LANES):
            c = tn[:, j * LANES:(j + 1) * LANES]
            r = (c * ra + pltpu.roll(c, LANES - ROPE_DIM // 2, 1) * rm
                 + pltpu.roll(c, ROPE_DIM // 2, 1) * rp)
            cols.append((r * mult).astype(BF16))
        return jnp.concatenate(cols, axis=1)

    qk = _dot(h, win_ref[:, o:o + 2 * da_w])
    out_ref[:, o:o + da_w] = qk_norm_rope(qk[:, 0:da_w], gq_ref[...], scale * LOG2E)
    out_ref[:, o + da_w:o + 2 * da_w] = qk_norm_rope(qk[:, da_w:2 * da_w], gk_ref[...], 1.0)
    sb = _dot(h, win_ref[:, 0:o])
    out_ref[:, 0:sb_w] = (sb[:, 0:sb_w] * (scale * LOG2E)).astype(BF16)
    out_ref[:, sb_w:o] = sb[:, sb_w:o].astype(BF16)
    out_ref[:, o + 2 * da_w:o + 3 * da_w] = _dot(h, win_ref[:, o + 2 * da_w:o + 3 * da_w]).astype(BF16)


def _stack_masked(q, n_parts):
    lane = lax.broadcasted_iota(jnp.int32, q.shape, 1)
    zero = jnp.zeros_like(q)
    return jnp.concatenate(
        [jnp.where((lane >= p * HEAD_DIM) & (lane < (p + 1) * HEAD_DIM), q, zero)
         for p in range(n_parts)], axis=0)


def _sb_kernel(q_ref, k_ref, v_ref, o_ref, qs_ref, acc_ref, run_ref, *, tq, n_grp, n_seg):
    first_tile = pl.program_id(2) * n_seg
    m2 = 2 * tq
    r2 = lax.broadcasted_iota(jnp.int32, (tq, tq), 0)
    c2 = lax.broadcasted_iota(jnp.int32, (tq, tq), 1)
    tri = jnp.where(r2 > c2, 1.0, 0.0).astype(BF16)
    lanes = lambda g: slice(g * LANES, (g + 1) * LANES)
    rows = lambda seg: slice(seg * tq, (seg + 1) * tq)

    def block(seg, g, blk, run, diagonal=False, gate=None):
        ks = pl.multiple_of(blk * tq, tq)
        z = _dot_nt(qs_ref[seg, g], k_ref[pl.ds(ks, tq), lanes(g)])
        if diagonal:
            row = lax.broadcasted_iota(jnp.int32, (m2, tq), 0)
            col = lax.broadcasted_iota(jnp.int32, (m2, tq), 1)
            z = jnp.where(col < jnp.where(row >= tq, row - tq, row), z, NEG_INF)
        if gate is not None:
            z = jnp.where(gate > 0.0, z, NEG_INF)
        p = jnp.maximum(z, jnp.log2(1.0 + jnp.exp2(jnp.minimum(z, SB_EXP2_MAX))))
        after = _dot(p.astype(BF16), tri)
        spent = after + jnp.concatenate([run] * (tq // LANES), axis=1)
        run = run + jnp.broadcast_to(after[:, 0:1] + p[:, 0:1], (m2, LANES))
        a = jnp.exp2((z - p) - spent)
        return run, _dot(a.astype(BF16), v_ref[pl.ds(ks, tq), lanes(g)])

    zeros = jnp.zeros((m2, LANES), F32)
    least0 = []
    diag = {}
    for seg in range(n_seg):
        for g in range(n_grp):
            qs_ref[seg, g] = _stack_masked(q_ref[rows(seg), lanes(g)], 2)
            diag[seg, g] = block(seg, g, first_tile + seg, zeros, diagonal=True)
    for seg in range(n_seg):
        i = first_tile + seg
        least = None
        for g in range(n_grp):
            run, out = diag[seg, g]
            if seg == 0:
                run, out2 = block(seg, g, jnp.maximum(i - 1, 0), run,
                                  gate=jnp.where(i > 0, 1.0, 0.0).astype(F32))
            else:
                run, out2 = block(seg, g, i - 1, run)
            run_ref[seg, g] = run
            acc_ref[seg, g] = out + out2
            least = jnp.min(run) if least is None else jnp.minimum(least, jnp.min(run))
        least0.append(least)

    def more(carry):
        blk, least = carry
        return jnp.logical_and(blk >= 0, least < SB_STOP_LOG2)

    for seg in range(n_seg):
        def step(carry, seg=seg):
            blk, _ = carry
            least = None
            for g in range(n_grp):
                run, out = block(seg, g, blk, run_ref[seg, g])
                run_ref[seg, g] = run
                acc_ref[seg, g] += out
                least = jnp.min(run) if least is None else jnp.minimum(least, jnp.min(run))
            return blk - 1, least

        lax.while_loop(more, step, (first_tile + seg - 2, least0[seg]))

    lane = lax.broadcasted_iota(jnp.int32, (tq, LANES), 1)
    for seg in range(n_seg):
        for g in range(n_grp):
            o_ref[rows(seg), lanes(g)] = jnp.where(
                lane < HEAD_DIM, acc_ref[seg, g, 0:tq, :], acc_ref[seg, g, tq:m2, :]
            ).astype(o_ref.dtype)


def _da_flat_kernel(q_ref, k_ref, v_ref, lq1_ref, lk1_ref, lq2_ref, lk2_ref, gsub_ref, *refs,
                    tq, n_tiles, n_cast, lambda_init):
    w32_refs, o_ref, w16_refs = refs[:n_cast], refs[n_cast], refs[n_cast + 1:2 * n_cast + 1]
    vext_ref, qs_ref, s_ref, m_ref, l_ref, acc_ref = refs[2 * n_cast + 1:]
    for w32, w16 in zip(w32_refs, w16_refs):
        w16[...] = w32[...].astype(w16.dtype)

    m2 = 2 * tq
    vext_ref[:, 0:LANES] = v_ref[...]
    vext_ref[:, LANES:2 * LANES] = jnp.ones(v_ref.shape, v_ref.dtype)
    lam = (jnp.exp(jnp.sum(lq1_ref[...] * lk1_ref[...], axis=-1, keepdims=True))
           - jnp.exp(jnp.sum(lq2_ref[...] * lk2_ref[...], axis=-1, keepdims=True))
           + lambda_init)
    steps = [(i, blk) for i in range(n_tiles) for blk in range(i + 1)]
    rows = lambda j: slice(j * tq, (j + 1) * tq)

    h = tq // 2
    quarter = (slice(0, h), slice(h, tq), slice(tq, tq + h), slice(tq + h, m2))
    upper = lambda x: jnp.concatenate([x[quarter[0]], x[quarter[2]]], axis=0)
    lower = lambda x: jnp.concatenate([x[quarter[1]], x[quarter[3]]], axis=0)
    rowmax = lambda s: jnp.broadcast_to(jnp.max(s, axis=1, keepdims=True), (s.shape[0], LANES))

    def scores(t):
        i, blk = steps[t]
        if blk == 0:
            qs_ref[i % 2] = _stack_masked(q_ref[rows(i), :], 2)
        q = qs_ref[i % 2]
        if blk < i:
            s = _dot_nt(q, k_ref[rows(blk), :])
            s_ref[t % 2] = s
        else:
            k0 = blk * tq
            s_ref[t % 2, :, 0:h] = _dot_nt(q, k_ref[k0:k0 + h, :])
            s2 = _dot_nt(lower(q), k_ref[k0 + h:k0 + tq, :])
            s_ref[t % 2, quarter[1], h:tq] = s2[0:h]
            s_ref[t % 2, quarter[3], h:tq] = s2[h:tq]

    def update(s, m_cur, v, prev):
        m_new = m_cur if prev is None else jnp.maximum(prev[0], m_cur)
        p = jnp.exp2(s - jnp.concatenate([m_new] * (s.shape[1] // LANES), axis=1))
        pv = _dot(p.astype(BF16), v)
        if prev is None:
            return m_new, pv[:, LANES:], pv[:, :LANES]
        alpha = jnp.exp2(prev[0] - m_new)
        return m_new, alpha * prev[1] + pv[:, LANES:], alpha * prev[2] + pv[:, :LANES]

    def absorb(t):
        i, blk = steps[t]
        prev = None if blk == 0 else (m_ref[...], l_ref[...], acc_ref[...])
        if blk < i:
            s = s_ref[t % 2]
            m_ref[...], l_ref[...], acc_ref[...] = update(
                s, rowmax(s), vext_ref[rows(blk), :], prev)
            return
        k0 = blk * tq
        row = lax.broadcasted_iota(jnp.int32, (tq, h), 0)
        col = lax.broadcasted_iota(jnp.int32, (tq, h), 1)
        seen = (col // CHUNK) <= (jnp.where(row >= h, row - h, row) // CHUNK)
        s1 = s_ref[t % 2, :, 0:h]
        s_up = jnp.where(seen, upper(s1), NEG_INF)
        s2 = jnp.concatenate([s_ref[t % 2, quarter[1], h:tq], s_ref[t % 2, quarter[3], h:tq]],
                             axis=0)
        s_lo = jnp.concatenate([lower(s1), jnp.where(seen, s2, NEG_INF)], axis=1)
        _, l_up, acc_up = update(s_up, rowmax(s_up), vext_ref[k0:k0 + h, :],
                                 None if prev is None else tuple(upper(x) for x in prev))
        _, l_lo, acc_lo = update(s_lo, rowmax(s_lo), vext_ref[k0:k0 + tq, :],
                                 None if prev is None else tuple(lower(x) for x in prev))
        res_up, res_lo = acc_up / l_up, acc_lo / l_lo
        res = [jnp.concatenate([res_up[c * h:(c + 1) * h], res_lo[c * h:(c + 1) * h]], axis=0)
               for c in range(2)]
        o = res[0] - lam * res[1]
        ms = jnp.mean(o * o, axis=-1, keepdims=True)
        o = o * lax.rsqrt(ms + EPS) * gsub_ref[...] * (1.0 - lambda_init)
        o_ref[rows(i), :] = o.astype(o_ref.dtype)

    scores(0)
    for t in range(len(steps)):
        if t + 1 < len(steps):
            scores(t + 1)
        absorb(t)


def _post_kernel(x_ref, oa_ref, ob_ref, gmix_ref, wgate_ref, bgate_ref, wa_ref, wb_ref, wout_ref,
                 gffn_ref, wfg_ref, wfu_ref, wfd_ref, out_ref, *, d_model):
    x = x_ref[...]
    ms = jnp.mean(x * x, axis=-1, keepdims=True)
    h = (x * lax.rsqrt(ms + EPS) * gmix_ref[...]).astype(BF16)
    gates = jax.nn.sigmoid(_dot(h, wgate_ref[...]) + bgate_ref[...])
    br_a = _dot(oa_ref[...], wa_ref[...])
    br_b = _dot(ob_ref[...], wb_ref[...])
    merged = gates[:, :d_model] * br_a + gates[:, d_model:] * br_b
    x1 = x + _dot(merged.astype(BF16), wout_ref[...])

    ms2 = jnp.mean(x1 * x1, axis=-1, keepdims=True)
    h2 = (x1 * lax.rsqrt(ms2 + EPS) * gffn_ref[...]).astype(BF16)
    fg = _dot(h2, wfg_ref[...])
    fu = _dot(h2, wfu_ref[...])
    ff = (fg * jax.nn.sigmoid(fg)) * fu
    out_ref[...] = x1 + _dot(ff.astype(BF16), wfd_ref[...])


def _rope_tables(seq):
    half = ROPE_DIM // 2
    pos = np.arange(seq, dtype=np.float64)
    inv_freq = ROPE_THETA ** (-np.arange(0, ROPE_DIM, 2, dtype=np.float64) / ROPE_DIM)
    ang = pos[:, None] * inv_freq[None, :]
    cos, sin = np.cos(ang), np.sin(ang)
    ones = np.ones((seq, HEAD_DIM - ROPE_DIM))
    zeros = np.zeros((seq, HEAD_DIM - ROPE_DIM))
    zh = np.zeros((seq, half))
    ra = np.concatenate([cos, cos, ones], axis=1)
    rm = np.concatenate([-sin, zh, zeros], axis=1)
    rp = np.concatenate([zh, sin, zeros], axis=1)
    tile = lambda t: jnp.asarray(np.tile(t, (1, LANES // HEAD_DIM)), dtype=F32)
    return tile(ra), tile(rm), tile(rp)


def _tiles(seq):
    tm, tp, tq_sb, tq_da = 1024, 512, 256, 512
    for t in (tm, tp, tq_sb, tq_da):
        assert seq % t == 0 and t % LANES == 0 and t % CHUNK == 0
    return tm, tp, tq_sb, tq_da


def kernel(x, g_mix, w_in, g_q, g_k, lam_q1, lam_k1, lam_q2, lam_k2, g_sub, w_branch_a, w_branch_b,
           w_gate, b_gate, w_out, g_ffn, w_ffn_gate, w_ffn_up, w_ffn_down):
    b, s, d = x.shape
    depth = g_mix.shape[0]
    sb_w = SB_HEADS * HEAD_DIM
    da_w = DA_HEADS * 2 * HEAD_DIM
    in_w = 3 * sb_w + 3 * da_w
    n = b * s

    tm, tp, tq_sb, tq_da = _tiles(s)

    ra, rm, rp = _rope_tables(s)
    grp = np.arange(da_w) // HEAD_DIM
    gmat = jnp.asarray(np.where(grp[:, None] == grp[None, :], 1.0 / HEAD_DIM, 0.0), dtype=BF16)
    tile_g = lambda g: jnp.tile(g, da_w // HEAD_DIM)[None, :]
    vmem_full = pl.BlockSpec(memory_space=pltpu.VMEM)
    spt = s // tm

    xf = x.reshape(n, d)
    for layer in range(depth):
        lambda_init = 0.8 - 0.6 * math.exp(-0.3 * layer)
        gmix = g_mix[layer][None, :]

        proj = pl.pallas_call(
            functools.partial(_proj_kernel, sb_w=sb_w, da_w=da_w),
            grid=(n // tm,),
            in_specs=[
                pl.BlockSpec((tm, d), lambda t: (t, 0)),
                vmem_full, vmem_full, vmem_full, vmem_full, vmem_full,
                pl.BlockSpec((tm, LANES), lambda t: (t % spt, 0)),
                pl.BlockSpec((tm, LANES), lambda t: (t % spt, 0)),
                pl.BlockSpec((tm, LANES), lambda t: (t % spt, 0)),
            ],
            out_specs=pl.BlockSpec((tm, in_w), lambda t: (t, 0)),
            out_shape=jax.ShapeDtypeStruct((n, in_w), BF16),
            compiler_params=pltpu.CompilerParams(
                dimension_semantics=("arbitrary",), vmem_limit_bytes=VMEM_LIMIT),
            name="proj",
        )(xf, gmix, w_in[layer].astype(BF16), tile_g(g_q[layer]), tile_g(g_k[layer]), gmat,
          ra, rm, rp)

        cb = lambda off: off // LANES
        tq, gw, ns = tq_sb, SB_GROUPS * LANES, SB_TILES
        nq = s // (ns * tq)
        o_a = pl.pallas_call(
            functools.partial(_sb_kernel, tq=tq, n_grp=SB_GROUPS, n_seg=ns),
            grid=(b, sb_w // gw, nq),
            in_specs=[
                pl.BlockSpec((ns * tq, gw), lambda bi, j, i, nq=nq: (bi * nq + i, j)),
                pl.BlockSpec((s, gw), lambda bi, j, i: (bi, sb_w // gw + j)),
                pl.BlockSpec((s, gw), lambda bi, j, i: (bi, 2 * sb_w // gw + j)),
            ],
            out_specs=pl.BlockSpec((ns * tq, gw), lambda bi, j, i, nq=nq: (bi * nq + i, j)),
            out_shape=jax.ShapeDtypeStruct((n, sb_w), BF16),
            scratch_shapes=[pltpu.VMEM((ns, SB_GROUPS, 2 * tq, LANES), BF16),
                            pltpu.VMEM((ns, SB_GROUPS, 2 * tq, LANES), F32),
                            pltpu.VMEM((ns, SB_GROUPS, 2 * tq, LANES), F32)],
            compiler_params=pltpu.CompilerParams(
                dimension_semantics=("arbitrary", "arbitrary", "arbitrary"),
                vmem_limit_bytes=VMEM_LIMIT),
            name="sb_attn",
        )(proj, proj, proj)

        tq = tq_da
        post_w = [w_gate[layer], w_branch_a[layer], w_branch_b[layer], w_out[layer],
                  w_ffn_gate[layer], w_ffn_up[layer], w_ffn_down[layer]]
        n_steps = b * DA_HEADS
        for w in post_w:
            assert w.shape[0] % (16 * n_steps) == 0
        w_spec = lambda w: pl.BlockSpec((w.shape[0] // n_steps, w.shape[1]),
                                        lambda bi, j: (bi * DA_HEADS + j, 0))
        o_b, *post_w16 = pl.pallas_call(
            functools.partial(_da_flat_kernel, tq=tq, n_tiles=s // tq, n_cast=len(post_w),
                              lambda_init=lambda_init),
            grid=(b, DA_HEADS),
            in_specs=[
                pl.BlockSpec((s, LANES), lambda bi, j: (bi, cb(3 * sb_w) + j)),
                pl.BlockSpec((s, LANES), lambda bi, j: (bi, cb(3 * sb_w + da_w) + j)),
                pl.BlockSpec((s, LANES), lambda bi, j: (bi, cb(3 * sb_w + 2 * da_w) + j)),
                vmem_full, vmem_full, vmem_full, vmem_full, vmem_full,
            ] + [w_spec(w) for w in post_w],
            out_specs=[pl.BlockSpec((s, LANES), lambda bi, j: (bi, j))]
            + [w_spec(w) for w in post_w],
            out_shape=[jax.ShapeDtypeStruct((n, da_w), BF16)]
            + [jax.ShapeDtypeStruct(w.shape, BF16) for w in post_w],
            scratch_shapes=[pltpu.VMEM((s, 2 * LANES), BF16),
                            pltpu.VMEM((2, 2 * tq, LANES), BF16),
                            pltpu.VMEM((2, 2 * tq, tq), F32)]
            + [pltpu.VMEM((2 * tq, LANES), F32)] * 3,
            compiler_params=pltpu.CompilerParams(
                dimension_semantics=("arbitrary", "arbitrary"),
                vmem_limit_bytes=VMEM_LIMIT),
            name="da_attn",
        )(proj, proj, proj, lam_q1[layer][None, :], lam_k1[layer][None, :],
          lam_q2[layer][None, :], lam_k2[layer][None, :], g_sub[layer][None, :], *post_w)
        wgate16, wa16, wb16, wout16, wfg16, wfu16, wfd16 = post_w16

        xf = pl.pallas_call(
            functools.partial(_post_kernel, d_model=d),
            grid=(n // tp,),
            in_specs=[
                pl.BlockSpec((tp, d), lambda t: (t, 0)),
                pl.BlockSpec((tp, sb_w), lambda t: (t, 0)),
                pl.BlockSpec((tp, da_w), lambda t: (t, 0)),
            ] + [vmem_full] * 10,
            out_specs=pl.BlockSpec((tp, d), lambda t: (t, 0)),
            out_shape=jax.ShapeDtypeStruct((n, d), F32),
            compiler_params=pltpu.CompilerParams(
                dimension_semantics=("arbitrary",), vmem_limit_bytes=VMEM_LIMIT),
            name="post",
        )(xf, o_a, o_b, gmix, wgate16, b_gate[layer][None, :], wa16, wb16, wout16,
          g_ffn[layer][None, :], wfg16, wfu16, wfd16)
    return xf.reshape(b, s, d)
```

```python
import functools
import math

import jax
import jax.numpy as jnp
import numpy as np
from jax import lax
from jax.experimental import pallas as pl
from jax.experimental.pallas import tpu as pltpu

F32 = jnp.float32
BF16 = jnp.bfloat16

CHUNK = 64
SB_HEADS = 8
DA_HEADS = 4
HEAD_DIM = 64
ROPE_THETA = 500000.0
ROPE_DIM = HEAD_DIM // 4
EPS = 1e-6
NEG_INF = -1e30
LOG2E = math.log2(math.e)
SB_GROUPS = 4
SB_TILES = 4
SB_STOP_LOG2 = 152.0
SB_EXP2_MAX = 126.0
LANES = 128
VMEM_LIMIT = 56 * 1024 * 1024


def _dot(a, b):
    return jnp.dot(a, b, preferred_element_type=F32)


def _dot_nt(a, b):
    return lax.dot_general(a, b, (((1,), (1,)), ((), ())), preferred_element_type=F32)


def _proj_kernel(x_ref, gmix_ref, win_ref, gq_ref, gk_ref, gmat_ref,
                 ra_ref, rm_ref, rp_ref, out_ref, *, sb_w, da_w):
    x = x_ref[...]
    ms = jnp.mean(x * x, axis=-1, keepdims=True)
    h = (x * lax.rsqrt(ms + EPS) * gmix_ref[...]).astype(BF16)
    scale = HEAD_DIM ** -0.5
    o = 3 * sb_w
    ra, rm, rp = ra_ref[...], rm_ref[...], rp_ref[...]

    def qk_norm_rope(t, g, mult):
        msq = _dot((t * t).astype(BF16), gmat_ref[...])
        tn = t * lax.rsqrt(msq + EPS) * g
        cols = []
        for j in range(da_w // LANES):
            c = tn[:, j * LANES:(j + 1) * LANES]
            r = (c * ra + pltpu.roll(c, LANES - ROPE_DIM // 2, 1) * rm
                 + pltpu.roll(c, ROPE_DIM // 2, 1) * rp)
            cols.append((r * mult).astype(BF16))
        return jnp.concatenate(cols, axis=1)

    qk = _dot(h, win_ref[:, o:o + 2 * da_w])
    out_ref[:, o:o + da_w] = qk_norm_rope(qk[:, 0:da_w], gq_ref[...], scale * LOG2E)
    out_ref[:, o + da_w:o + 2 * da_w] = qk_norm_rope(qk[:, da_w:2 * da_w], gk_ref[...], 1.0)
    sb = _dot(h, win_ref[:, 0:o])
    out_ref[:, 0:sb_w] = (sb[:, 0:sb_w] * (scale * LOG2E)).astype(BF16)
    out_ref[:, sb_w:o] = sb[:, sb_w:o].astype(BF16)
    out_ref[:, o + 2 * da_w:o + 3 * da_w] = _dot(h, win_ref[:, o + 2 * da_w:o + 3 * da_w]).astype(BF16)


def _stack_masked(q, n_parts):
    lane = lax.broadcasted_iota(jnp.int32, q.shape, 1)
    zero = jnp.zeros_like(q)
    return jnp.concatenate(
        [jnp.where((lane >= p * HEAD_DIM) & (lane < (p + 1) * HEAD_DIM), q, zero)
         for p in range(n_parts)], axis=0)


def _sb_kernel(q_ref, k_ref, v_ref, o_ref, qs_ref, acc_ref, run_ref, *, tq, n_grp, n_seg):
    first_tile = pl.program_id(2) * n_seg
    m2 = 2 * tq
    r2 = lax.broadcasted_iota(jnp.int32, (tq, tq), 0)
    c2 = lax.broadcasted_iota(jnp.int32, (tq, tq), 1)
    tri = jnp.where(r2 > c2, 1.0, 0.0).astype(BF16)
    lanes = lambda g: slice(g * LANES, (g + 1) * LANES)
    rows = lambda seg: slice(seg * tq, (seg + 1) * tq)

    def block_head(seg, g, blk, diagonal=False, gate=None):
        ks = pl.multiple_of(blk * tq, tq)
        z = _dot_nt(qs_ref[seg, g], k_ref[pl.ds(ks, tq), lanes(g)])
        if diagonal:
            row = lax.broadcasted_iota(jnp.int32, (m2, tq), 0)
            col = lax.broadcasted_iota(jnp.int32, (m2, tq), 1)
            z = jnp.where(col < jnp.where(row >= tq, row - tq, row), z, NEG_INF)
        if gate is not None:
            z = jnp.where(gate > 0.0, z, NEG_INF)
        p = jnp.maximum(z, jnp.log2(1.0 + jnp.exp2(jnp.minimum(z, SB_EXP2_MAX))))
        after = _dot(p.astype(BF16), tri)
        return g, ks, z, p, after

    def block_tail(head, run):
        g, ks, z, p, after = head
        spent = after + jnp.concatenate([run] * (tq // LANES), axis=1)
        run = run + jnp.broadcast_to(after[:, 0:1] + p[:, 0:1], (m2, LANES))
        a = jnp.exp2((z - p) - spent)
        return run, _dot(a.astype(BF16), v_ref[pl.ds(ks, tq), lanes(g)])

    def block(seg, g, blk, run):
        return block_tail(block_head(seg, g, blk), run)

    zeros = jnp.zeros((m2, LANES), F32)
    least0 = []
    for seg in range(n_seg):
        i = first_tile + seg
        for g in range(n_grp):
            qs_ref[seg, g] = _stack_masked(q_ref[rows(seg), lanes(g)], 2)
        diag = [block_head(seg, g, i, diagonal=True) for g in range(n_grp)]
        if seg == 0:
            gate = jnp.where(i > 0, 1.0, 0.0).astype(F32)
            prev = [block_head(seg, g, jnp.maximum(i - 1, 0), gate=gate) for g in range(n_grp)]
        else:
            prev = [block_head(seg, g, i - 1) for g in range(n_grp)]
        done = [block_tail(diag[g], zeros) for g in range(n_grp)]
        least = None
        for g in range(n_grp):
            run, out2 = block_tail(prev[g], done[g][0])
            run_ref[seg, g] = run
            acc_ref[seg, g] = done[g][1] + out2
            least = jnp.min(run) if least is None else jnp.minimum(least, jnp.min(run))
        least0.append(least)

    def more(carry):
        blk, least = carry
        return jnp.logical_and(blk >= 0, least < SB_STOP_LOG2)

    for seg in range(n_seg):
        def step(carry, seg=seg):
            blk, _ = carry
            least = None
            for g in range(n_grp):
                run, out = block(seg, g, blk, run_ref[seg, g])
                run_ref[seg, g] = run
                acc_ref[seg, g] += out
                least = jnp.min(run) if least is None else jnp.minimum(least, jnp.min(run))
            return blk - 1, least

        lax.while_loop(more, step, (first_tile + seg - 2, least0[seg]))

    lane = lax.broadcasted_iota(jnp.int32, (tq, LANES), 1)
    for seg in range(n_seg):
        for g in range(n_grp):
            o_ref[rows(seg), lanes(g)] = jnp.where(
                lane < HEAD_DIM, acc_ref[seg, g, 0:tq, :], acc_ref[seg, g, tq:m2, :]
            ).astype(o_ref.dtype)


def _da_flat_kernel(q_ref, k_ref, v_ref, lq1_ref, lk1_ref, lq2_ref, lk2_ref, gsub_ref, *refs,
                    tq, n_tiles, n_cast, lambda_init):
    w32_refs, o_ref, w16_refs = refs[:n_cast], refs[n_cast], refs[n_cast + 1:2 * n_cast + 1]
    vext_ref, qs_ref, s_ref, m_ref, l_ref, acc_ref = refs[2 * n_cast + 1:]
    for w32, w16 in zip(w32_refs, w16_refs):
        w16[...] = w32[...].astype(w16.dtype)

    m2 = 2 * tq
    vext_ref[:, 0:LANES] = v_ref[...]
    vext_ref[:, LANES:2 * LANES] = jnp.ones(v_ref.shape, v_ref.dtype)
    lam = (jnp.exp(jnp.sum(lq1_ref[...] * lk1_ref[...], axis=-1, keepdims=True))
           - jnp.exp(jnp.sum(lq2_ref[...] * lk2_ref[...], axis=-1, keepdims=True))
           + lambda_init)
    steps = [(i, blk) for i in range(n_tiles) for blk in range(i + 1)]
    rows = lambda j: slice(j * tq, (j + 1) * tq)

    h = tq // 2
    quarter = (slice(0, h), slice(h, tq), slice(tq, tq + h), slice(tq + h, m2))
    upper = lambda x: jnp.concatenate([x[quarter[0]], x[quarter[2]]], axis=0)
    lower = lambda x: jnp.concatenate([x[quarter[1]], x[quarter[3]]], axis=0)
    rowmax = lambda s: jnp.broadcast_to(jnp.max(s, axis=1, keepdims=True), (s.shape[0], LANES))

    def scores(t):
        i, blk = steps[t]
        if blk == 0:
            qs_ref[i % 2] = _stack_masked(q_ref[rows(i), :], 2)
        q = qs_ref[i % 2]
        if blk < i:
            s = _dot_nt(q, k_ref[rows(blk), :])
            s_ref[t % 2] = s
        else:
            k0 = blk * tq
            s_ref[t % 2, :, 0:h] = _dot_nt(q, k_ref[k0:k0 + h, :])
            s2 = _dot_nt(lower(q), k_ref[k0 + h:k0 + tq, :])
            s_ref[t % 2, quarter[1], h:tq] = s2[0:h]
            s_ref[t % 2, quarter[3], h:tq] = s2[h:tq]

    def update(s, m_cur, v, prev):
        m_new = m_cur if prev is None else jnp.maximum(prev[0], m_cur)
        p = jnp.exp2(s - jnp.concatenate([m_new] * (s.shape[1] // LANES), axis=1))
        pv = _dot(p.astype(BF16), v)
        if prev is None:
            return m_new, pv[:, LANES:], pv[:, :LANES]
        alpha = jnp.exp2(prev[0] - m_new)
        return m_new, alpha * prev[1] + pv[:, LANES:], alpha * prev[2] + pv[:, :LANES]

    def absorb(t):
        i, blk = steps[t]
        prev = None if blk == 0 else (m_ref[...], l_ref[...], acc_ref[...])
        if blk < i:
            s = s_ref[t % 2]
            m_ref[...], l_ref[...], acc_ref[...] = update(
                s, rowmax(s), vext_ref[rows(blk), :], prev)
            return
        k0 = blk * tq
        row = lax.broadcasted_iota(jnp.int32, (tq, h), 0)
        col = lax.broadcasted_iota(jnp.int32, (tq, h), 1)
        seen = (col // CHUNK) <= (jnp.where(row >= h, row - h, row) // CHUNK)
        s1 = s_ref[t % 2, :, 0:h]
        s_up = jnp.where(seen, upper(s1), NEG_INF)
        s2 = jnp.concatenate([s_ref[t % 2, quarter[1], h:tq], s_ref[t % 2, quarter[3], h:tq]],
                             axis=0)
        s_lo = jnp.concatenate([lower(s1), jnp.where(seen, s2, NEG_INF)], axis=1)
        _, l_up, acc_up = update(s_up, rowmax(s_up), vext_ref[k0:k0 + h, :],
                                 None if prev is None else tuple(upper(x) for x in prev))
        _, l_lo, acc_lo = update(s_lo, rowmax(s_lo), vext_ref[k0:k0 + tq, :],
                                 None if prev is None else tuple(lower(x) for x in prev))
        res_up, res_lo = acc_up / l_up, acc_lo / l_lo
        res = [jnp.concatenate([res_up[c * h:(c + 1) * h], res_lo[c * h:(c + 1) * h]], axis=0)
               for c in range(2)]
        o = res[0] - lam * res[1]
        ms = jnp.mean(o * o, axis=-1, keepdims=True)
        o = o * lax.rsqrt(ms + EPS) * gsub_ref[...] * (1.0 - lambda_init)
        o_ref[rows(i), :] = o.astype(o_ref.dtype)

    scores(0)
    for t in range(len(steps)):
        if t + 1 < len(steps):
            scores(t + 1)
        absorb(t)


def _post_kernel(x_ref, oa_ref, ob_ref, gmix_ref, wgate_ref, bgate_ref, wa_ref, wb_ref, wout_ref,
                 gffn_ref, wfg_ref, wfu_ref, wfd_ref, out_ref, *, d_model):
    x = x_ref[...]
    ms = jnp.mean(x * x, axis=-1, keepdims=True)
    h = (x * lax.rsqrt(ms + EPS) * gmix_ref[...]).astype(BF16)
    gates = jax.nn.sigmoid(_dot(h, wgate_ref[...]) + bgate_ref[...])
    br_a = _dot(oa_ref[...], wa_ref[...])
    br_b = _dot(ob_ref[...], wb_ref[...])
    merged = gates[:, :d_model] * br_a + gates[:, d_model:] * br_b
    x1 = x + _dot(merged.astype(BF16), wout_ref[...])

    ms2 = jnp.mean(x1 * x1, axis=-1, keepdims=True)
    h2 = (x1 * lax.rsqrt(ms2 + EPS) * gffn_ref[...]).astype(BF16)
    fg = _dot(h2, wfg_ref[...])
    fu = _dot(h2, wfu_ref[...])
    ff = (fg * jax.nn.sigmoid(fg)) * fu
    out_ref[...] = x1 + _dot(ff.astype(BF16), wfd_ref[...])


def _rope_tables(seq):
    half = ROPE_DIM // 2
    pos = np.arange(seq, dtype=np.float64)
    inv_freq = ROPE_THETA ** (-np.arange(0, ROPE_DIM, 2, dtype=np.float64) / ROPE_DIM)
    ang = pos[:, None] * inv_freq[None, :]
    cos, sin = np.cos(ang), np.sin(ang)
    ones = np.ones((seq, HEAD_DIM - ROPE_DIM))
    zeros = np.zeros((seq, HEAD_DIM - ROPE_DIM))
    zh = np.zeros((seq, half))
    ra = np.concatenate([cos, cos, ones], axis=1)
    rm = np.concatenate([-sin, zh, zeros], axis=1)
    rp = np.concatenate([zh, sin, zeros], axis=1)
    tile = lambda t: jnp.asarray(np.tile(t, (1, LANES // HEAD_DIM)), dtype=F32)
    return tile(ra), tile(rm), tile(rp)


def _tiles(seq):
    tm, tp, tq_sb, tq_da = 1024, 512, 256, 512
    for t in (tm, tp, tq_sb, tq_da):
        assert seq % t == 0 and t % LANES == 0 and t % CHUNK == 0
    return tm, tp, tq_sb, tq_da


def kernel(x, g_mix, w_in, g_q, g_k, lam_q1, lam_k1, lam_q2, lam_k2, g_sub, w_branch_a, w_branch_b,
           w_gate, b_gate, w_out, g_ffn, w_ffn_gate, w_ffn_up, w_ffn_down):
    b, s, d = x.shape
    depth = g_mix.shape[0]
    sb_w = SB_HEADS * HEAD_DIM
    da_w = DA_HEADS * 2 * HEAD_DIM
    in_w = 3 * sb_w + 3 * da_w
    n = b * s

    tm, tp, tq_sb, tq_da = _tiles(s)

    ra, rm, rp = _rope_tables(s)
    grp = np.arange(da_w) // HEAD_DIM
    gmat = jnp.asarray(np.where(grp[:, None] == grp[None, :], 1.0 / HEAD_DIM, 0.0), dtype=BF16)
    tile_g = lambda g: jnp.tile(g, da_w // HEAD_DIM)[None, :]
    vmem_full = pl.BlockSpec(memory_space=pltpu.VMEM)
    spt = s // tm

    xf = x.reshape(n, d)
    for layer in range(depth):
        lambda_init = 0.8 - 0.6 * math.exp(-0.3 * layer)
        gmix = g_mix[layer][None, :]

        proj = pl.pallas_call(
            functools.partial(_proj_kernel, sb_w=sb_w, da_w=da_w),
            grid=(n // tm,),
            in_specs=[
                pl.BlockSpec((tm, d), lambda t: (t, 0)),
                vmem_full, vmem_full, vmem_full, vmem_full, vmem_full,
                pl.BlockSpec((tm, LANES), lambda t: (t % spt, 0)),
                pl.BlockSpec((tm, LANES), lambda t: (t % spt, 0)),
                pl.BlockSpec((tm, LANES), lambda t: (t % spt, 0)),
            ],
            out_specs=pl.BlockSpec((tm, in_w), lambda t: (t, 0)),
            out_shape=jax.ShapeDtypeStruct((n, in_w), BF16),
            compiler_params=pltpu.CompilerParams(
                dimension_semantics=("arbitrary",), vmem_limit_bytes=VMEM_LIMIT),
            name="proj",
        )(xf, gmix, w_in[layer].astype(BF16), tile_g(g_q[layer]), tile_g(g_k[layer]), gmat,
          ra, rm, rp)

        cb = lambda off: off // LANES
        tq, gw, ns = tq_sb, SB_GROUPS * LANES, SB_TILES
        nq = s // (ns * tq)
        o_a = pl.pallas_call(
            functools.partial(_sb_kernel, tq=tq, n_grp=SB_GROUPS, n_seg=ns),
            grid=(b, sb_w // gw, nq),
            in_specs=[
                pl.BlockSpec((ns * tq, gw), lambda bi, j, i, nq=nq: (bi * nq + i, j)),
                pl.BlockSpec((s, gw), lambda bi, j, i: (bi, sb_w // gw + j)),
                pl.BlockSpec((s, gw), lambda bi, j, i: (bi, 2 * sb_w // gw + j)),
            ],
            out_specs=pl.BlockSpec((ns * tq, gw), lambda bi, j, i, nq=nq: (bi * nq + i, j)),
            out_shape=jax.ShapeDtypeStruct((n, sb_w), BF16),
            scratch_shapes=[pltpu.VMEM((ns, SB_GROUPS, 2 * tq, LANES), BF16),
                            pltpu.VMEM((ns, SB_GROUPS, 2 * tq, LANES), F32),
                            pltpu.VMEM((ns, SB_GROUPS, 2 * tq, LANES), F32)],
            compiler_params=pltpu.CompilerParams(
                dimension_semantics=("arbitrary", "arbitrary", "arbitrary"),
                vmem_limit_bytes=VMEM_LIMIT),
            name="sb_attn",
        )(proj, proj, proj)

        tq = tq_da
        post_w = [w_gate[layer], w_branch_a[layer], w_branch_b[layer], w_out[layer],
                  w_ffn_gate[layer], w_ffn_up[layer], w_ffn_down[layer]]
        n_steps = b * DA_HEADS
        for w in post_w:
            assert w.shape[0] % (16 * n_steps) == 0
        w_spec = lambda w: pl.BlockSpec((w.shape[0] // n_steps, w.shape[1]),
                                        lambda bi, j: (bi * DA_HEADS + j, 0))
        o_b, *post_w16 = pl.pallas_call(
            functools.partial(_da_flat_kernel, tq=tq, n_tiles=s // tq, n_cast=len(post_w),
                              lambda_init=lambda_init),
            grid=(b, DA_HEADS),
            in_specs=[
                pl.BlockSpec((s, LANES), lambda bi, j: (bi, cb(3 * sb_w) + j)),
                pl.BlockSpec((s, LANES), lambda bi, j: (bi, cb(3 * sb_w + da_w) + j)),
                pl.BlockSpec((s, LANES), lambda bi, j: (bi, cb(3 * sb_w + 2 * da_w) + j)),
                vmem_full, vmem_full, vmem_full, vmem_full, vmem_full,
            ] + [w_spec(w) for w in post_w],
            out_specs=[pl.BlockSpec((s, LANES), lambda bi, j: (bi, j))]
            + [w_spec(w) for w in post_w],
            out_shape=[jax.ShapeDtypeStruct((n, da_w), BF16)]
            + [jax.ShapeDtypeStruct(w.shape, BF16) for w in post_w],
            scratch_shapes=[pltpu.VMEM((s, 2 * LANES), BF16),
                            pltpu.VMEM((2, 2 * tq, LANES), BF16),
                            pltpu.VMEM((2, 2 * tq, tq), F32)]
            + [pltpu.VMEM((2 * tq, LANES), F32)] * 3,
            compiler_params=pltpu.CompilerParams(
                dimension_semantics=("arbitrary", "arbitrary"),
                vmem_limit_bytes=VMEM_LIMIT),
            name="da_attn",
        )(proj, proj, proj, lam_q1[layer][None, :], lam_k1[layer][None, :],
          lam_q2[layer][None, :], lam_k2[layer][None, :], g_sub[layer][None, :], *post_w)
        wgate16, wa16, wb16, wout16, wfg16, wfu16, wfd16 = post_w16

        xf = pl.pallas_call(
            functools.partial(_post_kernel, d_model=d),
            grid=(n // tp,),
            in_specs=[
                pl.BlockSpec((tp, d), lambda t: (t, 0)),
                pl.BlockSpec((tp, sb_w), lambda t: (t, 0)),
                pl.BlockSpec((tp, da_w), lambda t: (t, 0)),
            ] + [vmem_full] * 10,
            out_specs=pl.BlockSpec((tp, d), lambda t: (t, 0)),
            out_shape=jax.ShapeDtypeStruct((n, d), F32),
            compiler_params=pltpu.CompilerParams(
                dimension_semantics=("arbitrary",), vmem_limit_bytes=VMEM_LIMIT),
            name="post",
        )(xf, o_a, o_b, gmix, wgate16, b_gate[layer][None, :], wa16, wb16, wout16,
          g_ffn[layer][None, :], wfg16, wfu16, wfd16)
    return xf.reshape(b, s, d)
```

```python
import functools
import math

import jax
import jax.numpy as jnp
import numpy as np
from jax import lax
from jax.experimental import pallas as pl
from jax.experimental.pallas import tpu as pltpu

F32 = jnp.float32
BF16 = jnp.bfloat16

CHUNK = 64
SB_HEADS = 8
DA_HEADS = 4
HEAD_DIM = 64
ROPE_THETA = 500000.0
ROPE_DIM = HEAD_DIM // 4
EPS = 1e-6
NEG_INF = -1e30
LOG2E = math.log2(math.e)
SB_GROUPS = 4
SB_TILES = 4
SB_STOP_LOG2 = 152.0
SB_EXP2_MAX = 126.0
LANES = 128
VMEM_LIMIT = 56 * 1024 * 1024


def _dot(a, b):
    return jnp.dot(a, b, preferred_element_type=F32)


def _dot_nt(a, b):
    return lax.dot_general(a, b, (((1,), (1,)), ((), ())), preferred_element_type=F32)


def _proj_kernel(x_ref, gmix_ref, win_ref, gq_ref, gk_ref, gmat_ref,
                 ra_ref, rm_ref, rp_ref, out_ref, *, sb_w, da_w):
    x = x_ref[...]
    ms = jnp.mean(x * x, axis=-1, keepdims=True)
    h = (x * lax.rsqrt(ms + EPS) * gmix_ref[...]).astype(BF16)
    scale = HEAD_DIM ** -0.5
    o = 3 * sb_w
    ra, rm, rp = ra_ref[...], rm_ref[...], rp_ref[...]

    def qk_norm_rope(t, g, mult):
        msq = _dot((t * t).astype(BF16), gmat_ref[...])
        tn = t * lax.rsqrt(msq + EPS) * g
        cols = []
        for j in range(da_w // LANES):
            c = tn[:, j * LANES:(j + 1) * LANES]
            r = (c * ra + pltpu.roll(c, LANES - ROPE_DIM // 2, 1) * rm
                 + pltpu.roll(c, ROPE_DIM // 2, 1) * rp)
            cols.append((r * mult).astype(BF16))
        return jnp.concatenate(cols, axis=1)

    qk = _dot(h, win_ref[:, o:o + 2 * da_w])
    out_ref[:, o:o + da_w] = qk_norm_rope(qk[:, 0:da_w], gq_ref[...], scale * LOG2E)
    out_ref[:, o + da_w:o + 2 * da_w] = qk_norm_rope(qk[:, da_w:2 * da_w], gk_ref[...], 1.0)
    sb = _dot(h, win_ref[:, 0:o])
    out_ref[:, 0:sb_w] = (sb[:, 0:sb_w] * (scale * LOG2E)).astype(BF16)
    out_ref[:, sb_w:o] = sb[:, sb_w:o].astype(BF16)
    out_ref[:, o + 2 * da_w:o + 3 * da_w] = _dot(h, win_ref[:, o + 2 * da_w:o + 3 * da_w]).astype(BF16)


def _stack_masked(q, n_parts):
    lane = lax.broadcasted_iota(jnp.int32, q.shape, 1)
    zero = jnp.zeros_like(q)
    return jnp.concatenate(
        [jnp.where((lane >= p * HEAD_DIM) & (lane < (p + 1) * HEAD_DIM), q, zero)
         for p in range(n_parts)], axis=0)


def _sb_kernel(q_ref, k_ref, v_ref, o_ref, qs_ref, acc_ref, run_ref, *, tq, n_grp, n_seg):
    first_tile = pl.program_id(2) * n_seg
    m2 = 2 * tq
    r2 = lax.broadcasted_iota(jnp.int32, (tq, tq), 0)
    c2 = lax.broadcasted_iota(jnp.int32, (tq, tq), 1)
    tri = jnp.where(r2 > c2, 1.0, 0.0).astype(BF16)
    lanes = lambda g: slice(g * LANES, (g + 1) * LANES)
    rows = lambda seg: slice(seg * tq, (seg + 1) * tq)

    def block_head(seg, g, blk, diagonal=False, gate=None):
        ks = pl.multiple_of(blk * tq, tq)
        z = _dot_nt(qs_ref[seg, g], k_ref[pl.ds(ks, tq), lanes(g)])
        if diagonal:
            row = lax.broadcasted_iota(jnp.int32, (m2, tq), 0)
            col = lax.broadcasted_iota(jnp.int32, (m2, tq), 1)
            z = jnp.where(col < jnp.where(row >= tq, row - tq, row), z, NEG_INF)
        if gate is not None:
            z = jnp.where(gate > 0.0, z, NEG_INF)
        p = jnp.maximum(z, jnp.log2(1.0 + jnp.exp2(jnp.minimum(z, SB_EXP2_MAX))))
        after = _dot(p.astype(BF16), tri)
        return g, ks, z, p, after

    def block_tail(head, run):
        g, ks, z, p, after = head
        spent = after + jnp.concatenate([run] * (tq // LANES), axis=1)
        run = run + jnp.broadcast_to(after[:, 0:1] + p[:, 0:1], (m2, LANES))
        a = jnp.exp2((z - p) - spent)
        return run, _dot(a.astype(BF16), v_ref[pl.ds(ks, tq), lanes(g)])

    def block(seg, g, blk, run):
        return block_tail(block_head(seg, g, blk), run)

    zeros = jnp.zeros((m2, LANES), F32)
    least0 = [None] * n_seg
    diag = {}

    def diag_blocks(seg):
        for g in range(n_grp):
            qs_ref[seg, g] = _stack_masked(q_ref[rows(seg), lanes(g)], 2)
            diag[seg, g] = block_tail(
                block_head(seg, g, first_tile + seg, diagonal=True), zeros)

    def prev_blocks(seg):
        i = first_tile + seg
        for g in range(n_grp):
            run, out = diag[seg, g]
            if seg == 0:
                head = block_head(seg, g, jnp.maximum(i - 1, 0),
                                  gate=jnp.where(i > 0, 1.0, 0.0).astype(F32))
            else:
                head = block_head(seg, g, i - 1)
            run, out2 = block_tail(head, run)
            run_ref[seg, g] = run
            acc_ref[seg, g] = out + out2
            least = jnp.min(run)
            least0[seg] = least if least0[seg] is None else jnp.minimum(least0[seg], least)

    for seg in range(n_seg + 1):
        if seg < n_seg:
            diag_blocks(seg)
        if seg > 0:
            prev_blocks(seg - 1)

    def more(carry):
        blk, least = carry
        return jnp.logical_and(blk >= 0, least < SB_STOP_LOG2)

    for seg in range(n_seg):
        def step(carry, seg=seg):
            blk, _ = carry
            least = None
            for g in range(n_grp):
                run, out = block(seg, g, blk, run_ref[seg, g])
                run_ref[seg, g] = run
                acc_ref[seg, g] += out
                least = jnp.min(run) if least is None else jnp.minimum(least, jnp.min(run))
            return blk - 1, least

        lax.while_loop(more, step, (first_tile + seg - 2, least0[seg]))

    lane = lax.broadcasted_iota(jnp.int32, (tq, LANES), 1)
    for seg in range(n_seg):
        for g in range(n_grp):
            o_ref[rows(seg), lanes(g)] = jnp.where(
                lane < HEAD_DIM, acc_ref[seg, g, 0:tq, :], acc_ref[seg, g, tq:m2, :]
            ).astype(o_ref.dtype)


def _da_flat_kernel(q_ref, k_ref, v_ref, lq1_ref, lk1_ref, lq2_ref, lk2_ref, gsub_ref, *refs,
                    tq, n_tiles, n_cast, lambda_init):
    w32_refs, o_ref, w16_refs = refs[:n_cast], refs[n_cast], refs[n_cast + 1:2 * n_cast + 1]
    vext_ref, qs_ref, s_ref, m_ref, l_ref, acc_ref = refs[2 * n_cast + 1:]
    for w32, w16 in zip(w32_refs, w16_refs):
        w16[...] = w32[...].astype(w16.dtype)

    m2 = 2 * tq
    vext_ref[:, 0:LANES] = v_ref[...]
    vext_ref[:, LANES:2 * LANES] = jnp.ones(v_ref.shape, v_ref.dtype)
    lam = (jnp.exp(jnp.sum(lq1_ref[...] * lk1_ref[...], axis=-1, keepdims=True))
           - jnp.exp(jnp.sum(lq2_ref[...] * lk2_ref[...], axis=-1, keepdims=True))
           + lambda_init)
    steps = [(i, blk) for i in range(n_tiles) for blk in range(i + 1)]
    rows = lambda j: slice(j * tq, (j + 1) * tq)

    h = tq // 2
    quarter = (slice(0, h), slice(h, tq), slice(tq, tq + h), slice(tq + h, m2))
    upper = lambda x: jnp.concatenate([x[quarter[0]], x[quarter[2]]], axis=0)
    lower = lambda x: jnp.concatenate([x[quarter[1]], x[quarter[3]]], axis=0)
    rowmax = lambda s: jnp.broadcast_to(jnp.max(s, axis=1, keepdims=True), (s.shape[0], LANES))

    def scores(t):
        i, blk = steps[t]
        if blk == 0:
            qs_ref[i % 2] = _stack_masked(q_ref[rows(i), :], 2)
        q = qs_ref[i % 2]
        if blk < i:
            s = _dot_nt(q, k_ref[rows(blk), :])
            s_ref[t % 2] = s
        else:
            k0 = blk * tq
            s_ref[t % 2, :, 0:h] = _dot_nt(q, k_ref[k0:k0 + h, :])
            s2 = _dot_nt(lower(q), k_ref[k0 + h:k0 + tq, :])
            s_ref[t % 2, quarter[1], h:tq] = s2[0:h]
            s_ref[t % 2, quarter[3], h:tq] = s2[h:tq]

    def update(s, m_cur, v, prev):
        m_new = m_cur if prev is None else jnp.maximum(prev[0], m_cur)
        p = jnp.exp2(s - jnp.concatenate([m_new] * (s.shape[1] // LANES), axis=1))
        pv = _dot(p.astype(BF16), v)
        if prev is None:
            return m_new, pv[:, LANES:], pv[:, :LANES]
        alpha = jnp.exp2(prev[0] - m_new)
        return m_new, alpha * prev[1] + pv[:, LANES:], alpha * prev[2] + pv[:, :LANES]

    def absorb(t):
        i, blk = steps[t]
        prev = None if blk == 0 else (m_ref[...], l_ref[...], acc_ref[...])
        if blk < i:
            s = s_ref[t % 2]
            m_ref[...], l_ref[...], acc_ref[...] = update(
                s, rowmax(s), vext_ref[rows(blk), :], prev)
            return
        k0 = blk * tq
        row = lax.broadcasted_iota(jnp.int32, (tq, h), 0)
        col = lax.broadcasted_iota(jnp.int32, (tq, h), 1)
        seen = (col // CHUNK) <= (jnp.where(row >= h, row - h, row) // CHUNK)
        s1 = s_ref[t % 2, :, 0:h]
        s_up = jnp.where(seen, upper(s1), NEG_INF)
        s2 = jnp.concatenate([s_ref[t % 2, quarter[1], h:tq], s_ref[t % 2, quarter[3], h:tq]],
                             axis=0)
        s_lo = jnp.concatenate([lower(s1), jnp.where(seen, s2, NEG_INF)], axis=1)
        _, l_up, acc_up = update(s_up, rowmax(s_up), vext_ref[k0:k0 + h, :],
                                 None if prev is None else tuple(upper(x) for x in prev))
        _, l_lo, acc_lo = update(s_lo, rowmax(s_lo), vext_ref[k0:k0 + tq, :],
                                 None if prev is None else tuple(lower(x) for x in prev))
        res_up, res_lo = acc_up / l_up, acc_lo / l_lo
        res = [jnp.concatenate([res_up[c * h:(c + 1) * h], res_lo[c * h:(c + 1) * h]], axis=0)
               for c in range(2)]
        o = res[0] - lam * res[1]
        ms = jnp.mean(o * o, axis=-1, keepdims=True)
        o = o * lax.rsqrt(ms + EPS) * gsub_ref[...] * (1.0 - lambda_init)
        o_ref[rows(i), :] = o.astype(o_ref.dtype)

    scores(0)
    for t in range(len(steps)):
        if t + 1 < len(steps):
            scores(t + 1)
        absorb(t)


def _post_kernel(x_ref, oa_ref, ob_ref, gmix_ref, wgate_ref, bgate_ref, wa_ref, wb_ref, wout_ref,
                 gffn_ref, wfg_ref, wfu_ref, wfd_ref, out_ref, *, d_model):
    x = x_ref[...]
    ms = jnp.mean(x * x, axis=-1, keepdims=True)
    h = (x * lax.rsqrt(ms + EPS) * gmix_ref[...]).astype(BF16)
    gates = jax.nn.sigmoid(_dot(h, wgate_ref[...]) + bgate_ref[...])
    br_a = _dot(oa_ref[...], wa_ref[...])
    br_b = _dot(ob_ref[...], wb_ref[...])
    merged = gates[:, :d_model] * br_a + gates[:, d_model:] * br_b
    x1 = x + _dot(merged.astype(BF16), wout_ref[...])

    ms2 = jnp.mean(x1 * x1, axis=-1, keepdims=True)
    h2 = (x1 * lax.rsqrt(ms2 + EPS) * gffn_ref[...]).astype(BF16)
    fg = _dot(h2, wfg_ref[...])
    fu = _dot(h2, wfu_ref[...])
    ff = (fg * jax.nn.sigmoid(fg)) * fu
    out_ref[...] = x1 + _dot(ff.astype(BF16), wfd_ref[...])


def _rope_tables(seq):
    half = ROPE_DIM // 2
    pos = np.arange(seq, dtype=np.float64)
    inv_freq = ROPE_THETA ** (-np.arange(0, ROPE_DIM, 2, dtype=np.float64) / ROPE_DIM)
    ang = pos[:, None] * inv_freq[None, :]
    cos, sin = np.cos(ang), np.sin(ang)
    ones = np.ones((seq, HEAD_DIM - ROPE_DIM))
    zeros = np.zeros((seq, HEAD_DIM - ROPE_DIM))
    zh = np.zeros((seq, half))
    ra = np.concatenate([cos, cos, ones], axis=1)
    rm = np.concatenate([-sin, zh, zeros], axis=1)
    rp = np.concatenate([zh, sin, zeros], axis=1)
    tile = lambda t: jnp.asarray(np.tile(t, (1, LANES // HEAD_DIM)), dtype=F32)
    return tile(ra), tile(rm), tile(rp)


def _tiles(seq):
    tm, tp, tq_sb, tq_da = 1024, 512, 256, 512
    for t in (tm, tp, tq_sb, tq_da):
        assert seq % t == 0 and t % LANES == 0 and t % CHUNK == 0
    return tm, tp, tq_sb, tq_da


def kernel(x, g_mix, w_in, g_q, g_k, lam_q1, lam_k1, lam_q2, lam_k2, g_sub, w_branch_a, w_branch_b,
           w_gate, b_gate, w_out, g_ffn, w_ffn_gate, w_ffn_up, w_ffn_down):
    b, s, d = x.shape
    depth = g_mix.shape[0]
    sb_w = SB_HEADS * HEAD_DIM
    da_w = DA_HEADS * 2 * HEAD_DIM
    in_w = 3 * sb_w + 3 * da_w
    n = b * s

    tm, tp, tq_sb, tq_da = _tiles(s)

    ra, rm, rp = _rope_tables(s)
    grp = np.arange(da_w) // HEAD_DIM
    gmat = jnp.asarray(np.where(grp[:, None] == grp[None, :], 1.0 / HEAD_DIM, 0.0), dtype=BF16)
    tile_g = lambda g: jnp.tile(g, da_w // HEAD_DIM)[None, :]
    vmem_full = pl.BlockSpec(memory_space=pltpu.VMEM)
    spt = s // tm

    xf = x.reshape(n, d)
    for layer in range(depth):
        lambda_init = 0.8 - 0.6 * math.exp(-0.3 * layer)
        gmix = g_mix[layer][None, :]

        proj = pl.pallas_call(
            functools.partial(_proj_kernel, sb_w=sb_w, da_w=da_w),
            grid=(n // tm,),
            in_specs=[
                pl.BlockSpec((tm, d), lambda t: (t, 0)),
                vmem_full, vmem_full, vmem_full, vmem_full, vmem_full,
                pl.BlockSpec((tm, LANES), lambda t: (t % spt, 0)),
                pl.BlockSpec((tm, LANES), lambda t: (t % spt, 0)),
                pl.BlockSpec((tm, LANES), lambda t: (t % spt, 0)),
            ],
            out_specs=pl.BlockSpec((tm, in_w), lambda t: (t, 0)),
            out_shape=jax.ShapeDtypeStruct((n, in_w), BF16),
            compiler_params=pltpu.CompilerParams(
                dimension_semantics=("arbitrary",), vmem_limit_bytes=VMEM_LIMIT),
            name="proj",
        )(xf, gmix, w_in[layer].astype(BF16), tile_g(g_q[layer]), tile_g(g_k[layer]), gmat,
          ra, rm, rp)

        cb = lambda off: off // LANES
        tq, gw, ns = tq_sb, SB_GROUPS * LANES, SB_TILES
        nq = s // (ns * tq)
        o_a = pl.pallas_call(
            functools.partial(_sb_kernel, tq=tq, n_grp=SB_GROUPS, n_seg=ns),
            grid=(b, sb_w // gw, nq),
            in_specs=[
                pl.BlockSpec((ns * tq, gw), lambda bi, j, i, nq=nq: (bi * nq + i, j)),
                pl.BlockSpec((s, gw), lambda bi, j, i: (bi, sb_w // gw + j)),
                pl.BlockSpec((s, gw), lambda bi, j, i: (bi, 2 * sb_w // gw + j)),
            ],
            out_specs=pl.BlockSpec((ns * tq, gw), lambda bi, j, i, nq=nq: (bi * nq + i, j)),
            out_shape=jax.ShapeDtypeStruct((n, sb_w), BF16),
            scratch_shapes=[pltpu.VMEM((ns, SB_GROUPS, 2 * tq, LANES), BF16),
                            pltpu.VMEM((ns, SB_GROUPS, 2 * tq, LANES), F32),
                            pltpu.VMEM((ns, SB_GROUPS, 2 * tq, LANES), F32)],
            compiler_params=pltpu.CompilerParams(
                dimension_semantics=("arbitrary", "arbitrary", "arbitrary"),
                vmem_limit_bytes=VMEM_LIMIT),
            name="sb_attn",
        )(proj, proj, proj)

        tq = tq_da
        post_w = [w_gate[layer], w_branch_a[layer], w_branch_b[layer], w_out[layer],
                  w_ffn_gate[layer], w_ffn_up[layer], w_ffn_down[layer]]
        n_steps = b * DA_HEADS
        for w in post_w:
            assert w.shape[0] % (16 * n_steps) == 0
        w_spec = lambda w: pl.BlockSpec((w.shape[0] // n_steps, w.shape[1]),
                                        lambda bi, j: (bi * DA_HEADS + j, 0))
        o_b, *post_w16 = pl.pallas_call(
            functools.partial(_da_flat_kernel, tq=tq, n_tiles=s // tq, n_cast=len(post_w),
                              lambda_init=lambda_init),
            grid=(b, DA_HEADS),
            in_specs=[
                pl.BlockSpec((s, LANES), lambda bi, j: (bi, cb(3 * sb_w) + j)),
                pl.BlockSpec((s, LANES), lambda bi, j: (bi, cb(3 * sb_w + da_w) + j)),
                pl.BlockSpec((s, LANES), lambda bi, j: (bi, cb(3 * sb_w + 2 * da_w) + j)),
                vmem_full, vmem_full, vmem_full, vmem_full, vmem_full,
            ] + [w_spec(w) for w in post_w],
            out_specs=[pl.BlockSpec((s, LANES), lambda bi, j: (bi, j))]
            + [w_spec(w) for w in post_w],
            out_shape=[jax.ShapeDtypeStruct((n, da_w), BF16)]
            + [jax.ShapeDtypeStruct(w.shape, BF16) for w in post_w],
            scratch_shapes=[pltpu.VMEM((s, 2 * LANES), BF16),
                            pltpu.VMEM((2, 2 * tq, LANES), BF16),
                            pltpu.VMEM((2, 2 * tq, tq), F32)]
            + [pltpu.VMEM((2 * tq, LANES), F32)] * 3,
            compiler_params=pltpu.CompilerParams(
                dimension_semantics=("arbitrary", "arbitrary"),
                vmem_limit_bytes=VMEM_LIMIT),
            name="da_attn",
        )(proj, proj, proj, lam_q1[layer][None, :], lam_k1[layer][None, :],
          lam_q2[layer][None, :], lam_k2[layer][None, :], g_sub[layer][None, :], *post_w)
        wgate16, wa16, wb16, wout16, wfg16, wfu16, wfd16 = post_w16

        xf = pl.pallas_call(
            functools.partial(_post_kernel, d_model=d),
            grid=(n // tp,),
            in_specs=[
                pl.BlockSpec((tp, d), lambda t: (t, 0)),
                pl.BlockSpec((tp, sb_w), lambda t: (t, 0)),
                pl.BlockSpec((tp, da_w), lambda t: (t, 0)),
            ] + [vmem_full] * 10,
            out_specs=pl.BlockSpec((tp, d), lambda t: (t, 0)),
            out_shape=jax.ShapeDtypeStruct((n, d), F32),
            compiler_params=pltpu.CompilerParams(
                dimension_semantics=("arbitrary",), vmem_limit_bytes=VMEM_LIMIT),
            name="post",
        )(xf, o_a, o_b, gmix, wgate16, b_gate[layer][None, :], wa16, wb16, wout16,
          g_ffn[layer][None, :], wfg16, wfu16, wfd16)
    return xf.reshape(b, s, d)
```

```python
import functools
import math

import jax
import jax.numpy as jnp
import numpy as np
from jax import lax
from jax.experimental import pallas as pl
from jax.experimental.pallas import tpu as pltpu

F32 = jnp.float32
BF16 = jnp.bfloat16

CHUNK = 64
SB_HEADS = 8
DA_HEADS = 4
HEAD_DIM = 64
ROPE_THETA = 500000.0
ROPE_DIM = HEAD_DIM // 4
EPS = 1e-6
NEG_INF = -1e30
LOG2E = math.log2(math.e)
SB_GROUPS = 4
SB_TILES = 4
SB_STOP_LOG2 = 152.0
SB_EXP2_MAX = 126.0
LANES = 128
VMEM_LIMIT = 56 * 1024 * 1024


def _dot(a, b):
    return jnp.dot(a, b, preferred_element_type=F32)


def _dot_nt(a, b):
    return lax.dot_general(a, b, (((1,), (1,)), ((), ())), preferred_element_type=F32)


def _proj_kernel(x_ref, gmix_ref, win_ref, gq_ref, gk_ref, gmat_ref,
                 ra_ref, rm_ref, rp_ref, out_ref, *, sb_w, da_w):
    x = x_ref[...]
    ms = jnp.mean(x * x, axis=-1, keepdims=True)
    h = (x * lax.rsqrt(ms + EPS) * gmix_ref[...]).astype(BF16)
    scale = HEAD_DIM ** -0.5
    o = 3 * sb_w
    ra, rm, rp = ra_ref[...], rm_ref[...], rp_ref[...]

    def qk_norm_rope(t, g, mult):
        msq = _dot((t * t).astype(BF16), gmat_ref[...])
        tn = t * lax.rsqrt(msq + EPS) * g
        cols = []
        for j in range(da_w // LANES):
            c = tn[:, j * LANES:(j + 1) * LANES]
            r = (c * ra + pltpu.roll(c, LANES - ROPE_DIM // 2, 1) * rm
                 + pltpu.roll(c, ROPE_DIM // 2, 1) * rp)
            cols.append((r * mult).astype(BF16))
        return jnp.concatenate(cols, axis=1)

    qk = _dot(h, win_ref[:, o:o + 2 * da_w])
    out_ref[:, o:o + da_w] = qk_norm_rope(qk[:, 0:da_w], gq_ref[...], scale * LOG2E)
    out_ref[:, o + da_w:o + 2 * da_w] = qk_norm_rope(qk[:, da_w:2 * da_w], gk_ref[...], 1.0)
    sb = _dot(h, win_ref[:, 0:o])
    out_ref[:, 0:sb_w] = (sb[:, 0:sb_w] * (scale * LOG2E)).astype(BF16)
    out_ref[:, sb_w:o] = sb[:, sb_w:o].astype(BF16)
    out_ref[:, o + 2 * da_w:o + 3 * da_w] = _dot(h, win_ref[:, o + 2 * da_w:o + 3 * da_w]).astype(BF16)


def _stack_masked(q, n_parts):
    lane = lax.broadcasted_iota(jnp.int32, q.shape, 1)
    zero = jnp.zeros_like(q)
    return jnp.concatenate(
        [jnp.where((lane >= p * HEAD_DIM) & (lane < (p + 1) * HEAD_DIM), q, zero)
         for p in range(n_parts)], axis=0)


def _sb_kernel(q_ref, k_ref, v_ref, o_ref, qs_ref, acc_ref, run_ref, *, tq, n_grp, n_seg):
    first_tile = pl.program_id(2) * n_seg
    m2 = 2 * tq
    r2 = lax.broadcasted_iota(jnp.int32, (tq, tq), 0)
    c2 = lax.broadcasted_iota(jnp.int32, (tq, tq), 1)
    tri = jnp.where(r2 > c2, 1.0, 0.0).astype(BF16)
    lanes = lambda g: slice(g * LANES, (g + 1) * LANES)
    rows = lambda seg: slice(seg * tq, (seg + 1) * tq)

    def block(seg, g, blk, run, diagonal=False, gate=None):
        ks = pl.multiple_of(blk * tq, tq)
        z = _dot_nt(qs_ref[seg, g], k_ref[pl.ds(ks, tq), lanes(g)])
        if diagonal:
            row = lax.broadcasted_iota(jnp.int32, (m2, tq), 0)
            col = lax.broadcasted_iota(jnp.int32, (m2, tq), 1)
            z = jnp.where(col < jnp.where(row >= tq, row - tq, row), z, NEG_INF)
        if gate is not None:
            z = jnp.where(gate > 0.0, z, NEG_INF)
        p = jnp.maximum(z, jnp.log2(1.0 + jnp.exp2(jnp.minimum(z, SB_EXP2_MAX))))
        after = _dot(p.astype(BF16), tri)
        spent = after + jnp.concatenate([run] * (tq // LANES), axis=1)
        run = run + jnp.broadcast_to(after[:, 0:1] + p[:, 0:1], (m2, LANES))
        a = jnp.exp2((z - p) - spent)
        return run, _dot(a.astype(BF16), v_ref[pl.ds(ks, tq), lanes(g)])

    zeros = jnp.zeros((m2, LANES), F32)
    least0 = []
    diag = {}
    for seg in range(n_seg):
        for g in range(n_grp):
            qs_ref[seg, g] = _stack_masked(q_ref[rows(seg), lanes(g)], 2)
            diag[seg, g] = block(seg, g, first_tile + seg, zeros, diagonal=True)
    for seg in range(n_seg):
        i = first_tile + seg
        least = None
        for g in range(n_grp):
            run, out = diag[seg, g]
            if seg == 0:
                run, out2 = block(seg, g, jnp.maximum(i - 1, 0), run,
                                  gate=jnp.where(i > 0, 1.0, 0.0).astype(F32))
            else:
                run, out2 = block(seg, g, i - 1, run)
            run_ref[seg, g] = run
            acc_ref[seg, g] = out + out2
            least = jnp.min(run) if least is None else jnp.minimum(least, jnp.min(run))
        least0.append(least)

    def more(carry):
        blk, least = carry
        return jnp.logical_and(blk >= 0, least < SB_STOP_LOG2)

    for seg in range(n_seg):
        def step(carry, seg=seg):
            blk, _ = carry
            least = None
            for g in range(n_grp):
                run, out = block(seg, g, blk, run_ref[seg, g])
                run_ref[seg, g] = run
                acc_ref[seg, g] += out
                least = jnp.min(run) if least is None else jnp.minimum(least, jnp.min(run))
            return blk - 1, least

        lax.while_loop(more, step, (first_tile + seg - 2, least0[seg]))

    lane = lax.broadcasted_iota(jnp.int32, (tq, LANES), 1)
    for seg in range(n_seg):
        for g in range(n_grp):
            o_ref[rows(seg), lanes(g)] = jnp.where(
                lane < HEAD_DIM, acc_ref[seg, g, 0:tq, :], acc_ref[seg, g, tq:m2, :]
            ).astype(o_ref.dtype)


def _da_flat_kernel(q_ref, k_ref, v_ref, lq1_ref, lk1_ref, lq2_ref, lk2_ref, gsub_ref, *refs,
                    tq, n_tiles, n_cast, lambda_init):
    w32_refs, o_ref, w16_refs = refs[:n_cast], refs[n_cast], refs[n_cast + 1:2 * n_cast + 1]
    vext_ref, qs_ref, s_ref, m_ref, l_ref, acc_ref = refs[2 * n_cast + 1:]
    for w32, w16 in zip(w32_refs, w16_refs):
        w16[...] = w32[...].astype(w16.dtype)

    m2 = 2 * tq
    vext_ref[:, 0:LANES] = v_ref[...]
    vext_ref[:, LANES:2 * LANES] = jnp.ones(v_ref.shape, v_ref.dtype)
    lam = (jnp.exp(jnp.sum(lq1_ref[...] * lk1_ref[...], axis=-1, keepdims=True))
           - jnp.exp(jnp.sum(lq2_ref[...] * lk2_ref[...], axis=-1, keepdims=True))
           + lambda_init)
    steps = [(i, blk) for i in range(n_tiles) for blk in range(i + 1)]
    rows = lambda j: slice(j * tq, (j + 1) * tq)

    h = tq // 2
    quarter = (slice(0, h), slice(h, tq), slice(tq, tq + h), slice(tq + h, m2))
    upper = lambda x: jnp.concatenate([x[quarter[0]], x[quarter[2]]], axis=0)
    lower = lambda x: jnp.concatenate([x[quarter[1]], x[quarter[3]]], axis=0)
    rowmax = lambda s: jnp.broadcast_to(jnp.max(s, axis=1, keepdims=True), (s.shape[0], LANES))

    def scores(t):
        i, blk = steps[t]
        if blk == 0:
            qs_ref[i % 2] = _stack_masked(q_ref[rows(i), :], 2)
        q = qs_ref[i % 2]
        if blk < i:
            s = _dot_nt(q, k_ref[rows(blk), :])
            s_ref[t % 2] = s
        else:
            k0 = blk * tq
            s_ref[t % 2, :, 0:h] = _dot_nt(q, k_ref[k0:k0 + h, :])
            s2 = _dot_nt(lower(q), k_ref[k0 + h:k0 + tq, :])
            s_ref[t % 2, quarter[1], h:tq] = s2[0:h]
            s_ref[t % 2, quarter[3], h:tq] = s2[h:tq]

    def update(s, m_cur, v, prev):
        m_new = m_cur if prev is None else jnp.maximum(prev[0], m_cur)
        p = jnp.exp2(s - jnp.concatenate([m_new] * (s.shape[1] // LANES), axis=1))
        pv = _dot(p.astype(BF16), v)
        if prev is None:
            return m_new, pv[:, LANES:], pv[:, :LANES]
        alpha = jnp.exp2(prev[0] - m_new)
        return m_new, alpha * prev[1] + pv[:, LANES:], alpha * prev[2] + pv[:, :LANES]

    def absorb(t):
        i, blk = steps[t]
        prev = None if blk == 0 else (m_ref[...], l_ref[...], acc_ref[...])
        if blk < i:
            s = s_ref[t % 2]
            m_ref[...], l_ref[...], acc_ref[...] = update(
                s, rowmax(s), vext_ref[rows(blk), :], prev)
            return
        k0 = blk * tq
        row = lax.broadcasted_iota(jnp.int32, (tq, h), 0)
        col = lax.broadcasted_iota(jnp.int32, (tq, h), 1)
        seen = (col // CHUNK) <= (jnp.where(row >= h, row - h, row) // CHUNK)
        s1 = s_ref[t % 2, :, 0:h]
        s_up = jnp.where(seen, upper(s1), NEG_INF)
        s2 = jnp.concatenate([s_ref[t % 2, quarter[1], h:tq], s_ref[t % 2, quarter[3], h:tq]],
                             axis=0)
        s_lo = jnp.concatenate([lower(s1), jnp.where(seen, s2, NEG_INF)], axis=1)
        _, l_up, acc_up = update(s_up, rowmax(s_up), vext_ref[k0:k0 + h, :],
                                 None if prev is None else tuple(upper(x) for x in prev))
        _, l_lo, acc_lo = update(s_lo, rowmax(s_lo), vext_ref[k0:k0 + tq, :],
                                 None if prev is None else tuple(lower(x) for x in prev))
        res_up, res_lo = acc_up / l_up, acc_lo / l_lo
        res = [jnp.concatenate([res_up[c * h:(c + 1) * h], res_lo[c * h:(c + 1) * h]], axis=0)
               for c in range(2)]
        o = res[0] - lam * res[1]
        ms = jnp.mean(o * o, axis=-1, keepdims=True)
        o = o * lax.rsqrt(ms + EPS) * gsub_ref[...] * (1.0 - lambda_init)
        o_ref[rows(i), :] = o.astype(o_ref.dtype)

    scores(0)
    scores(1)
    for t in range(len(steps)):
        absorb(t)
        if t + 2 < len(steps):
            scores(t + 2)


def _post_kernel(x_ref, oa_ref, ob_ref, gmix_ref, wgate_ref, bgate_ref, wa_ref, wb_ref, wout_ref,
                 gffn_ref, wfg_ref, wfu_ref, wfd_ref, out_ref, *, d_model):
    x = x_ref[...]
    ms = jnp.mean(x * x, axis=-1, keepdims=True)
    h = (x * lax.rsqrt(ms + EPS) * gmix_ref[...]).astype(BF16)
    gates = jax.nn.sigmoid(_dot(h, wgate_ref[...]) + bgate_ref[...])
    br_a = _dot(oa_ref[...], wa_ref[...])
    br_b = _dot(ob_ref[...], wb_ref[...])
    merged = gates[:, :d_model] * br_a + gates[:, d_model:] * br_b
    x1 = x + _dot(merged.astype(BF16), wout_ref[...])

    ms2 = jnp.mean(x1 * x1, axis=-1, keepdims=True)
    h2 = (x1 * lax.rsqrt(ms2 + EPS) * gffn_ref[...]).astype(BF16)
    fg = _dot(h2, wfg_ref[...])
    fu = _dot(h2, wfu_ref[...])
    ff = (fg * jax.nn.sigmoid(fg)) * fu
    out_ref[...] = x1 + _dot(ff.astype(BF16), wfd_ref[...])


def _rope_tables(seq):
    half = ROPE_DIM // 2
    pos = np.arange(seq, dtype=np.float64)
    inv_freq = ROPE_THETA ** (-np.arange(0, ROPE_DIM, 2, dtype=np.float64) / ROPE_DIM)
    ang = pos[:, None] * inv_freq[None, :]
    cos, sin = np.cos(ang), np.sin(ang)
    ones = np.ones((seq, HEAD_DIM - ROPE_DIM))
    zeros = np.zeros((seq, HEAD_DIM - ROPE_DIM))
    zh = np.zeros((seq, half))
    ra = np.concatenate([cos, cos, ones], axis=1)
    rm = np.concatenate([-sin, zh, zeros], axis=1)
    rp = np.concatenate([zh, sin, zeros], axis=1)
    tile = lambda t: jnp.asarray(np.tile(t, (1, LANES // HEAD_DIM)), dtype=F32)
    return tile(ra), tile(rm), tile(rp)


def _tiles(seq):
    tm, tp, tq_sb, tq_da = 1024, 512, 256, 512
    for t in (tm, tp, tq_sb, tq_da):
        assert seq % t == 0 and t % LANES == 0 and t % CHUNK == 0
    return tm, tp, tq_sb, tq_da


def kernel(x, g_mix, w_in, g_q, g_k, lam_q1, lam_k1, lam_q2, lam_k2, g_sub, w_branch_a, w_branch_b,
           w_gate, b_gate, w_out, g_ffn, w_ffn_gate, w_ffn_up, w_ffn_down):
    b, s, d = x.shape
    depth = g_mix.shape[0]
    sb_w = SB_HEADS * HEAD_DIM
    da_w = DA_HEADS * 2 * HEAD_DIM
    in_w = 3 * sb_w + 3 * da_w
    n = b * s

    tm, tp, tq_sb, tq_da = _tiles(s)

    ra, rm, rp = _rope_tables(s)
    grp = np.arange(da_w) // HEAD_DIM
    gmat = jnp.asarray(np.where(grp[:, None] == grp[None, :], 1.0 / HEAD_DIM, 0.0), dtype=BF16)
    tile_g = lambda g: jnp.tile(g, da_w // HEAD_DIM)[None, :]
    vmem_full = pl.BlockSpec(memory_space=pltpu.VMEM)
    spt = s // tm

    xf = x.reshape(n, d)
    for layer in range(depth):
        lambda_init = 0.8 - 0.6 * math.exp(-0.3 * layer)
        gmix = g_mix[layer][None, :]

        proj = pl.pallas_call(
            functools.partial(_proj_kernel, sb_w=sb_w, da_w=da_w),
            grid=(n // tm,),
            in_specs=[
                pl.BlockSpec((tm, d), lambda t: (t, 0)),
                vmem_full, vmem_full, vmem_full, vmem_full, vmem_full,
                pl.BlockSpec((tm, LANES), lambda t: (t % spt, 0)),
                pl.BlockSpec((tm, LANES), lambda t: (t % spt, 0)),
                pl.BlockSpec((tm, LANES), lambda t: (t % spt, 0)),
            ],
            out_specs=pl.BlockSpec((tm, in_w), lambda t: (t, 0)),
            out_shape=jax.ShapeDtypeStruct((n, in_w), BF16),
            compiler_params=pltpu.CompilerParams(
                dimension_semantics=("arbitrary",), vmem_limit_bytes=VMEM_LIMIT),
            name="proj",
        )(xf, gmix, w_in[layer].astype(BF16), tile_g(g_q[layer]), tile_g(g_k[layer]), gmat,
          ra, rm, rp)

        cb = lambda off: off // LANES
        tq, gw, ns = tq_sb, SB_GROUPS * LANES, SB_TILES
        nq = s // (ns * tq)
        o_a = pl.pallas_call(
            functools.partial(_sb_kernel, tq=tq, n_grp=SB_GROUPS, n_seg=ns),
            grid=(b, sb_w // gw, nq),
            in_specs=[
                pl.BlockSpec((ns * tq, gw), lambda bi, j, i, nq=nq: (bi * nq + i, j)),
                pl.BlockSpec((s, gw), lambda bi, j, i: (bi, sb_w // gw + j)),
                pl.BlockSpec((s, gw), lambda bi, j, i: (bi, 2 * sb_w // gw + j)),
            ],
            out_specs=pl.BlockSpec((ns * tq, gw), lambda bi, j, i, nq=nq: (bi * nq + i, j)),
            out_shape=jax.ShapeDtypeStruct((n, sb_w), BF16),
            scratch_shapes=[pltpu.VMEM((ns, SB_GROUPS, 2 * tq, LANES), BF16),
                            pltpu.VMEM((ns, SB_GROUPS, 2 * tq, LANES), F32),
                            pltpu.VMEM((ns, SB_GROUPS, 2 * tq, LANES), F32)],
            compiler_params=pltpu.CompilerParams(
                dimension_semantics=("arbitrary", "arbitrary", "arbitrary"),
                vmem_limit_bytes=VMEM_LIMIT),
            name="sb_attn",
        )(proj, proj, proj)

        tq = tq_da
        post_w = [w_gate[layer], w_branch_a[layer], w_branch_b[layer], w_out[layer],
                  w_ffn_gate[layer], w_ffn_up[layer], w_ffn_down[layer]]
        n_steps = b * DA_HEADS
        for w in post_w:
            assert w.shape[0] % (16 * n_steps) == 0
        w_spec = lambda w: pl.BlockSpec((w.shape[0] // n_steps, w.shape[1]),
                                        lambda bi, j: (bi * DA_HEADS + j, 0))
        o_b, *post_w16 = pl.pallas_call(
            functools.partial(_da_flat_kernel, tq=tq, n_tiles=s // tq, n_cast=len(post_w),
                              lambda_init=lambda_init),
            grid=(b, DA_HEADS),
            in_specs=[
                pl.BlockSpec((s, LANES), lambda bi, j: (bi, cb(3 * sb_w) + j)),
                pl.BlockSpec((s, LANES), lambda bi, j: (bi, cb(3 * sb_w + da_w) + j)),
                pl.BlockSpec((s, LANES), lambda bi, j: (bi, cb(3 * sb_w + 2 * da_w) + j)),
                vmem_full, vmem_full, vmem_full, vmem_full, vmem_full,
            ] + [w_spec(w) for w in post_w],
            out_specs=[pl.BlockSpec((s, LANES), lambda bi, j: (bi, j))]
            + [w_spec(w) for w in post_w],
            out_shape=[jax.ShapeDtypeStruct((n, da_w), BF16)]
            + [jax.ShapeDtypeStruct(w.shape, BF16) for w in post_w],
            scratch_shapes=[pltpu.VMEM((s, 2 * LANES), BF16),
                            pltpu.VMEM((2, 2 * tq, LANES), BF16),
                            pltpu.VMEM((2, 2 * tq, tq), F32)]
            + [pltpu.VMEM((2 * tq, LANES), F32)] * 3,
            compiler_params=pltpu.CompilerParams(
                dimension_semantics=("arbitrary", "arbitrary"),
                vmem_limit_bytes=VMEM_LIMIT),
            name="da_attn",
        )(proj, proj, proj, lam_q1[layer][None, :], lam_k1[layer][None, :],
          lam_q2[layer][None, :], lam_k2[layer][None, :], g_sub[layer][None, :], *post_w)
        wgate16, wa16, wb16, wout16, wfg16, wfu16, wfd16 = post_w16

        xf = pl.pallas_call(
            functools.partial(_post_kernel, d_model=d),
            grid=(n // tp,),
            in_specs=[
                pl.BlockSpec((tp, d), lambda t: (t, 0)),
                pl.BlockSpec((tp, sb_w), lambda t: (t, 0)),
                pl.BlockSpec((tp, da_w), lambda t: (t, 0)),
            ] + [vmem_full] * 10,
            out_specs=pl.BlockSpec((tp, d), lambda t: (t, 0)),
            out_shape=jax.ShapeDtypeStruct((n, d), F32),
            compiler_params=pltpu.CompilerParams(
                dimension_semantics=("arbitrary",), vmem_limit_bytes=VMEM_LIMIT),
            name="post",
        )(xf, o_a, o_b, gmix, wgate16, b_gate[layer][None, :], wa16, wb16, wout16,
          g_ffn[layer][None, :], wfg16, wfu16, wfd16)
    return xf.reshape(b, s, d)
```

```python
import functools
import math

import jax
import jax.numpy as jnp
import numpy as np
from jax import lax
from jax.experimental import pallas as pl
from jax.experimental.pallas import tpu as pltpu

F32 = jnp.float32
BF16 = jnp.bfloat16

CHUNK = 64
SB_HEADS = 8
DA_HEADS = 4
HEAD_DIM = 64
ROPE_THETA = 500000.0
ROPE_DIM = HEAD_DIM // 4
EPS = 1e-6
NEG_INF = -1e30
LOG2E = math.log2(math.e)
SB_GROUPS = 4
SB_TILES = 4
SB_STOP_LOG2 = 152.0
SB_EXP2_MAX = 126.0
LANES = 128
VMEM_LIMIT = 56 * 1024 * 1024


def _dot(a, b):
    return jnp.dot(a, b, preferred_element_type=F32)


def _dot_nt(a, b):
    return lax.dot_general(a, b, (((1,), (1,)), ((), ())), preferred_element_type=F32)


def _proj_kernel(x_ref, gmix_ref, win_ref, gq_ref, gk_ref, gmat_ref,
                 ra_ref, rm_ref, rp_ref, out_ref, *, sb_w, da_w):
    x = x_ref[...]
    ms = jnp.mean(x * x, axis=-1, keepdims=True)
    h = (x * lax.rsqrt(ms + EPS) * gmix_ref[...]).astype(BF16)
    scale = HEAD_DIM ** -0.5
    o = 3 * sb_w
    ra, rm, rp = ra_ref[...], rm_ref[...], rp_ref[...]

    def qk_norm_rope(t, g, mult):
        msq = _dot((t * t).astype(BF16), gmat_ref[...])
        tn = t * lax.rsqrt(msq + EPS) * g
        cols = []
        for j in range(da_w // LANES):
            c = tn[:, j * LANES:(j + 1) * LANES]
            r = (c * ra + pltpu.roll(c, LANES - ROPE_DIM // 2, 1) * rm
                 + pltpu.roll(c, ROPE_DIM // 2, 1) * rp)
            cols.append((r * mult).astype(BF16))
        return jnp.concatenate(cols, axis=1)

    qk = _dot(h, win_ref[:, o:o + 2 * da_w])
    out_ref[:, o:o + da_w] = qk_norm_rope(qk[:, 0:da_w], gq_ref[...], scale * LOG2E)
    out_ref[:, o + da_w:o + 2 * da_w] = qk_norm_rope(qk[:, da_w:2 * da_w], gk_ref[...], 1.0)
    sb = _dot(h, win_ref[:, 0:o])
    out_ref[:, 0:sb_w] = (sb[:, 0:sb_w] * (scale * LOG2E)).astype(BF16)
    out_ref[:, sb_w:o] = sb[:, sb_w:o].astype(BF16)
    out_ref[:, o + 2 * da_w:o + 3 * da_w] = _dot(h, win_ref[:, o + 2 * da_w:o + 3 * da_w]).astype(BF16)


def _stack_masked(q, n_parts):
    lane = lax.broadcasted_iota(jnp.int32, q.shape, 1)
    zero = jnp.zeros_like(q)
    return jnp.concatenate(
        [jnp.where((lane >= p * HEAD_DIM) & (lane < (p + 1) * HEAD_DIM), q, zero)
         for p in range(n_parts)], axis=0)


def _sb_kernel(q_ref, k_ref, v_ref, o_ref, qs_ref, acc_ref, run_ref, *, tq, n_grp, n_seg):
    first_tile = pl.program_id(2) * n_seg
    m2 = 2 * tq
    r2 = lax.broadcasted_iota(jnp.int32, (tq, tq), 0)
    c2 = lax.broadcasted_iota(jnp.int32, (tq, tq), 1)
    tri = jnp.where(r2 > c2, 1.0, 0.0).astype(BF16)
    lanes = lambda g: slice(g * LANES, (g + 1) * LANES)
    rows = lambda seg: slice(seg * tq, (seg + 1) * tq)

    def block(seg, g, blk, run, diagonal=False, gate=None):
        ks = pl.multiple_of(blk * tq, tq)
        z = _dot_nt(qs_ref[seg, g], k_ref[pl.ds(ks, tq), lanes(g)])
        if diagonal:
            row = lax.broadcasted_iota(jnp.int32, (m2, tq), 0)
            col = lax.broadcasted_iota(jnp.int32, (m2, tq), 1)
            z = jnp.where(col < jnp.where(row >= tq, row - tq, row), z, NEG_INF)
        if gate is not None:
            z = jnp.where(gate > 0.0, z, NEG_INF)
        p = jnp.maximum(z, jnp.log2(1.0 + jnp.exp2(jnp.minimum(z, SB_EXP2_MAX))))
        after = _dot(p.astype(BF16), tri)
        spent = after + jnp.concatenate([run] * (tq // LANES), axis=1)
        run = run + jnp.broadcast_to(after[:, 0:1] + p[:, 0:1], (m2, LANES))
        a = jnp.exp2((z - p) - spent)
        return run, _dot(a.astype(BF16), v_ref[pl.ds(ks, tq), lanes(g)])

    zeros = jnp.zeros((m2, LANES), F32)
    least0 = []
    diag = {}
    for seg in range(n_seg):
        for g in range(n_grp):
            qs_ref[seg, g] = _stack_masked(q_ref[rows(seg), lanes(g)], 2)
            diag[seg, g] = block(seg, g, first_tile + seg, zeros, diagonal=True)
    for seg in range(n_seg):
        i = first_tile + seg
        least = None
        for g in range(n_grp):
            run, out = diag[seg, g]
            if seg == 0:
                run, out2 = block(seg, g, jnp.maximum(i - 1, 0), run,
                                  gate=jnp.where(i > 0, 1.0, 0.0).astype(F32))
            else:
                run, out2 = block(seg, g, i - 1, run)
            run_ref[seg, g] = run
            acc_ref[seg, g] = out + out2
            least = jnp.min(run) if least is None else jnp.minimum(least, jnp.min(run))
        least0.append(least)

    def more(carry):
        blk, least = carry
        return jnp.logical_and(blk >= 0, least < SB_STOP_LOG2)

    for seg in range(n_seg):
        def step(carry, seg=seg):
            blk, _ = carry
            least = None
            for g in range(n_grp):
                run, out = block(seg, g, blk, run_ref[seg, g])
                run_ref[seg, g] = run
                acc_ref[seg, g] += out
                least = jnp.min(run) if least is None else jnp.minimum(least, jnp.min(run))
            return blk - 1, least

        lax.while_loop(more, step, (first_tile + seg - 2, least0[seg]))

    lane = lax.broadcasted_iota(jnp.int32, (tq, LANES), 1)
    for seg in range(n_seg):
        for g in range(n_grp):
            o_ref[rows(seg), lanes(g)] = jnp.where(
                lane < HEAD_DIM, acc_ref[seg, g, 0:tq, :], acc_ref[seg, g, tq:m2, :]
            ).astype(o_ref.dtype)


def _da_flat_kernel(q_ref, k_ref, v_ref, lq1_ref, lk1_ref, lq2_ref, lk2_ref, gsub_ref, *refs,
                    tq, n_tiles, n_cast, lambda_init):
    w32_refs, o_ref, w16_refs = refs[:n_cast], refs[n_cast], refs[n_cast + 1:2 * n_cast + 1]
    vext_ref, qs_ref, s_ref, m_ref, l_ref, acc_ref = refs[2 * n_cast + 1:]
    for w32, w16 in zip(w32_refs, w16_refs):
        w16[...] = w32[...].astype(w16.dtype)

    m2 = 2 * tq
    vext_ref[:, 0:LANES] = v_ref[...]
    vext_ref[:, LANES:2 * LANES] = jnp.ones(v_ref.shape, v_ref.dtype)
    lam = (jnp.exp(jnp.sum(lq1_ref[...] * lk1_ref[...], axis=-1, keepdims=True))
           - jnp.exp(jnp.sum(lq2_ref[...] * lk2_ref[...], axis=-1, keepdims=True))
           + lambda_init)
    steps = [(i, blk) for i in range(n_tiles) for blk in range(i + 1)]
    rows = lambda j: slice(j * tq, (j + 1) * tq)

    h = tq // 2
    quarter = (slice(0, h), slice(h, tq), slice(tq, tq + h), slice(tq + h, m2))
    upper = lambda x: jnp.concatenate([x[quarter[0]], x[quarter[2]]], axis=0)
    lower = lambda x: jnp.concatenate([x[quarter[1]], x[quarter[3]]], axis=0)
    rowmax = lambda s: jnp.broadcast_to(jnp.max(s, axis=1, keepdims=True), (s.shape[0], LANES))

    def scores(t):
        i, blk = steps[t]
        if blk == 0:
            qs_ref[i % 2] = _stack_masked(q_ref[rows(i), :], 2)
        q = qs_ref[i % 2]
        if blk < i:
            s = _dot_nt(q, k_ref[rows(blk), :])
            s_ref[t % 2] = s
        else:
            k0 = blk * tq
            s_ref[t % 2, :, 0:h] = _dot_nt(q, k_ref[k0:k0 + h, :])
            s2 = _dot_nt(lower(q), k_ref[k0 + h:k0 + tq, :])
            s_ref[t % 2, quarter[1], h:tq] = s2[0:h]
            s_ref[t % 2, quarter[3], h:tq] = s2[h:tq]

    def update(s, m_cur, v, prev):
        m_new = m_cur if prev is None else jnp.maximum(prev[0], m_cur)
        p = jnp.exp2(s - jnp.concatenate([m_new] * (s.shape[1] // LANES), axis=1))
        pv = _dot(p.astype(BF16), v)
        if prev is None:
            return m_new, pv[:, LANES:], pv[:, :LANES]
        alpha = jnp.exp2(prev[0] - m_new)
        return m_new, alpha * prev[1] + pv[:, LANES:], alpha * prev[2] + pv[:, :LANES]

    def absorb(t):
        i, blk = steps[t]
        prev = None if blk == 0 else (m_ref[...], l_ref[...], acc_ref[...])
        if blk < i:
            s = s_ref[t % 2]
            m_ref[...], l_ref[...], acc_ref[...] = update(
                s, rowmax(s), vext_ref[rows(blk), :], prev)
            return
        k0 = blk * tq
        row = lax.broadcasted_iota(jnp.int32, (tq, h), 0)
        col = lax.broadcasted_iota(jnp.int32, (tq, h), 1)
        seen = (col // CHUNK) <= (jnp.where(row >= h, row - h, row) // CHUNK)
        s1 = s_ref[t % 2, :, 0:h]
        s_up = jnp.where(seen, upper(s1), NEG_INF)
        s2 = jnp.concatenate([s_ref[t % 2, quarter[1], h:tq], s_ref[t % 2, quarter[3], h:tq]],
                             axis=0)
        s_lo = jnp.concatenate([lower(s1), jnp.where(seen, s2, NEG_INF)], axis=1)
        _, l_up, acc_up = update(s_up, rowmax(s_up), vext_ref[k0:k0 + h, :],
                                 None if prev is None else tuple(upper(x) for x in prev))
        _, l_lo, acc_lo = update(s_lo, rowmax(s_lo), vext_ref[k0:k0 + tq, :],
                                 None if prev is None else tuple(lower(x) for x in prev))
        res_up, res_lo = acc_up / l_up, acc_lo / l_lo
        res = [jnp.concatenate([res_up[c * h:(c + 1) * h], res_lo[c * h:(c + 1) * h]], axis=0)
               for c in range(2)]
        o = res[0] - lam * res[1]
        ms = jnp.mean(o * o, axis=-1, keepdims=True)
        o = o * lax.rsqrt(ms + EPS) * gsub_ref[...] * (1.0 - lambda_init)
        o_ref[rows(i), :] = o.astype(o_ref.dtype)

    scores(0)
    for t in range(len(steps)):
        if t + 1 < len(steps):
            scores(t + 1)
        absorb(t)


def _post_kernel(x_ref, oa_ref, ob_ref, gmix_ref, wgate_ref, bgate_ref, wa_ref, wb_ref, wout_ref,
                 gffn_ref, wfg_ref, wfu_ref, wfd_ref, out_ref, *, d_model):
    br_a = _dot(oa_ref[...], wa_ref[...])
    br_b = _dot(ob_ref[...], wb_ref[...])
    x = x_ref[...]
    ms = jnp.mean(x * x, axis=-1, keepdims=True)
    h = (x * lax.rsqrt(ms + EPS) * gmix_ref[...]).astype(BF16)
    gates = jax.nn.sigmoid(_dot(h, wgate_ref[...]) + bgate_ref[...])
    merged = gates[:, :d_model] * br_a + gates[:, d_model:] * br_b
    x1 = x + _dot(merged.astype(BF16), wout_ref[...])

    ms2 = jnp.mean(x1 * x1, axis=-1, keepdims=True)
    h2 = (x1 * lax.rsqrt(ms2 + EPS) * gffn_ref[...]).astype(BF16)
    fg = _dot(h2, wfg_ref[...])
    fu = _dot(h2, wfu_ref[...])
    ff = (fg * jax.nn.sigmoid(fg)) * fu
    out_ref[...] = x1 + _dot(ff.astype(BF16), wfd_ref[...])


def _rope_tables(seq):
    half = ROPE_DIM // 2
    pos = np.arange(seq, dtype=np.float64)
    inv_freq = ROPE_THETA ** (-np.arange(0, ROPE_DIM, 2, dtype=np.float64) / ROPE_DIM)
    ang = pos[:, None] * inv_freq[None, :]
    cos, sin = np.cos(ang), np.sin(ang)
    ones = np.ones((seq, HEAD_DIM - ROPE_DIM))
    zeros = np.zeros((seq, HEAD_DIM - ROPE_DIM))
    zh = np.zeros((seq, half))
    ra = np.concatenate([cos, cos, ones], axis=1)
    rm = np.concatenate([-sin, zh, zeros], axis=1)
    rp = np.concatenate([zh, sin, zeros], axis=1)
    tile = lambda t: jnp.asarray(np.tile(t, (1, LANES // HEAD_DIM)), dtype=F32)
    return tile(ra), tile(rm), tile(rp)


def _tiles(seq):
    tm, tp, tq_sb, tq_da = 1024, 512, 256, 512
    for t in (tm, tp, tq_sb, tq_da):
        assert seq % t == 0 and t % LANES == 0 and t % CHUNK == 0
    return tm, tp, tq_sb, tq_da


def kernel(x, g_mix, w_in, g_q, g_k, lam_q1, lam_k1, lam_q2, lam_k2, g_sub, w_branch_a, w_branch_b,
           w_gate, b_gate, w_out, g_ffn, w_ffn_gate, w_ffn_up, w_ffn_down):
    b, s, d = x.shape
    depth = g_mix.shape[0]
    sb_w = SB_HEADS * HEAD_DIM
    da_w = DA_HEADS * 2 * HEAD_DIM
    in_w = 3 * sb_w + 3 * da_w
    n = b * s

    tm, tp, tq_sb, tq_da = _tiles(s)

    ra, rm, rp = _rope_tables(s)
    grp = np.arange(da_w) // HEAD_DIM
    gmat = jnp.asarray(np.where(grp[:, None] == grp[None, :], 1.0 / HEAD_DIM, 0.0), dtype=BF16)
    tile_g = lambda g: jnp.tile(g, da_w // HEAD_DIM)[None, :]
    vmem_full = pl.BlockSpec(memory_space=pltpu.VMEM)
    spt = s // tm

    xf = x.reshape(n, d)
    for layer in range(depth):
        lambda_init = 0.8 - 0.6 * math.exp(-0.3 * layer)
        gmix = g_mix[layer][None, :]

        proj = pl.pallas_call(
            functools.partial(_proj_kernel, sb_w=sb_w, da_w=da_w),
            grid=(n // tm,),
            in_specs=[
                pl.BlockSpec((tm, d), lambda t: (t, 0)),
                vmem_full, vmem_full, vmem_full, vmem_full, vmem_full,
                pl.BlockSpec((tm, LANES), lambda t: (t % spt, 0)),
                pl.BlockSpec((tm, LANES), lambda t: (t % spt, 0)),
                pl.BlockSpec((tm, LANES), lambda t: (t % spt, 0)),
            ],
            out_specs=pl.BlockSpec((tm, in_w), lambda t: (t, 0)),
            out_shape=jax.ShapeDtypeStruct((n, in_w), BF16),
            compiler_params=pltpu.CompilerParams(
                dimension_semantics=("arbitrary",), vmem_limit_bytes=VMEM_LIMIT),
            name="proj",
        )(xf, gmix, w_in[layer].astype(BF16), tile_g(g_q[layer]), tile_g(g_k[layer]), gmat,
          ra, rm, rp)

        cb = lambda off: off // LANES
        tq, gw, ns = tq_sb, SB_GROUPS * LANES, SB_TILES
        nq = s // (ns * tq)
        o_a = pl.pallas_call(
            functools.partial(_sb_kernel, tq=tq, n_grp=SB_GROUPS, n_seg=ns),
            grid=(b, sb_w // gw, nq),
            in_specs=[
                pl.BlockSpec((ns * tq, gw), lambda bi, j, i, nq=nq: (bi * nq + i, j)),
                pl.BlockSpec((s, gw), lambda bi, j, i: (bi, sb_w // gw + j)),
                pl.BlockSpec((s, gw), lambda bi, j, i: (bi, 2 * sb_w // gw + j)),
            ],
            out_specs=pl.BlockSpec((ns * tq, gw), lambda bi, j, i, nq=nq: (bi * nq + i, j)),
            out_shape=jax.ShapeDtypeStruct((n, sb_w), BF16),
            scratch_shapes=[pltpu.VMEM((ns, SB_GROUPS, 2 * tq, LANES), BF16),
                            pltpu.VMEM((ns, SB_GROUPS, 2 * tq, LANES), F32),
                            pltpu.VMEM((ns, SB_GROUPS, 2 * tq, LANES), F32)],
            compiler_params=pltpu.CompilerParams(
                dimension_semantics=("arbitrary", "arbitrary", "arbitrary"),
                vmem_limit_bytes=VMEM_LIMIT),
            name="sb_attn",
        )(proj, proj, proj)

        tq = tq_da
        post_w = [w_gate[layer], w_branch_a[layer], w_branch_b[layer], w_out[layer],
                  w_ffn_gate[layer], w_ffn_up[layer], w_ffn_down[layer]]
        n_steps = b * DA_HEADS
        for w in post_w:
            assert w.shape[0] % (16 * n_steps) == 0
        w_spec = lambda w: pl.BlockSpec((w.shape[0] // n_steps, w.shape[1]),
                                        lambda bi, j: (bi * DA_HEADS + j, 0))
        o_b, *post_w16 = pl.pallas_call(
            functools.partial(_da_flat_kernel, tq=tq, n_tiles=s // tq, n_cast=len(post_w),
                              lambda_init=lambda_init),
            grid=(b, DA_HEADS),
            in_specs=[
                pl.BlockSpec((s, LANES), lambda bi, j: (bi, cb(3 * sb_w) + j)),
                pl.BlockSpec((s, LANES), lambda bi, j: (bi, cb(3 * sb_w + da_w) + j)),
                pl.BlockSpec((s, LANES), lambda bi, j: (bi, cb(3 * sb_w + 2 * da_w) + j)),
                vmem_full, vmem_full, vmem_full, vmem_full, vmem_full,
            ] + [w_spec(w) for w in post_w],
            out_specs=[pl.BlockSpec((s, LANES), lambda bi, j: (bi, j))]
            + [w_spec(w) for w in post_w],
            out_shape=[jax.ShapeDtypeStruct((n, da_w), BF16)]
            + [jax.ShapeDtypeStruct(w.shape, BF16) for w in post_w],
            scratch_shapes=[pltpu.VMEM((s, 2 * LANES), BF16),
                            pltpu.VMEM((2, 2 * tq, LANES), BF16),
                            pltpu.VMEM((2, 2 * tq, tq), F32)]
            + [pltpu.VMEM((2 * tq, LANES), F32)] * 3,
            compiler_params=pltpu.CompilerParams(
                dimension_semantics=("arbitrary", "arbitrary"),
                vmem_limit_bytes=VMEM_LIMIT),
            name="da_attn",
        )(proj, proj, proj, lam_q1[layer][None, :], lam_k1[layer][None, :],
          lam_q2[layer][None, :], lam_k2[layer][None, :], g_sub[layer][None, :], *post_w)
        wgate16, wa16, wb16, wout16, wfg16, wfu16, wfd16 = post_w16

        xf = pl.pallas_call(
            functools.partial(_post_kernel, d_model=d),
            grid=(n // tp,),
            in_specs=[
                pl.BlockSpec((tp, d), lambda t: (t, 0)),
                pl.BlockSpec((tp, sb_w), lambda t: (t, 0)),
                pl.BlockSpec((tp, da_w), lambda t: (t, 0)),
            ] + [vmem_full] * 10,
            out_specs=pl.BlockSpec((tp, d), lambda t: (t, 0)),
            out_shape=jax.ShapeDtypeStruct((n, d), F32),
            compiler_params=pltpu.CompilerParams(
                dimension_semantics=("arbitrary",), vmem_limit_bytes=VMEM_LIMIT),
            name="post",
        )(xf, o_a, o_b, gmix, wgate16, b_gate[layer][None, :], wa16, wb16, wout16,
          g_ffn[layer][None, :], wfg16, wfu16, wfd16)
    return xf.reshape(b, s, d)
```

```python
import functools
import math

import jax
import jax.numpy as jnp
import numpy as np
from jax import lax
from jax.experimental import pallas as pl
from jax.experimental.pallas import tpu as pltpu

F32 = jnp.float32
BF16 = jnp.bfloat16

CHUNK = 64
SB_HEADS = 8
DA_HEADS = 4
HEAD_DIM = 64
ROPE_THETA = 500000.0
ROPE_DIM = HEAD_DIM // 4
EPS = 1e-6
NEG_INF = -1e30
LOG2E = math.log2(math.e)
SB_GROUPS = 4
SB_TILES = 4
SB_STOP_LOG2 = 152.0
SB_EXP2_MAX = 126.0
LANES = 128
MXU_DEPTH = 256
VMEM_LIMIT = 56 * 1024 * 1024


def _dot(a, b):
    return jnp.dot(a, b, preferred_element_type=F32)


def _dot_nt(a, b):
    return lax.dot_general(a, b, (((1,), (1,)), ((), ())), preferred_element_type=F32)


def _proj_kernel(x_ref, gmix_ref, win_ref, gq_ref, gk_ref, gmat_ref,
                 ra_ref, rm_ref, rp_ref, out_ref, *, sb_w, da_w):
    x = x_ref[...]
    ms = jnp.mean(x * x, axis=-1, keepdims=True)
    h = (x * lax.rsqrt(ms + EPS) * gmix_ref[...]).astype(BF16)
    scale = HEAD_DIM ** -0.5
    o = 3 * sb_w
    ra, rm, rp = ra_ref[...], rm_ref[...], rp_ref[...]

    def qk_norm_rope(t, g, mult):
        msq = _dot((t * t).astype(BF16), gmat_ref[...])
        tn = t * lax.rsqrt(msq + EPS) * g
        cols = []
        for j in range(da_w // LANES):
            c = tn[:, j * LANES:(j + 1) * LANES]
            r = (c * ra + pltpu.roll(c, LANES - ROPE_DIM // 2, 1) * rm
                 + pltpu.roll(c, ROPE_DIM // 2, 1) * rp)
            cols.append((r * mult).astype(BF16))
        return jnp.concatenate(cols, axis=1)

    qk = _dot(h, win_ref[:, o:o + 2 * da_w])
    out_ref[:, o:o + da_w] = qk_norm_rope(qk[:, 0:da_w], gq_ref[...], scale * LOG2E)
    out_ref[:, o + da_w:o + 2 * da_w] = qk_norm_rope(qk[:, da_w:2 * da_w], gk_ref[...], 1.0)
    sb = _dot(h, win_ref[:, 0:o])
    out_ref[:, 0:sb_w] = (sb[:, 0:sb_w] * (scale * LOG2E)).astype(BF16)
    out_ref[:, sb_w:o] = sb[:, sb_w:o].astype(BF16)
    out_ref[:, o + 2 * da_w:o + 3 * da_w] = _dot(h, win_ref[:, o + 2 * da_w:o + 3 * da_w]).astype(BF16)


def _stack_masked(q, n_parts):
    lane = lax.broadcasted_iota(jnp.int32, q.shape, 1)
    zero = jnp.zeros_like(q)
    return jnp.concatenate(
        [jnp.where((lane >= p * HEAD_DIM) & (lane < (p + 1) * HEAD_DIM), q, zero)
         for p in range(n_parts)], axis=0)


def _sb_kernel(q_ref, k_ref, v_ref, o_ref, qs_ref, acc_ref, run_ref, *, tq, n_grp, n_seg):
    first_tile = pl.program_id(2) * n_seg
    m2 = 2 * tq
    r2 = lax.broadcasted_iota(jnp.int32, (tq, tq), 0)
    c2 = lax.broadcasted_iota(jnp.int32, (tq, tq), 1)
    tri = jnp.where(r2 > c2, 1.0, 0.0).astype(BF16)
    lanes = lambda g: slice(g * LANES, (g + 1) * LANES)
    rows = lambda seg: slice(seg * tq, (seg + 1) * tq)

    def block(seg, g, blk, run, diagonal=False, gate=None):
        ks = pl.multiple_of(blk * tq, tq)
        z = _dot_nt(qs_ref[seg, g], k_ref[pl.ds(ks, tq), lanes(g)])
        if diagonal:
            row = lax.broadcasted_iota(jnp.int32, (m2, tq), 0)
            col = lax.broadcasted_iota(jnp.int32, (m2, tq), 1)
            z = jnp.where(col < jnp.where(row >= tq, row - tq, row), z, NEG_INF)
        if gate is not None:
            z = jnp.where(gate > 0.0, z, NEG_INF)
        p = jnp.maximum(z, jnp.log2(1.0 + jnp.exp2(jnp.minimum(z, SB_EXP2_MAX))))
        after = _dot(p.astype(BF16), tri)
        spent = after + jnp.concatenate([run] * (tq // LANES), axis=1)
        run = run + jnp.broadcast_to(after[:, 0:1] + p[:, 0:1], (m2, LANES))
        a = jnp.exp2((z - p) - spent)
        return run, _dot(a.astype(BF16), v_ref[pl.ds(ks, tq), lanes(g)])

    zeros = jnp.zeros((m2, LANES), F32)
    least0 = []
    diag = {}
    for seg in range(n_seg):
        for g in range(n_grp):
            qs_ref[seg, g] = _stack_masked(q_ref[rows(seg), lanes(g)], 2)
            diag[seg, g] = block(seg, g, first_tile + seg, zeros, diagonal=True)
    for seg in range(n_seg):
        i = first_tile + seg
        least = None
        for g in range(n_grp):
            run, out = diag[seg, g]
            if seg == 0:
                run, out2 = block(seg, g, jnp.maximum(i - 1, 0), run,
                                  gate=jnp.where(i > 0, 1.0, 0.0).astype(F32))
            else:
                run, out2 = block(seg, g, i - 1, run)
            run_ref[seg, g] = run
            acc_ref[seg, g] = out + out2
            least = jnp.min(run) if least is None else jnp.minimum(least, jnp.min(run))
        least0.append(least)

    def more(carry):
        blk, least = carry
        return jnp.logical_and(blk >= 0, least < SB_STOP_LOG2)

    for seg in range(n_seg):
        def step(carry, seg=seg):
            blk, _ = carry
            least = None
            for g in range(n_grp):
                run, out = block(seg, g, blk, run_ref[seg, g])
                run_ref[seg, g] = run
                acc_ref[seg, g] += out
                least = jnp.min(run) if least is None else jnp.minimum(least, jnp.min(run))
            return blk - 1, least

        lax.while_loop(more, step, (first_tile + seg - 2, least0[seg]))

    lane = lax.broadcasted_iota(jnp.int32, (tq, LANES), 1)
    for seg in range(n_seg):
        for g in range(n_grp):
            o_ref[rows(seg), lanes(g)] = jnp.where(
                lane < HEAD_DIM, acc_ref[seg, g, 0:tq, :], acc_ref[seg, g, tq:m2, :]
            ).astype(o_ref.dtype)


def _da_flat_kernel(q_ref, k_ref, v_ref, lq1_ref, lk1_ref, lq2_ref, lk2_ref, gsub_ref, *refs,
                    tq, n_tiles, n_cast, lambda_init):
    w32_refs, o_ref, w16_refs = refs[:n_cast], refs[n_cast], refs[n_cast + 1:2 * n_cast + 1]
    vext_ref, qs_ref, s_ref, m_ref, l_ref, acc_ref = refs[2 * n_cast + 1:]
    for w32, w16 in zip(w32_refs, w16_refs):
        w16[...] = w32[...].astype(w16.dtype)

    m2 = 2 * tq
    vext_ref[:, 0:LANES] = v_ref[...]
    vext_ref[:, LANES:2 * LANES] = jnp.ones(v_ref.shape, v_ref.dtype)
    lam = (jnp.exp(jnp.sum(lq1_ref[...] * lk1_ref[...], axis=-1, keepdims=True))
           - jnp.exp(jnp.sum(lq2_ref[...] * lk2_ref[...], axis=-1, keepdims=True))
           + lambda_init)
    steps = [(i, blk) for i in range(n_tiles) for blk in range(i + 1)]
    rows = lambda j: slice(j * tq, (j + 1) * tq)

    h = tq // 2
    quarter = (slice(0, h), slice(h, tq), slice(tq, tq + h), slice(tq + h, m2))
    upper = lambda x: jnp.concatenate([x[quarter[0]], x[quarter[2]]], axis=0)
    lower = lambda x: jnp.concatenate([x[quarter[1]], x[quarter[3]]], axis=0)
    rowmax = lambda s: jnp.broadcast_to(jnp.max(s, axis=1, keepdims=True), (s.shape[0], LANES))

    def scores(t):
        i, blk = steps[t]
        if blk == 0:
            qs_ref[i % 2] = _stack_masked(q_ref[rows(i), :], 2)
        q = qs_ref[i % 2]
        if blk < i:
            s = _dot_nt(q, k_ref[rows(blk), :])
            s_ref[t % 2] = s
        else:
            k0 = blk * tq
            s_ref[t % 2, :, 0:h] = _dot_nt(q, k_ref[k0:k0 + h, :])
            s2 = _dot_nt(lower(q), k_ref[k0 + h:k0 + tq, :])
            s_ref[t % 2, quarter[1], h:tq] = s2[0:h]
            s_ref[t % 2, quarter[3], h:tq] = s2[h:tq]

    def update(s, m_cur, v, prev):
        m_new = m_cur if prev is None else jnp.maximum(prev[0], m_cur)
        p = jnp.exp2(s - jnp.concatenate([m_new] * (s.shape[1] // LANES), axis=1))
        pv = _dot(p.astype(BF16), v)
        if prev is None:
            return m_new, pv[:, LANES:], pv[:, :LANES]
        alpha = jnp.exp2(prev[0] - m_new)
        return m_new, alpha * prev[1] + pv[:, LANES:], alpha * prev[2] + pv[:, :LANES]

    def absorb(t):
        i, blk = steps[t]
        prev = None if blk == 0 else (m_ref[...], l_ref[...], acc_ref[...])
        if blk < i:
            s = s_ref[t % 2]
            m_ref[...], l_ref[...], acc_ref[...] = update(
                s, rowmax(s), vext_ref[rows(blk), :], prev)
            return
        k0 = blk * tq
        row = lax.broadcasted_iota(jnp.int32, (tq, h), 0)
        col = lax.broadcasted_iota(jnp.int32, (tq, h), 1)
        seen = (col // CHUNK) <= (jnp.where(row >= h, row - h, row) // CHUNK)
        s1 = s_ref[t % 2, :, 0:h]
        s_up = jnp.where(seen, upper(s1), NEG_INF)
        s2 = jnp.concatenate([s_ref[t % 2, quarter[1], h:tq], s_ref[t % 2, quarter[3], h:tq]],
                             axis=0)
        s_lo = jnp.concatenate([lower(s1), jnp.where(seen, s2, NEG_INF)], axis=1)
        _, l_up, acc_up = update(s_up, rowmax(s_up), vext_ref[k0:k0 + h, :],
                                 None if prev is None else tuple(upper(x) for x in prev))
        _, l_lo, acc_lo = update(s_lo, rowmax(s_lo), vext_ref[k0:k0 + tq, :],
                                 None if prev is None else tuple(lower(x) for x in prev))
        res_up, res_lo = acc_up / l_up, acc_lo / l_lo
        res = [jnp.concatenate([res_up[c * h:(c + 1) * h], res_lo[c * h:(c + 1) * h]], axis=0)
               for c in range(2)]
        o = res[0] - lam * res[1]
        ms = jnp.mean(o * o, axis=-1, keepdims=True)
        o = o * lax.rsqrt(ms + EPS) * gsub_ref[...] * (1.0 - lambda_init)
        o_ref[rows(i), :] = o.astype(o_ref.dtype)

    scores(0)
    for t in range(len(steps)):
        if t + 1 < len(steps):
            scores(t + 1)
        absorb(t)


def _post_kernel(x_ref, oa_ref, ob_ref, gmix_ref, wgate_ref, bgate_ref, wa_ref, wb_ref, wout_ref,
                 gffn_ref, wfg_ref, wfu_ref, wfd_ref, out_ref, *, d_model):
    x = x_ref[...]
    ms = jnp.mean(x * x, axis=-1, keepdims=True)
    h = (x * lax.rsqrt(ms + EPS) * gmix_ref[...]).astype(BF16)
    gates = jax.nn.sigmoid(_dot(h, wgate_ref[...]) + bgate_ref[...])
    br_a = _dot(oa_ref[...], wa_ref[...])
    br_b = _dot(ob_ref[...], wb_ref[...])
    merged = gates[:, :d_model] * br_a + gates[:, d_model:] * br_b
    x1 = x + _dot(merged.astype(BF16), wout_ref[...])

    ms2 = jnp.mean(x1 * x1, axis=-1, keepdims=True)
    h2 = (x1 * lax.rsqrt(ms2 + EPS) * gffn_ref[...]).astype(BF16)
    d_ff = wfg_ref.shape[1]
    cut = (d_ff // MXU_DEPTH + 1) // 2 * MXU_DEPTH
    out = x1
    for lo, hi in ((0, cut), (cut, d_ff)):
        fg = _dot(h2, wfg_ref[:, lo:hi])
        fu = _dot(h2, wfu_ref[:, lo:hi])
        ff = (fg * jax.nn.sigmoid(fg)) * fu
        out = out + _dot(ff.astype(BF16), wfd_ref[lo:hi, :])
    out_ref[...] = out


def _rope_tables(seq):
    half = ROPE_DIM // 2
    pos = np.arange(seq, dtype=np.float64)
    inv_freq = ROPE_THETA ** (-np.arange(0, ROPE_DIM, 2, dtype=np.float64) / ROPE_DIM)
    ang = pos[:, None] * inv_freq[None, :]
    cos, sin = np.cos(ang), np.sin(ang)
    ones = np.ones((seq, HEAD_DIM - ROPE_DIM))
    zeros = np.zeros((seq, HEAD_DIM - ROPE_DIM))
    zh = np.zeros((seq, half))
    ra = np.concatenate([cos, cos, ones], axis=1)
    rm = np.concatenate([-sin, zh, zeros], axis=1)
    rp = np.concatenate([zh, sin, zeros], axis=1)
    tile = lambda t: jnp.asarray(np.tile(t, (1, LANES // HEAD_DIM)), dtype=F32)
    return tile(ra), tile(rm), tile(rp)


def _tiles(seq):
    tm, tp, tq_sb, tq_da = 1024, 512, 256, 512
    for t in (tm, tp, tq_sb, tq_da):
        assert seq % t == 0 and t % LANES == 0 and t % CHUNK == 0
    return tm, tp, tq_sb, tq_da


def kernel(x, g_mix, w_in, g_q, g_k, lam_q1, lam_k1, lam_q2, lam_k2, g_sub, w_branch_a, w_branch_b,
           w_gate, b_gate, w_out, g_ffn, w_ffn_gate, w_ffn_up, w_ffn_down):
    b, s, d = x.shape
    depth = g_mix.shape[0]
    sb_w = SB_HEADS * HEAD_DIM
    da_w = DA_HEADS * 2 * HEAD_DIM
    in_w = 3 * sb_w + 3 * da_w
    n = b * s

    tm, tp, tq_sb, tq_da = _tiles(s)

    ra, rm, rp = _rope_tables(s)
    grp = np.arange(da_w) // HEAD_DIM
    gmat = jnp.asarray(np.where(grp[:, None] == grp[None, :], 1.0 / HEAD_DIM, 0.0), dtype=BF16)
    tile_g = lambda g: jnp.tile(g, da_w // HEAD_DIM)[None, :]
    vmem_full = pl.BlockSpec(memory_space=pltpu.VMEM)
    spt = s // tm

    xf = x.reshape(n, d)
    for layer in range(depth):
        lambda_init = 0.8 - 0.6 * math.exp(-0.3 * layer)
        gmix = g_mix[layer][None, :]

        proj = pl.pallas_call(
            functools.partial(_proj_kernel, sb_w=sb_w, da_w=da_w),
            grid=(n // tm,),
            in_specs=[
                pl.BlockSpec((tm, d), lambda t: (t, 0)),
                vmem_full, vmem_full, vmem_full, vmem_full, vmem_full,
                pl.BlockSpec((tm, LANES), lambda t: (t % spt, 0)),
                pl.BlockSpec((tm, LANES), lambda t: (t % spt, 0)),
                pl.BlockSpec((tm, LANES), lambda t: (t % spt, 0)),
            ],
            out_specs=pl.BlockSpec((tm, in_w), lambda t: (t, 0)),
            out_shape=jax.ShapeDtypeStruct((n, in_w), BF16),
            compiler_params=pltpu.CompilerParams(
                dimension_semantics=("arbitrary",), vmem_limit_bytes=VMEM_LIMIT),
            name="proj",
        )(xf, gmix, w_in[layer].astype(BF16), tile_g(g_q[layer]), tile_g(g_k[layer]), gmat,
          ra, rm, rp)

        cb = lambda off: off // LANES
        tq, gw, ns = tq_sb, SB_GROUPS * LANES, SB_TILES
        nq = s // (ns * tq)
        o_a = pl.pallas_call(
            functools.partial(_sb_kernel, tq=tq, n_grp=SB_GROUPS, n_seg=ns),
            grid=(b, sb_w // gw, nq),
            in_specs=[
                pl.BlockSpec((ns * tq, gw), lambda bi, j, i, nq=nq: (bi * nq + i, j)),
                pl.BlockSpec((s, gw), lambda bi, j, i: (bi, sb_w // gw + j)),
                pl.BlockSpec((s, gw), lambda bi, j, i: (bi, 2 * sb_w // gw + j)),
            ],
            out_specs=pl.BlockSpec((ns * tq, gw), lambda bi, j, i, nq=nq: (bi * nq + i, j)),
            out_shape=jax.ShapeDtypeStruct((n, sb_w), BF16),
            scratch_shapes=[pltpu.VMEM((ns, SB_GROUPS, 2 * tq, LANES), BF16),
                            pltpu.VMEM((ns, SB_GROUPS, 2 * tq, LANES), F32),
                            pltpu.VMEM((ns, SB_GROUPS, 2 * tq, LANES), F32)],
            compiler_params=pltpu.CompilerParams(
                dimension_semantics=("arbitrary", "arbitrary", "arbitrary"),
                vmem_limit_bytes=VMEM_LIMIT),
            name="sb_attn",
        )(proj, proj, proj)

        tq = tq_da
        post_w = [w_gate[layer], w_branch_a[layer], w_branch_b[layer], w_out[layer],
                  w_ffn_gate[layer], w_ffn_up[layer], w_ffn_down[layer]]
        n_steps = b * DA_HEADS
        for w in post_w:
            assert w.shape[0] % (16 * n_steps) == 0
        w_spec = lambda w: pl.BlockSpec((w.shape[0] // n_steps, w.shape[1]),
                                        lambda bi, j: (bi * DA_HEADS + j, 0))
        o_b, *post_w16 = pl.pallas_call(
            functools.partial(_da_flat_kernel, tq=tq, n_tiles=s // tq, n_cast=len(post_w),
                              lambda_init=lambda_init),
            grid=(b, DA_HEADS),
            in_specs=[
                pl.BlockSpec((s, LANES), lambda bi, j: (bi, cb(3 * sb_w) + j)),
                pl.BlockSpec((s, LANES), lambda bi, j: (bi, cb(3 * sb_w + da_w) + j)),
                pl.BlockSpec((s, LANES), lambda bi, j: (bi, cb(3 * sb_w + 2 * da_w) + j)),
                vmem_full, vmem_full, vmem_full, vmem_full, vmem_full,
            ] + [w_spec(w) for w in post_w],
            out_specs=[pl.BlockSpec((s, LANES), lambda bi, j: (bi, j))]
            + [w_spec(w) for w in post_w],
            out_shape=[jax.ShapeDtypeStruct((n, da_w), BF16)]
            + [jax.ShapeDtypeStruct(w.shape, BF16) for w in post_w],
            scratch_shapes=[pltpu.VMEM((s, 2 * LANES), BF16),
                            pltpu.VMEM((2, 2 * tq, LANES), BF16),
                            pltpu.VMEM((2, 2 * tq, tq), F32)]
            + [pltpu.VMEM((2 * tq, LANES), F32)] * 3,
            compiler_params=pltpu.CompilerParams(
                dimension_semantics=("arbitrary", "arbitrary"),
                vmem_limit_bytes=VMEM_LIMIT),
            name="da_attn",
        )(proj, proj, proj, lam_q1[layer][None, :], lam_k1[layer][None, :],
          lam_q2[layer][None, :], lam_k2[layer][None, :], g_sub[layer][None, :], *post_w)
        wgate16, wa16, wb16, wout16, wfg16, wfu16, wfd16 = post_w16

        xf = pl.pallas_call(
            functools.partial(_post_kernel, d_model=d),
            grid=(n // tp,),
            in_specs=[
                pl.BlockSpec((tp, d), lambda t: (t, 0)),
                pl.BlockSpec((tp, sb_w), lambda t: (t, 0)),
                pl.BlockSpec((tp, da_w), lambda t: (t, 0)),
            ] + [vmem_full] * 10,
            out_specs=pl.BlockSpec((tp, d), lambda t: (t, 0)),
            out_shape=jax.ShapeDtypeStruct((n, d), F32),
            compiler_params=pltpu.CompilerParams(
                dimension_semantics=("arbitrary",), vmem_limit_bytes=VMEM_LIMIT),
            name="post",
        )(xf, o_a, o_b, gmix, wgate16, b_gate[layer][None, :], wa16, wb16, wout16,
          g_ffn[layer][None, :], wfg16, wfu16, wfd16)
    return xf.reshape(b, s, d)
```

```python
import functools
import math

import jax
import jax.numpy as jnp
import numpy as np
from jax import lax
from jax.experimental import pallas as pl
from jax.experimental.pallas import tpu as pltpu

F32 = jnp.float32
BF16 = jnp.bfloat16

CHUNK = 64
SB_HEADS = 8
DA_HEADS = 4
HEAD_DIM = 64
ROPE_THETA = 500000.0
ROPE_DIM = HEAD_DIM // 4
EPS = 1e-6
NEG_INF = -1e30
LOG2E = math.log2(math.e)
SB_GROUPS = 4
SB_TILES = 4
SB_STOP_LOG2 = 152.0
SB_EXP2_MAX = 126.0
LANES = 128
VMEM_LIMIT = 56 * 1024 * 1024


def _dot(a, b):
    return jnp.dot(a, b, preferred_element_type=F32)


def _dot_nt(a, b):
    return lax.dot_general(a, b, (((1,), (1,)), ((), ())), preferred_element_type=F32)


def _proj_kernel(x_ref, gmix_ref, win_ref, gq_ref, gk_ref, gmat_ref,
                 ra_ref, rm_ref, rp_ref, out_ref, *, sb_w, da_w):
    x = x_ref[...]
    ms = jnp.mean(x * x, axis=-1, keepdims=True)
    h = (x * lax.rsqrt(ms + EPS) * gmix_ref[...]).astype(BF16)
    scale = HEAD_DIM ** -0.5
    o = 3 * sb_w
    ra, rm, rp = ra_ref[...], rm_ref[...], rp_ref[...]

    def qk_norm_rope(t, g, mult):
        msq = _dot((t * t).astype(BF16), gmat_ref[...])
        tn = t * lax.rsqrt(msq + EPS) * g
        cols = []
        for j in range(da_w // LANES):
            c = tn[:, j * LANES:(j + 1) * LANES]
            r = (c * ra + pltpu.roll(c, LANES - ROPE_DIM // 2, 1) * rm
                 + pltpu.roll(c, ROPE_DIM // 2, 1) * rp)
            cols.append((r * mult).astype(BF16))
        return jnp.concatenate(cols, axis=1)

    qk = _dot(h, win_ref[:, o:o + 2 * da_w])
    out_ref[:, o:o + da_w] = qk_norm_rope(qk[:, 0:da_w], gq_ref[...], scale * LOG2E)
    out_ref[:, o + da_w:o + 2 * da_w] = qk_norm_rope(qk[:, da_w:2 * da_w], gk_ref[...], 1.0)
    sb = _dot(h, win_ref[:, 0:o])
    out_ref[:, 0:sb_w] = (sb[:, 0:sb_w] * (scale * LOG2E)).astype(BF16)
    out_ref[:, sb_w:o] = sb[:, sb_w:o].astype(BF16)
    out_ref[:, o + 2 * da_w:o + 3 * da_w] = _dot(h, win_ref[:, o + 2 * da_w:o + 3 * da_w]).astype(BF16)


def _stack_masked(q, n_parts):
    lane = lax.broadcasted_iota(jnp.int32, q.shape, 1)
    zero = jnp.zeros_like(q)
    return jnp.concatenate(
        [jnp.where((lane >= p * HEAD_DIM) & (lane < (p + 1) * HEAD_DIM), q, zero)
         for p in range(n_parts)], axis=0)


def _sb_kernel(q_ref, k_ref, v_ref, o_ref, qs_ref, acc_ref, run_ref, *, tq, n_grp, n_seg):
    first_tile = pl.program_id(2) * n_seg
    m2 = 2 * tq
    r2 = lax.broadcasted_iota(jnp.int32, (tq, tq), 0)
    c2 = lax.broadcasted_iota(jnp.int32, (tq, tq), 1)
    tri = jnp.where(r2 > c2, 1.0, 0.0).astype(BF16)
    lanes = lambda g: slice(g * LANES, (g + 1) * LANES)
    rows = lambda seg: slice(seg * tq, (seg + 1) * tq)

    def block(seg, g, blk, run, diagonal=False, gate=None):
        ks = pl.multiple_of(blk * tq, tq)
        z = _dot_nt(qs_ref[seg, g], k_ref[pl.ds(ks, tq), lanes(g)])
        if diagonal:
            row = lax.broadcasted_iota(jnp.int32, (m2, tq), 0)
            col = lax.broadcasted_iota(jnp.int32, (m2, tq), 1)
            z = jnp.where(col < jnp.where(row >= tq, row - tq, row), z, NEG_INF)
        if gate is not None:
            z = jnp.where(gate > 0.0, z, NEG_INF)
        p = jnp.maximum(z, jnp.log2(1.0 + jnp.exp2(jnp.minimum(z, SB_EXP2_MAX))))
        after = _dot(p.astype(BF16), tri)
        spent = after + jnp.concatenate([run] * (tq // LANES), axis=1)
        run = run + jnp.broadcast_to(after[:, 0:1] + p[:, 0:1], (m2, LANES))
        a = jnp.exp2((z - p) - spent)
        return run, _dot(a.astype(BF16), v_ref[pl.ds(ks, tq), lanes(g)])

    zeros = jnp.zeros((m2, LANES), F32)
    least0 = []
    diag = {}
    for seg in range(n_seg):
        for g in range(n_grp):
            qs_ref[seg, g] = _stack_masked(q_ref[rows(seg), lanes(g)], 2)
    for seg in range(n_seg):
        for g in range(n_grp):
            diag[seg, g] = block(seg, g, first_tile + seg, zeros, diagonal=True)
    for seg in range(n_seg):
        i = first_tile + seg
        least = None
        for g in range(n_grp):
            run, out = diag[seg, g]
            if seg == 0:
                run, out2 = block(seg, g, jnp.maximum(i - 1, 0), run,
                                  gate=jnp.where(i > 0, 1.0, 0.0).astype(F32))
            else:
                run, out2 = block(seg, g, i - 1, run)
            run_ref[seg, g] = run
            acc_ref[seg, g] = out + out2
            least = jnp.min(run) if least is None else jnp.minimum(least, jnp.min(run))
        least0.append(least)

    def more(carry):
        blk, least = carry
        return jnp.logical_and(blk >= 0, least < SB_STOP_LOG2)

    for seg in range(n_seg):
        def step(carry, seg=seg):
            blk, _ = carry
            least = None
            for g in range(n_grp):
                run, out = block(seg, g, blk, run_ref[seg, g])
                run_ref[seg, g] = run
                acc_ref[seg, g] += out
                least = jnp.min(run) if least is None else jnp.minimum(least, jnp.min(run))
            return blk - 1, least

        lax.while_loop(more, step, (first_tile + seg - 2, least0[seg]))

    lane = lax.broadcasted_iota(jnp.int32, (tq, LANES), 1)
    for seg in range(n_seg):
        for g in range(n_grp):
            o_ref[rows(seg), lanes(g)] = jnp.where(
                lane < HEAD_DIM, acc_ref[seg, g, 0:tq, :], acc_ref[seg, g, tq:m2, :]
            ).astype(o_ref.dtype)


def _da_flat_kernel(q_ref, k_ref, v_ref, lq1_ref, lk1_ref, lq2_ref, lk2_ref, gsub_ref, *refs,
                    tq, n_tiles, n_cast, lambda_init):
    w32_refs, o_ref, w16_refs = refs[:n_cast], refs[n_cast], refs[n_cast + 1:2 * n_cast + 1]
    vext_ref, qs_ref, s_ref, m_ref, l_ref, acc_ref = refs[2 * n_cast + 1:]
    for w32, w16 in zip(w32_refs, w16_refs):
        w16[...] = w32[...].astype(w16.dtype)

    m2 = 2 * tq
    vext_ref[:, 0:LANES] = v_ref[...]
    vext_ref[:, LANES:2 * LANES] = jnp.ones(v_ref.shape, v_ref.dtype)
    lam = (jnp.exp(jnp.sum(lq1_ref[...] * lk1_ref[...], axis=-1, keepdims=True))
           - jnp.exp(jnp.sum(lq2_ref[...] * lk2_ref[...], axis=-1, keepdims=True))
           + lambda_init)
    steps = [(i, blk) for i in range(n_tiles) for blk in range(i + 1)]
    rows = lambda j: slice(j * tq, (j + 1) * tq)

    h = tq // 2
    quarter = (slice(0, h), slice(h, tq), slice(tq, tq + h), slice(tq + h, m2))
    upper = lambda x: jnp.concatenate([x[quarter[0]], x[quarter[2]]], axis=0)
    lower = lambda x: jnp.concatenate([x[quarter[1]], x[quarter[3]]], axis=0)
    rowmax = lambda s: jnp.broadcast_to(jnp.max(s, axis=1, keepdims=True), (s.shape[0], LANES))

    def scores(t):
        i, blk = steps[t]
        if blk == 0:
            qs_ref[i % 2] = _stack_masked(q_ref[rows(i), :], 2)
        q = qs_ref[i % 2]
        if blk < i:
            s = _dot_nt(q, k_ref[rows(blk), :])
            s_ref[t % 2] = s
        else:
            k0 = blk * tq
            s_ref[t % 2, :, 0:h] = _dot_nt(q, k_ref[k0:k0 + h, :])
            s2 = _dot_nt(lower(q), k_ref[k0 + h:k0 + tq, :])
            s_ref[t % 2, quarter[1], h:tq] = s2[0:h]
            s_ref[t % 2, quarter[3], h:tq] = s2[h:tq]

    def update(s, m_cur, v, prev):
        m_new = m_cur if prev is None else jnp.maximum(prev[0], m_cur)
        p = jnp.exp2(s - jnp.concatenate([m_new] * (s.shape[1] // LANES), axis=1))
        pv = _dot(p.astype(BF16), v)
        if prev is None:
            return m_new, pv[:, LANES:], pv[:, :LANES]
        alpha = jnp.exp2(prev[0] - m_new)
        return m_new, alpha * prev[1] + pv[:, LANES:], alpha * prev[2] + pv[:, :LANES]

    def absorb(t):
        i, blk = steps[t]
        prev = None if blk == 0 else (m_ref[...], l_ref[...], acc_ref[...])
        if blk < i:
            s = s_ref[t % 2]
            m_ref[...], l_ref[...], acc_ref[...] = update(
                s, rowmax(s), vext_ref[rows(blk), :], prev)
            return
        k0 = blk * tq
        row = lax.broadcasted_iota(jnp.int32, (tq, h), 0)
        col = lax.broadcasted_iota(jnp.int32, (tq, h), 1)
        seen = (col // CHUNK) <= (jnp.where(row >= h, row - h, row) // CHUNK)
        s1 = s_ref[t % 2, :, 0:h]
        s_up = jnp.where(seen, upper(s1), NEG_INF)
        s2 = jnp.concatenate([s_ref[t % 2, quarter[1], h:tq], s_ref[t % 2, quarter[3], h:tq]],
                             axis=0)
        s_lo = jnp.concatenate([lower(s1), jnp.where(seen, s2, NEG_INF)], axis=1)
        _, l_up, acc_up = update(s_up, rowmax(s_up), vext_ref[k0:k0 + h, :],
                                 None if prev is None else tuple(upper(x) for x in prev))
        _, l_lo, acc_lo = update(s_lo, rowmax(s_lo), vext_ref[k0:k0 + tq, :],
                                 None if prev is None else tuple(lower(x) for x in prev))
        res_up, res_lo = acc_up / l_up, acc_lo / l_lo
        res = [jnp.concatenate([res_up[c * h:(c + 1) * h], res_lo[c * h:(c + 1) * h]], axis=0)
               for c in range(2)]
        o = res[0] - lam * res[1]
        ms = jnp.mean(o * o, axis=-1, keepdims=True)
        o = o * lax.rsqrt(ms + EPS) * gsub_ref[...] * (1.0 - lambda_init)
        o_ref[rows(i), :] = o.astype(o_ref.dtype)

    scores(0)
    for t in range(len(steps)):
        if t + 1 < len(steps):
            scores(t + 1)
        absorb(t)


def _post_kernel(x_ref, oa_ref, ob_ref, gmix_ref, wgate_ref, bgate_ref, wa_ref, wb_ref, wout_ref,
                 gffn_ref, wfg_ref, wfu_ref, wfd_ref, out_ref, *, d_model):
    x = x_ref[...]
    ms = jnp.mean(x * x, axis=-1, keepdims=True)
    h = (x * lax.rsqrt(ms + EPS) * gmix_ref[...]).astype(BF16)
    gates = jax.nn.sigmoid(_dot(h, wgate_ref[...]) + bgate_ref[...])
    br_a = _dot(oa_ref[...], wa_ref[...])
    br_b = _dot(ob_ref[...], wb_ref[...])
    merged = gates[:, :d_model] * br_a + gates[:, d_model:] * br_b
    x1 = x + _dot(merged.astype(BF16), wout_ref[...])

    ms2 = jnp.mean(x1 * x1, axis=-1, keepdims=True)
    h2 = (x1 * lax.rsqrt(ms2 + EPS) * gffn_ref[...]).astype(BF16)
    fg = _dot(h2, wfg_ref[...])
    fu = _dot(h2, wfu_ref[...])
    ff = (fg * jax.nn.sigmoid(fg)) * fu
    out_ref[...] = x1 + _dot(ff.astype(BF16), wfd_ref[...])


def _rope_tables(seq):
    half = ROPE_DIM // 2
    pos = np.arange(seq, dtype=np.float64)
    inv_freq = ROPE_THETA ** (-np.arange(0, ROPE_DIM, 2, dtype=np.float64) / ROPE_DIM)
    ang = pos[:, None] * inv_freq[None, :]
    cos, sin = np.cos(ang), np.sin(ang)
    ones = np.ones((seq, HEAD_DIM - ROPE_DIM))
    zeros = np.zeros((seq, HEAD_DIM - ROPE_DIM))
    zh = np.zeros((seq, half))
    ra = np.concatenate([cos, cos, ones], axis=1)
    rm = np.concatenate([-sin, zh, zeros], axis=1)
    rp = np.concatenate([zh, sin, zeros], axis=1)
    tile = lambda t: jnp.asarray(np.tile(t, (1, LANES // HEAD_DIM)), dtype=F32)
    return tile(ra), tile(rm), tile(rp)


def _tiles(seq):
    tm, tp, tq_sb, tq_da = 1024, 512, 256, 512
    for t in (tm, tp, tq_sb, tq_da):
        assert seq % t == 0 and t % LANES == 0 and t % CHUNK == 0
    return tm, tp, tq_sb, tq_da


def kernel(x, g_mix, w_in, g_q, g_k, lam_q1, lam_k1, lam_q2, lam_k2, g_sub, w_branch_a, w_branch_b,
           w_gate, b_gate, w_out, g_ffn, w_ffn_gate, w_ffn_up, w_ffn_down):
    b, s, d = x.shape
    depth = g_mix.shape[0]
    sb_w = SB_HEADS * HEAD_DIM
    da_w = DA_HEADS * 2 * HEAD_DIM
    in_w = 3 * sb_w + 3 * da_w
    n = b * s

    tm, tp, tq_sb, tq_da = _tiles(s)

    ra, rm, rp = _rope_tables(s)
    grp = np.arange(da_w) // HEAD_DIM
    gmat = jnp.asarray(np.where(grp[:, None] == grp[None, :], 1.0 / HEAD_DIM, 0.0), dtype=BF16)
    tile_g = lambda g: jnp.tile(g, da_w // HEAD_DIM)[None, :]
    vmem_full = pl.BlockSpec(memory_space=pltpu.VMEM)
    spt = s // tm

    xf = x.reshape(n, d)
    for layer in range(depth):
        lambda_init = 0.8 - 0.6 * math.exp(-0.3 * layer)
        gmix = g_mix[layer][None, :]

        proj = pl.pallas_call(
            functools.partial(_proj_kernel, sb_w=sb_w, da_w=da_w),
            grid=(n // tm,),
            in_specs=[
                pl.BlockSpec((tm, d), lambda t: (t, 0)),
                vmem_full, vmem_full, vmem_full, vmem_full, vmem_full,
                pl.BlockSpec((tm, LANES), lambda t: (t % spt, 0)),
                pl.BlockSpec((tm, LANES), lambda t: (t % spt, 0)),
                pl.BlockSpec((tm, LANES), lambda t: (t % spt, 0)),
            ],
            out_specs=pl.BlockSpec((tm, in_w), lambda t: (t, 0)),
            out_shape=jax.ShapeDtypeStruct((n, in_w), BF16),
            compiler_params=pltpu.CompilerParams(
                dimension_semantics=("arbitrary",), vmem_limit_bytes=VMEM_LIMIT),
            name="proj",
        )(xf, gmix, w_in[layer].astype(BF16), tile_g(g_q[layer]), tile_g(g_k[layer]), gmat,
          ra, rm, rp)

        cb = lambda off: off // LANES
        tq, gw, ns = tq_sb, SB_GROUPS * LANES, SB_TILES
        nq = s // (ns * tq)
        o_a = pl.pallas_call(
            functools.partial(_sb_kernel, tq=tq, n_grp=SB_GROUPS, n_seg=ns),
            grid=(b, sb_w // gw, nq),
            in_specs=[
                pl.BlockSpec((ns * tq, gw), lambda bi, j, i, nq=nq: (bi * nq + i, j)),
                pl.BlockSpec((s, gw), lambda bi, j, i: (bi, sb_w // gw + j)),
                pl.BlockSpec((s, gw), lambda bi, j, i: (bi, 2 * sb_w // gw + j)),
            ],
            out_specs=pl.BlockSpec((ns * tq, gw), lambda bi, j, i, nq=nq: (bi * nq + i, j)),
            out_shape=jax.ShapeDtypeStruct((n, sb_w), BF16),
            scratch_shapes=[pltpu.VMEM((ns, SB_GROUPS, 2 * tq, LANES), BF16),
                            pltpu.VMEM((ns, SB_GROUPS, 2 * tq, LANES), F32),
                            pltpu.VMEM((ns, SB_GROUPS, 2 * tq, LANES), F32)],
            compiler_params=pltpu.CompilerParams(
                dimension_semantics=("arbitrary", "arbitrary", "arbitrary"),
                vmem_limit_bytes=VMEM_LIMIT),
            name="sb_attn",
        )(proj, proj, proj)

        tq = tq_da
        post_w = [w_gate[layer], w_branch_a[layer], w_branch_b[layer], w_out[layer],
                  w_ffn_gate[layer], w_ffn_up[layer], w_ffn_down[layer]]
        n_steps = b * DA_HEADS
        for w in post_w:
            assert w.shape[0] % (16 * n_steps) == 0
        w_spec = lambda w: pl.BlockSpec((w.shape[0] // n_steps, w.shape[1]),
                                        lambda bi, j: (bi * DA_HEADS + j, 0))
        o_b, *post_w16 = pl.pallas_call(
            functools.partial(_da_flat_kernel, tq=tq, n_tiles=s // tq, n_cast=len(post_w),
                              lambda_init=lambda_init),
            grid=(b, DA_HEADS),
            in_specs=[
                pl.BlockSpec((s, LANES), lambda bi, j: (bi, cb(3 * sb_w) + j)),
                pl.BlockSpec((s, LANES), lambda bi, j: (bi, cb(3 * sb_w + da_w) + j)),
                pl.BlockSpec((s, LANES), lambda bi, j: (bi, cb(3 * sb_w + 2 * da_w) + j)),
                vmem_full, vmem_full, vmem_full, vmem_full, vmem_full,
            ] + [w_spec(w) for w in post_w],
            out_specs=[pl.BlockSpec((s, LANES), lambda bi, j: (bi, j))]
            + [w_spec(w) for w in post_w],
            out_shape=[jax.ShapeDtypeStruct((n, da_w), BF16)]
            + [jax.ShapeDtypeStruct(w.shape, BF16) for w in post_w],
            scratch_shapes=[pltpu.VMEM((s, 2 * LANES), BF16),
                            pltpu.VMEM((2, 2 * tq, LANES), BF16),
                            pltpu.VMEM((2, 2 * tq, tq), F32)]
            + [pltpu.VMEM((2 * tq, LANES), F32)] * 3,
            compiler_params=pltpu.CompilerParams(
                dimension_semantics=("arbitrary", "arbitrary"),
                vmem_limit_bytes=VMEM_LIMIT),
            name="da_attn",
        )(proj, proj, proj, lam_q1[layer][None, :], lam_k1[layer][None, :],
          lam_q2[layer][None, :], lam_k2[layer][None, :], g_sub[layer][None, :], *post_w)
        wgate16, wa16, wb16, wout16, wfg16, wfu16, wfd16 = post_w16

        xf = pl.pallas_call(
            functools.partial(_post_kernel, d_model=d),
            grid=(n // tp,),
            in_specs=[
                pl.BlockSpec((tp, d), lambda t: (t, 0)),
                pl.BlockSpec((tp, sb_w), lambda t: (t, 0)),
                pl.BlockSpec((tp, da_w), lambda t: (t, 0)),
            ] + [vmem_full] * 10,
            out_specs=pl.BlockSpec((tp, d), lambda t: (t, 0)),
            out_shape=jax.ShapeDtypeStruct((n, d), F32),
            compiler_params=pltpu.CompilerParams(
                dimension_semantics=("arbitrary",), vmem_limit_bytes=VMEM_LIMIT),
            name="post",
        )(xf, o_a, o_b, gmix, wgate16, b_gate[layer][None, :], wa16, wb16, wout16,
          g_ffn[layer][None, :], wfg16, wfu16, wfd16)
    return xf.reshape(b, s, d)
```
